```python
import math
import jax, jax.numpy as jnp
from jax import lax
import numpy as np

D_MODEL = 1024
BATCH = 4
SEQ = 8192
DEPTH = 4

HEAD_DIM = 64
NSA_HEADS = 8
NSA_GROUPS = 2
NSA_HPG = NSA_HEADS // NSA_GROUPS
CMP_LEN = 32
CMP_STRIDE = 16
CMP_HIDDEN = 256
SEL_BLK = 64
N_SEL = 16
WINDOW = 512
DSA_HEADS = 8
DSA_KV_RANK = 128
IDX_HEADS = 8
IDX_DIM = 64
DSA_TOPK = 256
DIFF_HEADS = 8
DIFF_DIM = 64
D_FF = 4 * D_MODEL
NUM_BUCKETS = 32
MAX_DISTANCE = 128
REL_HEADS = NSA_HEADS + DSA_HEADS
QBLK = 128
N_EVEN = (DEPTH + 1) // 2
N_ODD = DEPTH // 2
ALPHA = (2 * DEPTH) ** 0.25
BETA = (8 * DEPTH) ** -0.25
NEG_INF = -1e30
BIG = 1e9
EPS = 1e-5
NSA_Q = NSA_HEADS * HEAD_DIM
NSA_KV = 2 * NSA_GROUPS * HEAD_DIM
NSA_GATE = NSA_HEADS * 3
DSA_Q = DSA_HEADS * HEAD_DIM
EVEN_SIZES = (NSA_Q, NSA_KV, NSA_KV, NSA_KV, NSA_GATE, DSA_Q, DSA_KV_RANK, IDX_HEADS * IDX_DIM, IDX_DIM, IDX_HEADS)
EVEN_IN = sum(EVEN_SIZES)
EVEN_OUT = NSA_Q + DSA_HEADS * HEAD_DIM
ODD_IN = 3 * DIFF_HEADS * 2 * DIFF_DIM
ODD_OUT = DIFF_HEADS * 2 * DIFF_DIM

kernel_name = "hybrid_nsa_dsa_diff_deepnorm_trunk"


def _split(x, sizes):
    offs = np.cumsum(sizes)[:-1].tolist()
    return jnp.split(x, offs, axis=-1)


def layer_norm(x, g, b):
    xf = x.astype(jnp.float32)
    mu = jnp.mean(xf, axis=-1, keepdims=True)
    var = jnp.mean(jnp.square(xf - mu), axis=-1, keepdims=True)
    return ((xf - mu) * lax.rsqrt(var + EPS) * g + b).astype(x.dtype)


def rms_norm(x, g):
    xf = x.astype(jnp.float32)
    return (xf * lax.rsqrt(jnp.mean(jnp.square(xf), axis=-1, keepdims=True) + EPS) * g).astype(x.dtype)


def masked_softmax(logits, mask):
    z = jnp.where(mask, logits.astype(jnp.float32), NEG_INF)
    z = z - jnp.max(z, axis=-1, keepdims=True)
    e = jnp.where(mask, jnp.exp(z), 0.0)
    return e / jnp.maximum(jnp.sum(e, axis=-1, keepdims=True), 1e-30)


def t5_bucket(dist):
    n = jnp.maximum(dist, 0)
    exact = NUM_BUCKETS // 2
    nf = jnp.maximum(n, 1).astype(jnp.float32)
    large = exact + (jnp.log(nf / exact) / math.log(MAX_DISTANCE / exact) * (NUM_BUCKETS - exact)).astype(jnp.int32)
    return jnp.where(n < exact, n, jnp.minimum(large, NUM_BUCKETS - 1))


def sweep_query_blocks(fn, B, S):
    out = lax.map(fn, jnp.arange(S // QBLK))
    return jnp.moveaxis(out, 0, 1).reshape(B, S, -1)


def nsa_compress(k_raw, pe, w1, w2):
    B, S, G, Dh = k_raw.shape
    nc = (S - CMP_LEN) // CMP_STRIDE + 1
    idx = np.arange(nc)[:, None] * CMP_STRIDE + np.arange(CMP_LEN)[None, :]
    blk = k_raw[:, idx] + pe[:, None, :]
    blk = blk.transpose(0, 1, 3, 2, 4).reshape(B, nc, G, CMP_LEN * Dh)
    return jax.nn.gelu(blk @ w1) @ w2


def nsa_attention(q, kvc, kvs, kvw, gate_logits, pe_k, pe_v, w1_k, w2_k, w1_v, w2_v, bias_tab):
    B, S = q.shape[:2]
    G, HPG, Dh = NSA_GROUPS, NSA_HPG, HEAD_DIM
    q = q.reshape(B, S, G, HPG, Dh)
    kc_raw, vc_raw = [a.reshape(B, S, G, Dh) for a in jnp.split(kvc, 2, axis=-1)]
    kc = nsa_compress(kc_raw, pe_k, w1_k, w2_k)
    vc = nsa_compress(vc_raw, pe_v, w1_v, w2_v)
    ks, vs = [a.reshape(B, S, G, Dh).transpose(0, 2, 1, 3) for a in jnp.split(kvs, 2, axis=-1)]
    kw, vw = [jnp.pad(a.reshape(B, S, G, Dh), ((0, 0), (WINDOW, 0), (0, 0), (0, 0))) for a in jnp.split(kvw, 2, axis=-1)]
    gates = jax.nn.sigmoid(gate_logits.astype(jnp.float32)).astype(q.dtype).reshape(B, S, G, HPG, 3)
    nc = kc.shape[1]
    nb = S // SEL_BLK
    n_sel = min(N_SEL, nb)
    cs = np.arange(nc) * CMP_STRIDE
    ss = np.arange(nb) * SEL_BLK
    overlap = np.clip(np.minimum(cs[:, None] + CMP_LEN, ss[None, :] + SEL_BLK) - np.maximum(cs[:, None], ss[None, :]), 0, None)
    imp = jnp.asarray(overlap / CMP_STRIDE, dtype=jnp.float32)
    cmp_end = jnp.asarray(cs + CMP_LEN - 1, dtype=jnp.int32)
    tab = bias_tab.reshape(NUM_BUCKETS, G, HPG)
    scale = Dh ** -0.5
    bi = jnp.arange(B)[:, None, None, None]
    gi = jnp.arange(G)[None, :, None, None]
    jblk = jnp.arange(nb)

    def block(qb):
        q0 = qb * QBLK
        t = q0 + jnp.arange(QBLK)
        qq = lax.dynamic_slice_in_dim(q, q0, QBLK, axis=1)
        gg = lax.dynamic_slice_in_dim(gates, q0, QBLK, axis=1)
        dc = t[:, None] - cmp_end[None, :]
        bias_c = tab[t5_bucket(dc)].transpose(2, 3, 0, 1)
        lc = jnp.einsum('bqghd,bngd->bghqn', qq, kc) * scale + bias_c
        pc = masked_softmax(lc, dc >= 0)
        oc = jnp.einsum('bghqn,bngd->bqghd', pc.astype(vc.dtype), vc)
        score = jnp.einsum('bghqn,nj->bgqj', pc, imp)
        cb = (t // SEL_BLK)[:, None]
        allowed = jblk[None, :] * SEL_BLK <= t[:, None]
        forced = (jblk[None, :] == 0) | (jblk[None, :] == cb) | (jblk[None, :] == cb - 1)
        score = jnp.where(forced, BIG, jnp.where(allowed, score, -BIG))
        _, sel = lax.top_k(score, n_sel)
        tok = (sel[..., None] * SEL_BLK + jnp.arange(SEL_BLK)).reshape(B, G, QBLK, n_sel * SEL_BLK)
        k_sel = ks[bi, gi, tok]
        v_sel = vs[bi, gi, tok]
        dist = t[None, None, :, None] - tok
        bias_s = tab[t5_bucket(dist), gi].transpose(0, 1, 4, 2, 3)
        ls = jnp.einsum('bqghd,bgqtd->bghqt', qq, k_sel) * scale + bias_s
        ps = masked_softmax(ls, (dist >= 0)[:, :, None])
        o_s = jnp.einsum('bghqt,bgqtd->bqghd', ps.astype(v_sel.dtype), v_sel)
        kwb = lax.dynamic_slice_in_dim(kw, q0, WINDOW + QBLK, axis=1)
        vwb = lax.dynamic_slice_in_dim(vw, q0, WINDOW + QBLK, axis=1)
        s = q0 - WINDOW + jnp.arange(WINDOW + QBLK)
        dw = t[:, None] - s[None, :]
        mask_w = (dw >= 0) & (dw < WINDOW) & (s[None, :] >= 0)
        bias_w = tab[t5_bucket(dw)].transpose(2, 3, 0, 1)
        lw = jnp.einsum('bqghd,bsgd->bghqs', qq, kwb) * scale + bias_w
        pw = masked_softmax(lw, mask_w)
        o_w = jnp.einsum('bghqs,bsgd->bqghd', pw.astype(vwb.dtype), vwb)
        out = gg[..., 0:1] * oc + gg[..., 1:2] * o_s + gg[..., 2:3] * o_w
        return out.reshape(B, QBLK, G * HPG * Dh)

    return sweep_query_blocks(block, B, S)


def dsa_attention(q, kv_lat, iq, ik, iw, kv_norm, w_uk, w_uv, bias_tab):
    B, S = q.shape[:2]
    ckv = rms_norm(kv_lat, kv_norm)
    q_lat = jnp.einsum('bshd,rhd->bshr', q, w_uk)
    iq = iq.reshape(B, S, IDX_HEADS, IDX_DIM)
    iw = iw * (IDX_HEADS ** -0.5 * IDX_DIM ** -0.5)
    topk = min(DSA_TOPK, S // 4)
    scale = HEAD_DIM ** -0.5
    bi = jnp.arange(B)[:, None, None]
    spos = jnp.arange(S)

    def block(qb):
        q0 = qb * QBLK
        t = q0 + jnp.arange(QBLK)
        iqb = lax.dynamic_slice_in_dim(iq, q0, QBLK, axis=1)
        iwb = lax.dynamic_slice_in_dim(iw, q0, QBLK, axis=1)
        qlb = lax.dynamic_slice_in_dim(q_lat, q0, QBLK, axis=1)
        rel = jax.nn.relu(jnp.einsum('bqjd,bsd->bqjs', iqb, ik))
        isc = jnp.einsum('bqj,bqjs->bqs', iwb, rel).astype(jnp.float32)
        isc = jnp.where(spos[None, None, :] <= t[None, :, None], isc, NEG_INF)
        _, idx = lax.top_k(isc, topk)
        c_sel = ckv[bi, idx]
        dist = t[None, :, None] - idx
        bias = bias_tab[t5_bucket(dist)].transpose(0, 3, 1, 2)
        logits = jnp.einsum('bqhr,bqkr->bhqk', qlb, c_sel) * scale + bias
        p = masked_softmax(logits, (dist >= 0)[:, None])
        o_lat = jnp.einsum('bhqk,bqkr->bqhr', p.astype(c_sel.dtype), c_sel)
        o = jnp.einsum('bqhr,rhd->bqhd', o_lat, w_uv)
        return o.reshape(B, QBLK, DSA_HEADS * HEAD_DIM)

    return sweep_query_blocks(block, B, S)


def even_mixer(h, w_in, w_out, pe_k, pe_v, w1_k, w2_k, w1_v, w2_v, kv_norm, w_uk, w_uv, rel_bias):
    B, S, _ = h.shape
    nsa_q, kvc, kvs, kvw, gate, dsa_q, dsa_kv, idx_q, idx_k, idx_w = _split(h @ w_in, EVEN_SIZES)
    o_a = nsa_attention(nsa_q, kvc, kvs, kvw, gate, pe_k, pe_v, w1_k, w2_k, w1_v, w2_v, rel_bias[:, :NSA_HEADS])
    o_b = dsa_attention(dsa_q.reshape(B, S, DSA_HEADS, HEAD_DIM), dsa_kv, idx_q, idx_k, idx_w, kv_norm, w_uk, w_uv, rel_bias[:, NSA_HEADS:])
    return jnp.concatenate([o_a, o_b], axis=-1) @ w_out


def diff_mixer(h, w_in, w_out, lam, subln, rel_bias, lambda_init):
    B, S, _ = h.shape
    q, k, v = jnp.split(h @ w_in, 3, axis=-1)
    q = q.reshape(B, S, DIFF_HEADS, 2, DIFF_DIM)
    k = k.reshape(B, S, DIFF_HEADS, 2, DIFF_DIM)
    v = v.reshape(B, S, DIFF_HEADS, 2 * DIFF_DIM)
    lamf = lam.astype(jnp.float32)
    lam_full = jnp.exp(jnp.sum(lamf[0] * lamf[1])) - jnp.exp(jnp.sum(lamf[2] * lamf[3])) + lambda_init
    tab = rel_bias.reshape(NUM_BUCKETS, DIFF_HEADS, 2)
    scale = DIFF_DIM ** -0.5
    spos = jnp.arange(S)

    def block(qb):
        q0 = qb * QBLK
        t = q0 + jnp.arange(QBLK)
        qq = lax.dynamic_slice_in_dim(q, q0, QBLK, axis=1)
        dist = t[:, None] - spos[None, :]
        bias = tab[t5_bucket(dist)].transpose(2, 3, 0, 1)
        logits = jnp.einsum('bqhmd,bshmd->bhmqs', qq, k) * scale + bias
        p = masked_softmax(logits, dist >= 0)
        pd = p[:, :, 0] - lam_full * p[:, :, 1]
        o = jnp.einsum('bhqs,bshe->bqhe', pd.astype(v.dtype), v)
        o = rms_norm(o, subln) * (1.0 - lambda_init)
        return o.reshape(B, QBLK, DIFF_HEADS * 2 * DIFF_DIM)

    return sweep_query_blocks(block, B, S) @ w_out


def sq_relu_mlp(h, w1, w2):
    return jnp.square(jax.nn.relu(h @ w1)) @ w2


def modulate(x, shift, scale):
    return x * (1.0 + scale[:, None, :]) + shift[:, None, :]


def setup_inputs(seed: int = 0) -> dict:
    key = jax.random.key(seed)
    ks = jax.random.split(key, 26)
    f32 = jnp.float32

    def nrm(k, shape, fan_in, s=1.0):
        return jax.random.normal(k, shape, f32) * (s * fan_in ** -0.5)

    D = D_MODEL
    return {
        "x": jax.random.normal(ks[0], (BATCH, SEQ, D), f32),
        "c": jax.random.normal(ks[1], (BATCH, D), f32),
        "rel_bias": 0.5 * jax.random.normal(ks[2], (NUM_BUCKETS, REL_HEADS), f32),
        "ada_w": nrm(ks[3], (DEPTH, D, 6 * D), D, 0.2),
        "ada_b": 0.01 * jax.random.normal(ks[4], (DEPTH, 6 * D), f32),
        "ln_g": 1.0 + 0.02 * jax.random.normal(ks[5], (DEPTH, 2, D), f32),
        "ln_b": 0.02 * jax.random.normal(ks[6], (DEPTH, 2, D), f32),
        "ev_w_in": nrm(ks[7], (N_EVEN, D, EVEN_IN), D),
        "ev_w_out": nrm(ks[8], (N_EVEN, EVEN_OUT, D), EVEN_OUT, BETA),
        "nsa_pe_k": 0.5 * jax.random.normal(ks[9], (N_EVEN, CMP_LEN, HEAD_DIM), f32),
        "nsa_pe_v": 0.5 * jax.random.normal(ks[10], (N_EVEN, CMP_LEN, HEAD_DIM), f32),
        "nsa_w1_k": nrm(ks[11], (N_EVEN, CMP_LEN * HEAD_DIM, CMP_HIDDEN), CMP_LEN * HEAD_DIM),
        "nsa_w2_k": nrm(ks[12], (N_EVEN, CMP_HIDDEN, HEAD_DIM), CMP_HIDDEN),
        "nsa_w1_v": nrm(ks[13], (N_EVEN, CMP_LEN * HEAD_DIM, CMP_HIDDEN), CMP_LEN * HEAD_DIM),
        "nsa_w2_v": nrm(ks[14], (N_EVEN, CMP_HIDDEN, HEAD_DIM), CMP_HIDDEN),
        "dsa_kv_norm": 1.0 + 0.02 * jax.random.normal(ks[15], (N_EVEN, DSA_KV_RANK), f32),
        "dsa_w_uk": nrm(ks[16], (N_EVEN, DSA_KV_RANK, DSA_HEADS, HEAD_DIM), DSA_KV_RANK),
        "dsa_w_uv": nrm(ks[17], (N_EVEN, DSA_KV_RANK, DSA_HEADS, HEAD_DIM), DSA_KV_RANK),
        "od_w_in": nrm(ks[18], (N_ODD, D, ODD_IN), D),
        "od_w_out": nrm(ks[19], (N_ODD, ODD_OUT, D), ODD_OUT, BETA),
        "diff_lam": 0.1 * jax.random.normal(ks[20], (N_ODD, 4, DIFF_DIM), f32),
        "diff_subln": 1.0 + 0.02 * jax.random.normal(ks[21], (N_ODD, 2 * DIFF_DIM), f32),
        "mlp_w1": nrm(ks[22], (DEPTH, D, D_FF), D),
        "mlp_w2": nrm(ks[23], (DEPTH, D_FF, D), D_FF, BETA),
    }


def reference(x, c, rel_bias, ada_w, ada_b, ln_g, ln_b, ev_w_in, ev_w_out, nsa_pe_k, nsa_pe_v, nsa_w1_k, nsa_w2_k, nsa_w1_v, nsa_w2_v, dsa_kv_norm, dsa_w_uk, dsa_w_uv, od_w_in, od_w_out, diff_lam, diff_subln, mlp_w1, mlp_w2):
    c_act = jax.nn.silu(c)
    for l in range(DEPTH):
        ada = c_act @ ada_w[l] + ada_b[l]
        sh1, sc1, g1, sh2, sc2, g2 = jnp.split(ada, 6, axis=-1)
        h = modulate(x, sh1, sc1)
        i = l // 2
        if l % 2 == 0:
            y = even_mixer(h, ev_w_in[i], ev_w_out[i], nsa_pe_k[i], nsa_pe_v[i], nsa_w1_k[i], nsa_w2_k[i], nsa_w1_v[i], nsa_w2_v[i], dsa_kv_norm[i], dsa_w_uk[i], dsa_w_uv[i], rel_bias)
        else:
            lambda_init = 0.8 - 0.6 * math.exp(-0.3 * l)
            y = diff_mixer(h, od_w_in[i], od_w_out[i], diff_lam[i], diff_subln[i], rel_bias, lambda_init)
        x = layer_norm(ALPHA * x + (1.0 + g1[:, None, :]) * y, ln_g[l, 0], ln_b[l, 0])
        h = modulate(x, sh2, sc2)
        y = sq_relu_mlp(h, mlp_w1[l], mlp_w2[l])
        x = layer_norm(ALPHA * x + (1.0 + g2[:, None, :]) * y, ln_g[l, 1], ln_b[l, 1])
    return x
```

```python
import functools
import math

import jax
import jax.numpy as jnp
import numpy as np
from jax import lax
from jax.experimental import pallas as pl
from jax.experimental.pallas import tpu as pltpu

HEAD_DIM = 64
NSA_HEADS = 8
NSA_GROUPS = 2
NSA_HPG = NSA_HEADS // NSA_GROUPS
CMP_LEN = 32
CMP_STRIDE = 16
CMP_HIDDEN = 256
SEL_BLK = 64
N_SEL = 16
WINDOW = 512
DSA_HEADS = 8
DSA_KV_RANK = 128
IDX_HEADS = 8
IDX_DIM = 64
DSA_TOPK = 256
DIFF_HEADS = 8
DIFF_DIM = 64
NUM_BUCKETS = 32
MAX_DISTANCE = 128
NEG_INF = -1e30
BIG = 1e9
EPS = 1e-5

LANES = 128
TQ = 256
CMP_PAD = 16
CMP_WIN = 32
VMEM_LIMIT = 56 * 1024 * 1024

F32 = jnp.float32
BF16 = jnp.bfloat16


def _bucket_table():
    n = np.arange(MAX_DISTANCE)
    exact = NUM_BUCKETS // 2
    nf = np.maximum(n, 1).astype(np.float64)
    large = exact + (np.log(nf / exact) / math.log(MAX_DISTANCE / exact) * (NUM_BUCKETS - exact)).astype(np.int64)
    return np.where(n < exact, n, np.minimum(large, NUM_BUCKETS - 1)).astype(np.int32)


def _dot(a, b):
    return jnp.dot(a, b, preferred_element_type=F32)


def _dot_nt(a, b):
    return lax.dot_general(a, b, (((1,), (1,)), ((), ())), preferred_element_type=F32)


def _dot_hilo(a, b):
    hi = a.astype(BF16)
    lo = (a - hi.astype(F32)).astype(BF16)
    return _dot(hi, b) + _dot(lo, b)


def _params(*sem):
    return pltpu.CompilerParams(dimension_semantics=sem, vmem_limit_bytes=VMEM_LIMIT)


def _const_spec(shape):
    nd = len(shape)
    return pl.BlockSpec(shape, lambda *_: (0,) * nd)


def _ada_kernel(c_ref, w_ref, b_ref, o_ref):
    c = c_ref[...]
    ca = c * jax.nn.sigmoid(c)
    o_ref[0] = _dot(ca, w_ref[0]) + b_ref[0]


def _ada_call(c, ada_w, ada_b):
    depth, d, n = ada_w.shape
    bsz = c.shape[0]
    rows = 8
    cp = jnp.pad(c, ((0, rows - bsz), (0, 0)))
    tn = 1536
    out = pl.pallas_call(
        _ada_kernel,
        out_shape=jax.ShapeDtypeStruct((depth, rows, n), F32),
        grid=(depth, n // tn),
        in_specs=[
            pl.BlockSpec((rows, d), lambda l, j: (0, 0)),
            pl.BlockSpec((1, d, tn), lambda l, j: (l, 0, j)),
            pl.BlockSpec((1, 1, tn), lambda l, j: (l, 0, j)),
        ],
        out_specs=pl.BlockSpec((1, rows, tn), lambda l, j: (l, 0, j)),
        compiler_params=_params("arbitrary", "arbitrary"),
        name="ada_table",
    )(cp, ada_w, ada_b.reshape(depth, 1, n))
    return out[:, :bsz]


def _proj_kernel(segs, x_ref, sh_ref, sc_ref, w_ref, gain_ref, *o_refs):
    h = (x_ref[0] * (1.0 + sc_ref[0]) + sh_ref[0]).astype(BF16)
    off = 0
    for (width, _, epi), o_ref in zip(segs, o_refs):
        y = _dot(h, w_ref[:, off:off + width])
        if epi == "sigmoid":
            y = jax.nn.sigmoid(y)
        elif epi == "rms":
            y = y * lax.rsqrt(jnp.mean(jnp.square(y), axis=-1, keepdims=True) + EPS) * gain_ref[...]
        elif isinstance(epi, float):
            y = y * epi
        o_ref[0] = y.astype(o_ref.dtype)
        off += width


def _proj_call(x, sh, sc, w, gain, segs, tm=256):
    bsz, s, d = x.shape
    ntot = w.shape[1]
    return pl.pallas_call(
        functools.partial(_proj_kernel, segs),
        out_shape=[jax.ShapeDtypeStruct((bsz, s, width), dt) for width, dt, _ in segs],
        grid=(bsz, s // tm),
        in_specs=[
            pl.BlockSpec((1, tm, d), lambda b, i: (b, i, 0)),
            pl.BlockSpec((1, 1, d), lambda b, i: (b, 0, 0)),
            pl.BlockSpec((1, 1, d), lambda b, i: (b, 0, 0)),
            _const_spec((d, ntot)),
            _const_spec((1, LANES)),
        ],
        out_specs=[pl.BlockSpec((1, tm, width), lambda b, i: (b, i, 0)) for width, _, _ in segs],
        compiler_params=_params("parallel", "parallel"),
        name="modulate_in_proj",
    )(x, sh, sc, w, gain)


def _layer_norm(z, g, b):
    mu = jnp.mean(z, axis=-1, keepdims=True)
    zc = z - mu
    var = jnp.mean(jnp.square(zc), axis=-1, keepdims=True)
    return zc * lax.rsqrt(var + EPS) * g + b


def _out_kernel(alpha, x_ref, a_ref, w_ref, g1_ref, lg_ref, lb_ref, o_ref):
    y = _dot(a_ref[0], w_ref[...])
    z = alpha * x_ref[0] + (1.0 + g1_ref[0]) * y
    o_ref[0] = _layer_norm(z, lg_ref[...], lb_ref[...])


def _out_call(x, attn, w_out, g1, ln_g, ln_b, alpha, tm=512):
    bsz, s, d = x.shape
    k = attn.shape[-1]
    return pl.pallas_call(
        functools.partial(_out_kernel, alpha),
        out_shape=jax.ShapeDtypeStruct((bsz, s, d), F32),
        grid=(bsz, s // tm),
        in_specs=[
            pl.BlockSpec((1, tm, d), lambda b, i: (b, i, 0)),
            pl.BlockSpec((1, tm, k), lambda b, i: (b, i, 0)),
            _const_spec((k, d)),
            pl.BlockSpec((1, 1, d), lambda b, i: (b, 0, 0)),
            _const_spec((1, d)),
            _const_spec((1, d)),
        ],
        out_specs=pl.BlockSpec((1, tm, d), lambda b, i: (b, i, 0)),
        compiler_params=_params("parallel", "parallel"),
        name="out_proj_ln",
    )(x, attn, w_out, g1, ln_g, ln_b)


def _mlp_kernel(alpha, x_ref, sh_ref, sc_ref, g2_ref, w1_ref, w2_ref, lg_ref, lb_ref, o_ref, h_scr, acc_scr):
    f = pl.program_id(2)

    @pl.when(f == 0)
    def _():
        h_scr[...] = (x_ref[0] * (1.0 + sc_ref[0]) + sh_ref[0]).astype(BF16)
        acc_scr[...] = jnp.zeros_like(acc_scr)

    a = jnp.maximum(_dot(h_scr[...], w1_ref[...]), 0.0)
    acc_scr[...] += _dot((a * a).astype(BF16), w2_ref[...])

    @pl.when(f == pl.num_programs(2) - 1)
    def _():
        z = alpha * x_ref[0] + (1.0 + g2_ref[0]) * acc_scr[...]
        o_ref[0] = _layer_norm(z, lg_ref[...], lb_ref[...])


def _mlp_call(x, sh, sc, g2, w1, w2, ln_g, ln_b, alpha, tm=1024, tf=512):
    bsz, s, d = x.shape
    ff = w1.shape[1]
    tm = min(tm, s)
    return pl.pallas_call(
        functools.partial(_mlp_kernel, alpha),
        out_shape=jax.ShapeDtypeStruct((bsz, s, d), F32),
        grid=(bsz, s // tm, ff // tf),
        in_specs=[
            pl.BlockSpec((1, tm, d), lambda b, i, f: (b, i, 0)),
            pl.BlockSpec((1, 1, d), lambda b, i, f: (b, 0, 0)),
            pl.BlockSpec((1, 1, d), lambda b, i, f: (b, 0, 0)),
            pl.BlockSpec((1, 1, d), lambda b, i, f: (b, 0, 0)),
            pl.BlockSpec((d, tf), lambda b, i, f: (0, f)),
            pl.BlockSpec((tf, d), lambda b, i, f: (f, 0)),
            _const_spec((1, d)),
            _const_spec((1, d)),
        ],
        out_specs=pl.BlockSpec((1, tm, d), lambda b, i, f: (b, i, 0)),
        scratch_shapes=[pltpu.VMEM((tm, d), BF16), pltpu.VMEM((tm, d), F32)],
        compiler_params=_params("parallel", "parallel", "arbitrary"),
        name="mlp_ln",
    )(x, sh, sc, g2, w1, w2, ln_g, ln_b)


def _bias_kernel(tab_ref, t_ref, tc_ref):
    h = pl.program_id(0)
    ii = lax.broadcasted_iota(jnp.int32, (TQ, TQ), 0)
    jj = lax.broadcasted_iota(jnp.int32, (TQ, TQ), 1)
    cc = lax.broadcasted_iota(jnp.int32, (TQ, LANES), 1)
    ic = lax.broadcasted_iota(jnp.int32, (TQ, LANES), 0)
    dists = (ii - jj, TQ + ii - jj, ic - CMP_STRIDE * (cc - CMP_PAD) - (CMP_LEN - 1))

    def body(d, accs):
        val = tab_ref[h * MAX_DISTANCE + d]
        return tuple(jnp.where(dist == d, val, acc) for dist, acc in zip(dists, accs))

    accs = lax.fori_loop(0, MAX_DISTANCE, body, tuple(jnp.zeros(dist.shape, F32) for dist in dists))
    t_ref[0, 0] = jnp.where(dists[0] < 0, NEG_INF, accs[0])
    t_ref[0, 1] = accs[1]
    tc_ref[0] = jnp.where((dists[2] < 0) | (cc >= CMP_WIN), NEG_INF, accs[2])


def _bias_call(rel_bias):
    nh = rel_bias.shape[1]
    shifted = rel_bias[_bucket_table()] - rel_bias[NUM_BUCKETS - 1]
    tab = shifted.T.reshape(-1)
    return pl.pallas_call(
        _bias_kernel,
        out_shape=[
            jax.ShapeDtypeStruct((nh, 2, TQ, TQ), F32),
            jax.ShapeDtypeStruct((nh, TQ, LANES), F32),
        ],
        grid=(nh,),
        in_specs=[pl.BlockSpec(memory_space=pltpu.SMEM)],
        out_specs=[
            pl.BlockSpec((1, 2, TQ, TQ), lambda h: (h, 0, 0, 0)),
            pl.BlockSpec((1, TQ, LANES), lambda h: (h, 0, 0)),
        ],
        compiler_params=_params("arbitrary"),
        name="t5_bias_tiles",
    )(tab)


def _flash_init(m_ref, l_ref, acc_ref):
    m_ref[...] = jnp.full(m_ref.shape, NEG_INF, F32)
    l_ref[...] = jnp.zeros(l_ref.shape, F32)
    acc_ref[...] = jnp.zeros(acc_ref.shape, F32)


def _flash_update(s, v_tile, m_ref, l_ref, acc_ref):
    m_prev = m_ref[...]
    m_new = jnp.maximum(m_prev, jnp.max(s, axis=1, keepdims=True))
    alpha = jnp.exp(m_prev - m_new)
    p = jnp.exp(s - m_new)
    l_ref[...] = alpha * l_ref[...] + jnp.sum(p, axis=1, keepdims=True)
    acc_ref[...] = alpha * acc_ref[...] + _dot(p.astype(BF16), v_tile)
    m_ref[...] = m_new


def _half_mask(half):
    lane = lax.broadcasted_iota(jnp.int32, (1, LANES), 1)
    return (lane >= half * HEAD_DIM) & (lane < (half + 1) * HEAD_DIM)


def _diff_kernel(out_scale, q_ref, k_ref, v_ref, t_ref, lam_ref, sub_ref, o_ref, lhs_scr, m_scr, l_scr, acc_scr):
    qt = pl.program_id(2)
    scale = DIFF_DIM ** -0.5
    q = q_ref[0].astype(F32) * scale
    for m in range(2):
        lhs_scr[m * TQ:(m + 1) * TQ, :] = jnp.where(_half_mask(m), q, 0.0).astype(BF16)
    _flash_init(m_scr, l_scr, acc_scr)

    def step(kt, bias):
        k0 = pl.multiple_of(kt * TQ, TQ)
        s = _dot_nt(lhs_scr[...], k_ref[0, pl.ds(k0, TQ), :])
        if bias is not None:
            s = s + bias
        _flash_update(s, v_ref[0, pl.ds(k0, TQ), :], m_scr, l_scr, acc_scr)

    def far(kt, carry):
        step(kt, None)
        return carry

    lax.fori_loop(0, jnp.maximum(qt - 1, 0), far, 0)

    @pl.when(qt >= 1)
    def _():
        step(qt - 1, t_ref[:, 1].reshape(2 * TQ, TQ))

    step(qt, t_ref[:, 0].reshape(2 * TQ, TQ))

    o = acc_scr[...] / l_scr[...]
    lam = lam_ref[...]
    lam_full = (jnp.exp(jnp.sum(lam[0:1] * lam[1:2], axis=1, keepdims=True))
                - jnp.exp(jnp.sum(lam[2:3] * lam[3:4], axis=1, keepdims=True)) + (1.0 - out_scale))
    od = o[:TQ] - lam_full * o[TQ:]
    od = od * lax.rsqrt(jnp.mean(jnp.square(od), axis=-1, keepdims=True) + EPS) * sub_ref[...]
    o_ref[0] = (od * out_scale).astype(o_ref.dtype)


def _diff_call(qkv, t_tiles, lam, subln, lambda_init):
    bsz, s, _ = qkv.shape
    nh = DIFF_HEADS
    return pl.pallas_call(
        functools.partial(_diff_kernel, 1.0 - lambda_init),
        out_shape=jax.ShapeDtypeStruct((bsz, s, nh * LANES), BF16),
        grid=(bsz, nh, s // TQ),
        in_specs=[
            pl.BlockSpec((1, TQ, LANES), lambda b, h, i: (b, i, h)),
            pl.BlockSpec((1, s, LANES), lambda b, h, i: (b, 0, nh + h)),
            pl.BlockSpec((1, s, LANES), lambda b, h, i: (b, 0, 2 * nh + h)),
            pl.BlockSpec((2, 2, TQ, TQ), lambda b, h, i: (h, 0, 0, 0)),
            _const_spec((4, DIFF_DIM)),
            _const_spec((1, LANES)),
        ],
        out_specs=pl.BlockSpec((1, TQ, LANES), lambda b, h, i: (b, i, h)),
        scratch_shapes=[
            pltpu.VMEM((2 * TQ, LANES), BF16),
            pltpu.VMEM((2 * TQ, 1), F32),
            pltpu.VMEM((2 * TQ, 1), F32),
            pltpu.VMEM((2 * TQ, LANES), F32),
        ],
        compiler_params=_params("parallel", "parallel", "arbitrary"),
        name="diff_attention",
    )(qkv, qkv, qkv, t_tiles, lam, subln)


def _compress_kernel(nc, xk_ref, xv_ref, pe_ref, w1_ref, w2_ref, kc_ref, vc_ref):
    half = CMP_LEN // 2
    for kv, (x_ref, o_ref) in enumerate(((xk_ref, kc_ref), (xv_ref, vc_ref))):
        first = jnp.zeros((nc, NSA_GROUPS * CMP_HIDDEN), F32)
        second = jnp.zeros_like(first)
        for l in range(half):
            x = x_ref[0, pl.ds(l, nc, stride=CMP_STRIDE), :]
            first += _dot((x + pe_ref[kv, l:l + 1]).astype(BF16), w1_ref[kv, l])
            second += _dot((x + pe_ref[kv, half + l:half + l + 1]).astype(BF16), w1_ref[kv, half + l])
        pre = first + pltpu.roll(second, nc - 1, 0)
        hid = jax.nn.gelu(pre, approximate=True)
        o_ref[0] = jnp.zeros(o_ref.shape[1:], o_ref.dtype)
        o_ref[0, CMP_PAD:CMP_PAD + nc, :] = _dot(hid.astype(BF16), w2_ref[kv]).astype(o_ref.dtype)


def _cmp_rows(s):
    nc = s // CMP_STRIDE
    return -(-(nc + LANES) // LANES) * LANES


def _compress_call(kvc, pe2, w1bd, w2bd):
    bsz, s, _ = kvc.shape
    nc = s // CMP_STRIDE
    rows = _cmp_rows(s)
    return pl.pallas_call(
        functools.partial(_compress_kernel, nc),
        out_shape=[jax.ShapeDtypeStruct((bsz, rows, LANES), BF16)] * 2,
        grid=(bsz,),
        in_specs=[
            pl.BlockSpec((1, s, LANES), lambda b: (b, 0, 0)),
            pl.BlockSpec((1, s, LANES), lambda b: (b, 0, 1)),
            _const_spec(pe2.shape),
            _const_spec(w1bd.shape),
            _const_spec(w2bd.shape),
        ],
        out_specs=[pl.BlockSpec((1, rows, LANES), lambda b: (b, 0, 0))] * 2,
        compiler_params=_params("parallel"),
        name="nsa_compress",
    )(kvc, kvc, pe2, w1bd, w2bd)


def _nsa_kernel(q_ref, gate_ref, kc_ref, vc_ref, ks_ref, vs_ref, kw_ref, vw_ref, imp_ref, eneg_ref, t_ref, tc_ref,
                o_ref, qpad_scr, lhs_scr, m_scr, l_scr, acc_scr, res_scr):
    qt = pl.program_id(1)
    q0 = qt * TQ
    n0 = pl.multiple_of(qt * (TQ // CMP_STRIDE), TQ // CMP_STRIDE)
    rows = kc_ref.shape[1]
    hq = NSA_HPG * TQ
    scale = HEAD_DIM ** -0.5

    for g in range(NSA_GROUPS):
        for c in range(NSA_HPG):
            qcol = q_ref[0, :, c * LANES:(c + 1) * LANES].astype(F32) * scale
            qpad_scr[g, c * TQ:(c + 1) * TQ, :] = jnp.where(_half_mask(g), qcol, 0.0).astype(BF16)

    t_col = q0 + lax.broadcasted_iota(jnp.int32, (TQ, 1), 0)
    blk = lax.broadcasted_iota(jnp.int32, (TQ, LANES), 1)
    blkf = blk.astype(F32)
    cur = t_col // SEL_BLK
    allowed = blk <= cur
    forced = (blk == 0) | (blk == cur) | (blk == cur - 1)
    colp = lax.broadcasted_iota(jnp.int32, (1, rows), 1)
    far_ok = (colp >= CMP_PAD) & (colp < n0)
    near_ok = (n0 + lax.broadcasted_iota(jnp.int32, (1, LANES), 1)) >= CMP_PAD
    ii = lax.broadcasted_iota(jnp.int32, (TQ, TQ), 0)
    jj = lax.broadcasted_iota(jnp.int32, (TQ, TQ), 1)
    win_edge = jnp.where(jj > ii, 0.0, NEG_INF)

    for g in range(NSA_GROUPS):
        qp = qpad_scr[g]
        kc_near = kc_ref[0, pl.ds(n0, LANES), :]
        vc_near = vc_ref[0, pl.ds(n0, LANES), :]
        s_far = _dot_nt(qp, kc_ref[0])
        s_near = _dot_nt(qp, kc_near)
        psum_far = jnp.zeros((TQ, rows), F32)
        psum_near = jnp.zeros((TQ, LANES), F32)
        for c in range(NSA_HPG):
            h = g * NSA_HPG + c
            sf = jnp.where(far_ok, s_far[c * TQ:(c + 1) * TQ], NEG_INF)
            sn = jnp.where(near_ok, s_near[c * TQ:(c + 1) * TQ] + tc_ref[h], NEG_INF)
            mx = jnp.maximum(jnp.max(sf, axis=1, keepdims=True), jnp.max(sn, axis=1, keepdims=True))
            pf = jnp.where(sf > 0.5 * NEG_INF, jnp.exp(sf - mx), 0.0)
            pn = jnp.where(sn > 0.5 * NEG_INF, jnp.exp(sn - mx), 0.0)
            den = jnp.sum(pf, axis=1, keepdims=True) + jnp.sum(pn, axis=1, keepdims=True)
            inv = 1.0 / jnp.maximum(den, 1e-30)
            pf = pf * inv
            pn = pn * inv
            psum_far += pf
            psum_near += pn
            res_scr[0, g, c] = _dot(pf.astype(BF16), vc_ref[0]) + _dot(pn.astype(BF16), vc_near)
        score = _dot_hilo(psum_far, imp_ref[...]) + _dot_hilo(psum_near, imp_ref[pl.ds(n0, LANES), :])
        score = jnp.where(forced, BIG, jnp.where(allowed, score, -BIG))
        notsel = jnp.where(allowed, 0.0, 1.0)
        unpicked = jnp.ones((TQ, LANES), F32)
        for _ in range(N_SEL):
            best = jnp.max(score, axis=1, keepdims=True)
            first = jnp.min(jnp.where(score == best, blkf, float(LANES)), axis=1, keepdims=True)
            pick = blkf == first
            unpicked = jnp.where(pick, 0.0, unpicked)
            score = jnp.where(pick, -jnp.inf, score)
        notsel = jnp.maximum(notsel, unpicked).astype(BF16)
        for c in range(NSA_HPG):
            lhs_scr[c * TQ:(c + 1) * TQ, :LANES] = notsel
        lhs_scr[:, LANES:] = qp

        _flash_init(m_scr, l_scr, acc_scr)

        def sel_step(kt, bias_idx):
            k0 = pl.multiple_of(kt * TQ, TQ)
            rhs = jnp.concatenate([eneg_ref[pl.ds(k0, TQ), :], ks_ref[0, pl.ds(k0, TQ), :]], axis=1)
            s = _dot_nt(lhs_scr[...], rhs)
            if bias_idx is not None:
                s = s + t_ref[g * NSA_HPG:(g + 1) * NSA_HPG, bias_idx].reshape(hq, TQ)
            _flash_update(s, vs_ref[0, pl.ds(k0, TQ), :], m_scr, l_scr, acc_scr)

        def sel_far(kt, carry):
            sel_step(kt, None)
            return carry

        lax.fori_loop(0, jnp.maximum(qt - 1, 0), sel_far, 0)

        @pl.when(qt >= 1)
        def _():
            sel_step(qt - 1, 1)

        sel_step(qt, 0)
        res_scr[1, g] = (acc_scr[...] / l_scr[...]).reshape(NSA_HPG, TQ, LANES)

        _flash_init(m_scr, l_scr, acc_scr)

        def win_step(kt, bias):
            k0 = pl.multiple_of(kt * TQ, TQ)
            s = _dot_nt(qp, kw_ref[0, pl.ds(k0, TQ), :])
            s = (s.reshape(NSA_HPG, TQ, TQ) + bias).reshape(hq, TQ)
            _flash_update(s, vw_ref[0, pl.ds(k0, TQ), :], m_scr, l_scr, acc_scr)

        @pl.when(qt >= 2)
        def _():
            win_step(qt - 2, win_edge[None])

        @pl.when(qt >= 1)
        def _():
            win_step(qt - 1, t_ref[g * NSA_HPG:(g + 1) * NSA_HPG, 1])

        win_step(qt, t_ref[g * NSA_HPG:(g + 1) * NSA_HPG, 0])
        res_scr[2, g] = (acc_scr[...] / l_scr[...]).reshape(NSA_HPG, TQ, LANES)

    low = _half_mask(0)
    for c in range(NSA_HPG):
        out = jnp.zeros((TQ, LANES), F32)
        for br in range(3):
            gate = gate_ref[0, :, br * NSA_HEADS * HEAD_DIM + c * LANES:br * NSA_HEADS * HEAD_DIM + (c + 1) * LANES]
            out += gate * jnp.where(low, res_scr[br, 0, c], res_scr[br, 1, c])
        o_ref[0, :, c * LANES:(c + 1) * LANES] = out.astype(o_ref.dtype)


def _nsa_call(q, gates, kc, vc, kvs, kvw, imp, eneg, t_tiles, tc_tiles):
    bsz, s, _ = q.shape
    rows = kc.shape[1]
    hq = NSA_HPG * TQ
    whole = lambda col: pl.BlockSpec((1, s, LANES), lambda b, i: (b, 0, col))
    return pl.pallas_call(
        _nsa_kernel,
        out_shape=jax.ShapeDtypeStruct((bsz, s, NSA_HEADS * HEAD_DIM), BF16),
        grid=(bsz, s // TQ),
        in_specs=[
            pl.BlockSpec((1, TQ, NSA_HEADS * HEAD_DIM), lambda b, i: (b, i, 0)),
            pl.BlockSpec((1, TQ, 3 * NSA_HEADS * HEAD_DIM), lambda b, i: (b, i, 0)),
            pl.BlockSpec((1, rows, LANES), lambda b, i: (b, 0, 0)),
            pl.BlockSpec((1, rows, LANES), lambda b, i: (b, 0, 0)),
            whole(0), whole(1), whole(0), whole(1),
            _const_spec(imp.shape),
            _const_spec(eneg.shape),
            pl.BlockSpec((NSA_HEADS, 2, TQ, TQ), lambda b, i: (0, 0, 0, 0)),
            pl.BlockSpec((NSA_HEADS, TQ, LANES), lambda b, i: (0, 0, 0)),
        ],
        out_specs=pl.BlockSpec((1, TQ, NSA_HEADS * HEAD_DIM), lambda b, i: (b, i, 0)),
        scratch_shapes=[
            pltpu.VMEM((NSA_GROUPS, hq, LANES), BF16),
            pltpu.VMEM((hq, 2 * LANES), BF16),
            pltpu.VMEM((hq, 1), F32),
            pltpu.VMEM((hq, 1), F32),
            pltpu.VMEM((hq, LANES), F32),
            pltpu.VMEM((3, NSA_GROUPS, NSA_HPG, TQ, LANES), F32),
        ],
        compiler_params=_params("parallel", "arbitrary"),
        name="nsa_attention",
    )(q, gates, kc, vc, kvs, kvs, kvw, kvw, imp, eneg, t_tiles, tc_tiles)


def _sort_key(x):
    b = lax.bitcast_convert_type(x, jnp.int32)
    return b ^ ((b >> 31) & jnp.int32(0x7FFFFFFF))


def _dsa_kernel(topk, dq_ref, ckv_ref, iq_ref, ik_ref, iw_ref, wuk_ref, wuv_ref, t_ref, o_ref,
                key_scr, lhs_scr, m_scr, l_scr, acc_scr, thr_scr):
    qt = pl.program_id(1)
    nh = DSA_HEADS
    hq = nh * TQ
    ii = lax.broadcasted_iota(jnp.int32, (TQ, TQ), 0)
    jj = lax.broadcasted_iota(jnp.int32, (TQ, TQ), 1)
    int_min = jnp.int32(-2 ** 31)

    for j in range(IDX_HEADS):
        col = iq_ref[0, :, (j // 2) * LANES:(j // 2 + 1) * LANES]
        lhs_scr[j * TQ:(j + 1) * TQ, :] = jnp.where(_half_mask(j % 2), col, jnp.zeros_like(col))
    iw = iw_ref[0]

    def idx_tile(kt, causal):
        k0 = pl.multiple_of(kt * TQ, TQ)
        s = _dot_nt(lhs_scr[...], ik_ref[0, pl.ds(k0, TQ), :])
        isc = jnp.zeros((TQ, TQ), F32)
        for j in range(IDX_HEADS):
            isc += iw[:, j:j + 1] * jnp.maximum(s[j * TQ:(j + 1) * TQ], 0.0)
        if causal:
            isc = jnp.where(jj <= ii, isc, NEG_INF)
        key_scr[:, pl.ds(k0, TQ)] = _sort_key(isc)

    def idx_far(kt, carry):
        idx_tile(kt, False)
        return carry

    lax.fori_loop(0, qt, idx_far, 0)
    idx_tile(qt, True)

    def count(pred_fn):
        def body(kt, cnt):
            tile = key_scr[:, pl.ds(pl.multiple_of(kt * TQ, TQ), TQ)]
            hit = jnp.where(pred_fn(tile, kt), 1.0, 0.0)
            return cnt + hit[:, :LANES] + hit[:, LANES:]

        cnt = lax.fori_loop(0, qt + 1, body, jnp.zeros((TQ, LANES), F32))
        return jnp.sum(cnt, axis=1, keepdims=True)

    kf = float(topk)
    thr0 = jnp.where(count(lambda tile, kt: tile >= 0) >= kf, jnp.int32(0), int_min)

    def bit_body(i, thr):
        cand = thr | (jnp.int32(1) << (30 - i))
        return jnp.where(count(lambda tile, kt: tile >= cand) >= kf, cand, thr)

    thr = lax.fori_loop(0, 31, bit_body, thr0)
    thr_scr[...] = thr

    n_gt = count(lambda tile, kt: tile > thr)
    n_ge = count(lambda tile, kt: tile >= thr)
    need = kf - n_gt
    excess = jnp.max(jnp.where(n_ge - n_gt > need, 1.0, 0.0))

    @pl.when(excess > 0.0)
    def _():
        def pos_body(i, bound):
            cand = bound | (jnp.int32(1) << (30 - i))
            below = count(lambda tile, kt: (tile == thr) & (kt * TQ + jj < cand))
            return jnp.where(below <= need - 1.0, cand, bound)

        bound = lax.fori_loop(0, 31, pos_body, jnp.zeros((TQ, 1), jnp.int32))

        def demote(kt, carry):
            k0 = pl.multiple_of(kt * TQ, TQ)
            tile = key_scr[:, pl.ds(k0, TQ)]
            key_scr[:, pl.ds(k0, TQ)] = jnp.where((tile == thr) & (k0 + jj > bound), tile - 1, tile)
            return carry

        lax.fori_loop(0, qt + 1, demote, 0)

    scale = HEAD_DIM ** -0.5
    for h in range(nh):
        col = dq_ref[0, :, (h // 2) * LANES:(h // 2 + 1) * LANES]
        qm = jnp.where(_half_mask(h % 2), col, jnp.zeros_like(col))
        lhs_scr[h * TQ:(h + 1) * TQ, :] = (_dot(qm, wuk_ref[h // 2]) * scale).astype(BF16)

    _flash_init(m_scr, l_scr, acc_scr)

    def att_step(kt, bias_idx):
        k0 = pl.multiple_of(kt * TQ, TQ)
        c_tile = ckv_ref[0, pl.ds(k0, TQ), :]
        s = _dot_nt(lhs_scr[...], c_tile)
        bias = jnp.where(key_scr[:, pl.ds(k0, TQ)] >= thr_scr[...], 0.0, NEG_INF)[None]
        if bias_idx is not None:
            bias = bias + t_ref[:, bias_idx]
        s = (s.reshape(nh, TQ, TQ) + bias).reshape(hq, TQ)
        _flash_update(s, c_tile, m_scr, l_scr, acc_scr)

    def att_far(kt, carry):
        att_step(kt, None)
        return carry

    lax.fori_loop(0, jnp.maximum(qt - 1, 0), att_far, 0)

    @pl.when(qt >= 1)
    def _():
        att_step(qt - 1, 1)

    att_step(qt, 0)

    o_lat = (acc_scr[...] / l_scr[...]).astype(BF16)
    for c in range(nh // 2):
        out = _dot(o_lat[(2 * c) * TQ:(2 * c + 1) * TQ], wuv_ref[2 * c]) + _dot(
            o_lat[(2 * c + 1) * TQ:(2 * c + 2) * TQ], wuv_ref[2 * c + 1])
        o_ref[0, :, c * LANES:(c + 1) * LANES] = out.astype(o_ref.dtype)


def _dsa_call(dq, ckv, iq, ik2, iw, wuk, wuv, t_tiles, topk):
    bsz, s, _ = dq.shape
    nh = DSA_HEADS
    hq = nh * TQ
    return pl.pallas_call(
        functools.partial(_dsa_kernel, topk),
        out_shape=jax.ShapeDtypeStruct((bsz, s, nh * HEAD_DIM), BF16),
        grid=(bsz, s // TQ),
        in_specs=[
            pl.BlockSpec((1, TQ, nh * HEAD_DIM), lambda b, i: (b, i, 0)),
            pl.BlockSpec((1, s, LANES), lambda b, i: (b, 0, 0)),
            pl.BlockSpec((1, TQ, IDX_HEADS * IDX_DIM), lambda b, i: (b, i, 0)),
            pl.BlockSpec((1, s, LANES), lambda b, i: (b, 0, 0)),
            pl.BlockSpec((1, TQ, LANES), lambda b, i: (b, i, 0)),
            _const_spec(wuk.shape),
            _const_spec(wuv.shape),
            pl.BlockSpec((nh, 2, TQ, TQ), lambda b, i: (1, 0, 0, 0)),
        ],
        out_specs=pl.BlockSpec((1, TQ, nh * HEAD_DIM), lambda b, i: (b, i, 0)),
        scratch_shapes=[
            pltpu.VMEM((TQ, s), jnp.int32),
            pltpu.VMEM((hq, LANES), BF16),
            pltpu.VMEM((hq, 1), F32),
            pltpu.VMEM((hq, 1), F32),
            pltpu.VMEM((hq, LANES), F32),
            pltpu.VMEM((TQ, 1), jnp.int32),
        ],
        compiler_params=_params("parallel", "arbitrary"),
        name="dsa_attention",
    )(dq, ckv, iq, ik2, iw, wuk, wuv, t_tiles)


_NSA_Q = NSA_HEADS * HEAD_DIM
_NSA_KV = 2 * NSA_GROUPS * HEAD_DIM
_EVEN_SIZES = (_NSA_Q, _NSA_KV, _NSA_KV, _NSA_KV, NSA_HEADS * 3, DSA_HEADS * HEAD_DIM, DSA_KV_RANK,
               IDX_HEADS * IDX_DIM, IDX_DIM, IDX_HEADS)
_EVEN_OFF = np.concatenate([[0], np.cumsum(_EVEN_SIZES)])
_ZERO_COL = int(_EVEN_OFF[-1])


def _nsa_head_perm():
    pos = np.zeros(_NSA_Q, np.int64)
    for c in range(NSA_HPG):
        for g in range(NSA_GROUPS):
            for d in range(HEAD_DIM):
                pos[c * LANES + g * HEAD_DIM + d] = (g * NSA_HPG + c) * HEAD_DIM + d
    return pos


def _even_columns():
    o = _EVEN_OFF
    perm = _nsa_head_perm()
    gate = np.concatenate([o[4] + (perm // HEAD_DIM) * 3 + br for br in range(3)])
    idx_w = np.concatenate([np.arange(o[9], o[10]), np.full(LANES - IDX_HEADS, _ZERO_COL)])
    cols = [
        o[0] + perm,
        np.arange(o[1], o[2]), np.arange(o[2], o[3]), np.arange(o[3], o[4]),
        gate,
        np.arange(o[5], o[6]), np.arange(o[6], o[7]), np.arange(o[7], o[8]),
        np.concatenate([np.arange(o[8], o[9])] * 2),
        idx_w,
    ]
    segs = (
        (_NSA_Q, BF16, None), (_NSA_KV, F32, None), (_NSA_KV, BF16, None), (_NSA_KV, BF16, None),
        (3 * _NSA_Q, F32, "sigmoid"),
        (DSA_HEADS * HEAD_DIM, BF16, None), (DSA_KV_RANK, BF16, "rms"), (IDX_HEADS * IDX_DIM, BF16, None),
        (LANES, BF16, None), (LANES, F32, float(IDX_HEADS ** -0.5 * IDX_DIM ** -0.5)),
    )
    return np.concatenate(cols).astype(np.int32), segs


def _block_diag2(w):
    z = jnp.zeros_like(w)
    return jnp.concatenate([jnp.concatenate([w, z], axis=-1), jnp.concatenate([z, w], axis=-1)], axis=-2)


def _selection_constants(s):
    nc = s // CMP_STRIDE
    nb = s // SEL_BLK
    rows = _cmp_rows(s)
    cs = np.arange(nc) * CMP_STRIDE
    ss = np.arange(nb) * SEL_BLK
    overlap = np.clip(np.minimum(cs[:, None] + CMP_LEN, ss[None, :] + SEL_BLK) - np.maximum(cs[:, None], ss[None, :]), 0, None)
    imp = np.zeros((rows, LANES), np.float32)
    imp[CMP_PAD:CMP_PAD + nc, :nb] = overlap / CMP_STRIDE
    imp[CMP_PAD + nc - 1] = 0.0
    eneg = np.zeros((s, LANES), np.float32)
    eneg[np.arange(s), np.arange(s) // SEL_BLK] = NEG_INF
    return jnp.asarray(imp, BF16), jnp.asarray(eneg, BF16)


def kernel(x, c, rel_bias, ada_w, ada_b, ln_g, ln_b, ev_w_in, ev_w_out, nsa_pe_k, nsa_pe_v, nsa_w1_k, nsa_w2_k, nsa_w1_v, nsa_w2_v, dsa_kv_norm, dsa_w_uk, dsa_w_uv, od_w_in, od_w_out, diff_lam, diff_subln, mlp_w1, mlp_w2):
    bsz, s, d = x.shape
    depth = ada_w.shape[0]
    assert s % TQ == 0 and s // SEL_BLK <= LANES
    alpha = (2 * depth) ** 0.25
    topk = min(DSA_TOPK, s // 4)

    ada = _ada_call(c, ada_w, ada_b)
    t_tiles, tc_tiles = _bias_call(rel_bias)
    imp, eneg = _selection_constants(s)
    even_cols, even_segs = _even_columns()
    perm = _nsa_head_perm()
    odd_segs = ((od_w_in.shape[-1], BF16, None),)
    unit_gain = jnp.ones((1, LANES), F32)

    for l in range(depth):
        sh1, sc1, g1, sh2, sc2, g2 = [a.reshape(bsz, 1, d) for a in jnp.split(ada[l], 6, axis=-1)]
        i = l // 2
        if l % 2 == 0:
            w_aug = jnp.concatenate([ev_w_in[i], jnp.zeros((d, 1), F32)], axis=1)
            w_in = jnp.take(w_aug, even_cols, axis=1).astype(BF16)
            nsa_q, kvc, kvs, kvw, gates, dsa_q, ckv, idx_q, ik2, idx_w = _proj_call(
                x, sh1, sc1, w_in, dsa_kv_norm[i].reshape(1, LANES), even_segs)
            pe2 = jnp.stack([jnp.tile(nsa_pe_k[i], (1, 2)), jnp.tile(nsa_pe_v[i], (1, 2))])
            w1bd = jnp.stack([_block_diag2(w.reshape(CMP_LEN, HEAD_DIM, CMP_HIDDEN)) for w in (nsa_w1_k[i], nsa_w1_v[i])]).astype(BF16)
            w2bd = jnp.stack([_block_diag2(nsa_w2_k[i]), _block_diag2(nsa_w2_v[i])]).astype(BF16)
            kc, vc = _compress_call(kvc, pe2, w1bd, w2bd)
            o_a = _nsa_call(nsa_q, gates, kc, vc, kvs, kvw, imp, eneg, t_tiles, tc_tiles)
            wuk = dsa_w_uk[i].transpose(1, 2, 0).reshape(DSA_HEADS // 2, LANES, DSA_KV_RANK).astype(BF16)
            wuv_h = dsa_w_uv[i].transpose(1, 0, 2)
            zeros = jnp.zeros_like(wuv_h)
            even_h = (np.arange(DSA_HEADS) % 2 == 0)[:, None, None]
            wuv = jnp.where(even_h, jnp.concatenate([wuv_h, zeros], -1), jnp.concatenate([zeros, wuv_h], -1)).astype(BF16)
            o_b = _dsa_call(dsa_q, ckv, idx_q, ik2, idx_w, wuk, wuv, t_tiles, topk)
            attn = jnp.concatenate([o_a, o_b], axis=-1)
            w_out = jnp.concatenate([ev_w_out[i][perm], ev_w_out[i][_NSA_Q:]], axis=0).astype(BF16)
        else:
            lambda_init = 0.8 - 0.6 * math.exp(-0.3 * l)
            (qkv,) = _proj_call(x, sh1, sc1, od_w_in[i].astype(BF16), unit_gain, odd_segs)
            attn = _diff_call(qkv, t_tiles, diff_lam[i], diff_subln[i].reshape(1, LANES), lambda_init)
            w_out = od_w_out[i].astype(BF16)
        x = _out_call(x, attn, w_out, g1, ln_g[l, 0].reshape(1, d), ln_b[l, 0].reshape(1, d), alpha)
        x = _mlp_call(x, sh2, sc2, g2, mlp_w1[l].astype(BF16), mlp_w2[l].astype(BF16),
                      ln_g[l, 1].reshape(1, d), ln_b[l, 1].reshape(1, d), alpha)
    return x
```

```python
import functools
import math

import jax
import jax.numpy as jnp
import numpy as np
from jax import lax
from jax.experimental import pallas as pl
from jax.experimental.pallas import tpu as pltpu

HEAD_DIM = 64
NSA_HEADS = 8
NSA_GROUPS = 2
NSA_HPG = NSA_HEADS // NSA_GROUPS
CMP_LEN = 32
CMP_STRIDE = 16
CMP_HIDDEN = 256
SEL_BLK = 64
N_SEL = 16
WINDOW = 512
DSA_HEADS = 8
DSA_KV_RANK = 128
IDX_HEADS = 8
IDX_DIM = 64
DSA_TOPK = 256
DIFF_HEADS = 8
DIFF_DIM = 64
NUM_BUCKETS = 32
MAX_DISTANCE = 128
NEG_INF = -1e30
BIG = 1e9
EPS = 1e-5

LANES = 128
TQ = 256
CMP_PAD = 16
CMP_WIN = 32
VMEM_LIMIT = 56 * 1024 * 1024
LOG2E = math.log2(math.e)

F32 = jnp.float32
BF16 = jnp.bfloat16


def _bucket_table():
    n = np.arange(MAX_DISTANCE)
    exact = NUM_BUCKETS // 2
    nf = np.maximum(n, 1).astype(np.float64)
    large = exact + (np.log(nf / exact) / math.log(MAX_DISTANCE / exact) * (NUM_BUCKETS - exact)).astype(np.int64)
    return np.where(n < exact, n, np.minimum(large, NUM_BUCKETS - 1)).astype(np.int32)


def _dot(a, b):
    return jnp.dot(a, b, preferred_element_type=F32)


def _dot_tn(a, b):
    return lax.dot_general(a, b, (((0,), (0,)), ((), ())), preferred_element_type=F32)


def _params(*sem):
    return pltpu.CompilerParams(dimension_semantics=sem, vmem_limit_bytes=VMEM_LIMIT)


def _const_spec(shape):
    nd = len(shape)
    return pl.BlockSpec(shape, lambda *_: (0,) * nd)


def _ada_kernel(c_ref, w_ref, b_ref, o_ref):
    c = c_ref[...]
    ca = c * jax.nn.sigmoid(c)
    o_ref[0] = _dot(ca, w_ref[0]) + b_ref[0]


def _ada_call(c, ada_w, ada_b):
    depth, d, n = ada_w.shape
    bsz = c.shape[0]
    rows = 8
    cp = jnp.pad(c, ((0, rows - bsz), (0, 0)))
    tn = 1536
    out = pl.pallas_call(
        _ada_kernel,
        out_shape=jax.ShapeDtypeStruct((depth, rows, n), F32),
        grid=(depth, n // tn),
        in_specs=[
            pl.BlockSpec((rows, d), lambda l, j: (0, 0)),
            pl.BlockSpec((1, d, tn), lambda l, j: (l, 0, j)),
            pl.BlockSpec((1, 1, tn), lambda l, j: (l, 0, j)),
        ],
        out_specs=pl.BlockSpec((1, rows, tn), lambda l, j: (l, 0, j)),
        compiler_params=_params("arbitrary", "arbitrary"),
        name="ada_table",
    )(cp, ada_w, ada_b.reshape(depth, 1, n))
    return out[:, :bsz]


def _proj_kernel(segs, x_ref, sh_ref, sc_ref, w_ref, gain_ref, *o_refs):
    h = (x_ref[0] * (1.0 + sc_ref[0]) + sh_ref[0]).astype(BF16)
    off = 0
    for (width, _, epi), o_ref in zip(segs, o_refs):
        y = _dot(h, w_ref[:, off:off + width])
        if epi == "sigmoid":
            y = jax.nn.sigmoid(y)
        elif epi == "rms":
            y = y * lax.rsqrt(jnp.mean(jnp.square(y), axis=-1, keepdims=True) + EPS) * gain_ref[...]
        elif isinstance(epi, float):
            y = y * epi
        o_ref[0] = y.astype(o_ref.dtype)
        off += width


def _proj_call(x, sh, sc, w, gain, segs, tm=256):
    bsz, s, d = x.shape
    ntot = w.shape[1]
    return pl.pallas_call(
        functools.partial(_proj_kernel, segs),
        out_shape=[jax.ShapeDtypeStruct((bsz, s, width), dt) for width, dt, _ in segs],
        grid=(bsz, s // tm),
        in_specs=[
            pl.BlockSpec((1, tm, d), lambda b, i: (b, i, 0)),
            pl.BlockSpec((1, 1, d), lambda b, i: (b, 0, 0)),
            pl.BlockSpec((1, 1, d), lambda b, i: (b, 0, 0)),
            _const_spec((d, ntot)),
            _const_spec((1, LANES)),
        ],
        out_specs=[pl.BlockSpec((1, tm, width), lambda b, i: (b, i, 0)) for width, _, _ in segs],
        compiler_params=_params("parallel", "parallel"),
        name="modulate_in_proj",
    )(x, sh, sc, w, gain)


def _layer_norm(z, g, b):
    mu = jnp.mean(z, axis=-1, keepdims=True)
    zc = z - mu
    var = jnp.mean(jnp.square(zc), axis=-1, keepdims=True)
    return zc * lax.rsqrt(var + EPS) * g + b


def _out_kernel(alpha, x_ref, a_ref, w_ref, g1_ref, lg_ref, lb_ref, o_ref):
    y = _dot(a_ref[0], w_ref[...])
    z = alpha * x_ref[0] + (1.0 + g1_ref[0]) * y
    o_ref[0] = _layer_norm(z, lg_ref[...], lb_ref[...])


def _out_call(x, attn, w_out, g1, ln_g, ln_b, alpha, tm=512):
    bsz, s, d = x.shape
    k = attn.shape[-1]
    return pl.pallas_call(
        functools.partial(_out_kernel, alpha),
        out_shape=jax.ShapeDtypeStruct((bsz, s, d), F32),
        grid=(bsz, s // tm),
        in_specs=[
            pl.BlockSpec((1, tm, d), lambda b, i: (b, i, 0)),
            pl.BlockSpec((1, tm, k), lambda b, i: (b, i, 0)),
            _const_spec((k, d)),
            pl.BlockSpec((1, 1, d), lambda b, i: (b, 0, 0)),
            _const_spec((1, d)),
            _const_spec((1, d)),
        ],
        out_specs=pl.BlockSpec((1, tm, d), lambda b, i: (b, i, 0)),
        compiler_params=_params("parallel", "parallel"),
        name="out_proj_ln",
    )(x, attn, w_out, g1, ln_g, ln_b)


def _mlp_kernel(alpha, x_ref, sh_ref, sc_ref, g2_ref, w1_ref, w2_ref, lg_ref, lb_ref, o_ref, h_scr, acc_scr):
    f = pl.program_id(2)

    @pl.when(f == 0)
    def _():
        h_scr[...] = (x_ref[0] * (1.0 + sc_ref[0]) + sh_ref[0]).astype(BF16)
        acc_scr[...] = jnp.zeros_like(acc_scr)

    a = jnp.maximum(_dot(h_scr[...], w1_ref[...]), 0.0)
    acc_scr[...] += _dot((a * a).astype(BF16), w2_ref[...])

    @pl.when(f == pl.num_programs(2) - 1)
    def _():
        z = alpha * x_ref[0] + (1.0 + g2_ref[0]) * acc_scr[...]
        o_ref[0] = _layer_norm(z, lg_ref[...], lb_ref[...])


def _mlp_call(x, sh, sc, g2, w1, w2, ln_g, ln_b, alpha, tm=1024, tf=512):
    bsz, s, d = x.shape
    ff = w1.shape[1]
    tm = min(tm, s)
    assert s % tm == 0
    return pl.pallas_call(
        functools.partial(_mlp_kernel, alpha),
        out_shape=jax.ShapeDtypeStruct((bsz, s, d), F32),
        grid=(bsz, s // tm, ff // tf),
        in_specs=[
            pl.BlockSpec((1, tm, d), lambda b, i, f: (b, i, 0)),
            pl.BlockSpec((1, 1, d), lambda b, i, f: (b, 0, 0)),
            pl.BlockSpec((1, 1, d), lambda b, i, f: (b, 0, 0)),
            pl.BlockSpec((1, 1, d), lambda b, i, f: (b, 0, 0)),
            pl.BlockSpec((d, tf), lambda b, i, f: (0, f)),
            pl.BlockSpec((tf, d), lambda b, i, f: (f, 0)),
            _const_spec((1, d)),
            _const_spec((1, d)),
        ],
        out_specs=pl.BlockSpec((1, tm, d), lambda b, i, f: (b, i, 0)),
        scratch_shapes=[pltpu.VMEM((tm, d), BF16), pltpu.VMEM((tm, d), F32)],
        compiler_params=_params("parallel", "parallel", "arbitrary"),
        name="mlp_ln",
    )(x, sh, sc, g2, w1, w2, ln_g, ln_b)


def _bias_kernel(tab_ref, t_ref, tc_ref):
    h = pl.program_id(0)
    key = lax.broadcasted_iota(jnp.int32, (TQ, TQ), 0)
    qry = lax.broadcasted_iota(jnp.int32, (TQ, TQ), 1)
    win = lax.broadcasted_iota(jnp.int32, (LANES, TQ), 0)
    qry_c = lax.broadcasted_iota(jnp.int32, (LANES, TQ), 1)
    dists = (qry - key, TQ + qry - key, qry_c - CMP_STRIDE * (win - CMP_PAD) - (CMP_LEN - 1))

    def body(d, accs):
        val = tab_ref[h * MAX_DISTANCE + d] * LOG2E
        return tuple(jnp.where(dist == d, val, acc) for dist, acc in zip(dists, accs))

    accs = lax.fori_loop(0, MAX_DISTANCE, body, tuple(jnp.zeros(dist.shape, F32) for dist in dists))
    t_ref[0] = jnp.where(dists[0] < 0, NEG_INF, accs[0])
    t_ref[1] = accs[1]
    tc_ref[...] = jnp.where((dists[2] < 0) | (win >= CMP_WIN), NEG_INF, accs[2])


def _bias_call(rel_bias):
    nh = rel_bias.shape[1]
    shifted = rel_bias[_bucket_table()] - rel_bias[NUM_BUCKETS - 1]
    tab = shifted.T.reshape(-1)
    return pl.pallas_call(
        _bias_kernel,
        out_shape=[
            jax.ShapeDtypeStruct((2, TQ, nh * TQ), F32),
            jax.ShapeDtypeStruct((LANES, nh * TQ), F32),
        ],
        grid=(nh,),
        in_specs=[pl.BlockSpec(memory_space=pltpu.SMEM)],
        out_specs=[
            pl.BlockSpec((2, TQ, TQ), lambda h: (0, 0, h)),
            pl.BlockSpec((LANES, TQ), lambda h: (0, h)),
        ],
        compiler_params=_params("arbitrary"),
        name="t5_bias_tiles",
    )(tab)


def _flash_init(m_ref, l_ref, acc_ref):
    m_ref[...] = jnp.full(m_ref.shape, NEG_INF, F32)
    l_ref[...] = jnp.zeros(l_ref.shape, F32)
    acc_ref[...] = jnp.zeros(acc_ref.shape, F32)


def _flash_update(s, v_tile, m_ref, l_ref, acc_ref):
    m_prev = m_ref[...]
    m_new = jnp.maximum(m_prev, jnp.max(s, axis=0, keepdims=True))
    alpha = jnp.exp2(m_prev - m_new)
    p = jnp.exp2(s - m_new)
    l_ref[...] = alpha * l_ref[...] + jnp.sum(p, axis=0, keepdims=True)
    acc_ref[...] = alpha * acc_ref[...] + _dot_tn(v_tile, p.astype(BF16))
    m_ref[...] = m_new


def _flash_causal(qt, qk_fn, extra_fn, v_fn, bufs, m_ref, l_ref, acc_ref):
    even_set, odd_set = bufs
    _flash_init(m_ref, l_ref, acc_ref)
    for _, p_buf, a_buf in bufs:
        p_buf[...] = jnp.zeros(p_buf.shape, p_buf.dtype)
        a_buf[...] = jnp.ones(a_buf.shape, F32)

    def logits(kt, dst):
        dst[0][...] = qk_fn(kt)

    def softmax(kt, cur, kind):
        width = cur[0].shape[1]
        blocks = [slice(b * TQ, (b + 1) * TQ) for b in range(width // TQ)]
        adds = [extra_fn(kt, kind, b) for b in range(len(blocks))]
        tops = []
        for blk, add in zip(blocks, adds):
            s = cur[0][:, blk]
            tops.append(jnp.max(s if add is None else s + add, axis=0, keepdims=True))
        m_prev = m_ref[...]
        m_new = jnp.maximum(m_prev, jnp.concatenate(tops, axis=1))
        alpha = jnp.exp2(m_prev - m_new)
        sums = []
        for blk, add in zip(blocks, adds):
            z = cur[0][:, blk] - m_new[:, blk]
            p = jnp.exp2(z if add is None else z + add)
            sums.append(jnp.sum(p, axis=0, keepdims=True))
            cur[1][:, blk] = p.astype(BF16)
        l_ref[...] = alpha * l_ref[...] + jnp.concatenate(sums, axis=1)
        m_ref[...] = m_new
        cur[2][...] = alpha

    def values(kt, src):
        acc_ref[...] = src[2][...] * acc_ref[...] + _dot_tn(v_fn(jnp.maximum(kt, 0)), src[1][...])

    def trip(kt, cur, nxt):
        logits(kt + 1, nxt)
        softmax(kt, cur, None)
        values(kt - 1, nxt)

    is_even = (qt & 1) == 0

    @pl.when(is_even)
    def _():
        logits(0, even_set)

    @pl.when(jnp.logical_not(is_even))
    def _():
        logits(0, odd_set)

    @pl.when(is_even & (qt >= 2))
    def _():
        trip(0, even_set, odd_set)

    start = jnp.where(is_even, 1, 0)

    def pair(j, carry):
        kt = start + 2 * j
        trip(kt, odd_set, even_set)
        trip(kt + 1, even_set, odd_set)
        return carry

    lax.fori_loop(0, (jnp.maximum(qt - 1, 0) - start) // 2, pair, 0)

    @pl.when(qt >= 1)
    def _():
        logits(qt, even_set)
        softmax(qt - 1, odd_set, 1)
        values(qt - 2, even_set)

    softmax(qt, even_set, 0)
    values(qt - 1, odd_set)
    values(qt, even_set)


def _flash_scratch(width):
    stats = [pltpu.VMEM((1, width), F32), pltpu.VMEM((1, width), F32), pltpu.VMEM((LANES, width), F32)]
    one_set = [pltpu.VMEM((TQ, width), F32), pltpu.VMEM((TQ, width), BF16), pltpu.VMEM((1, width), F32)]
    return stats + 2 * one_set


def _half_rows(half, width):
    row = lax.broadcasted_iota(jnp.int32, (LANES, width), 0)
    return (row >= half * HEAD_DIM) & (row < (half + 1) * HEAD_DIM)


def _key_tile(ref, kt):
    return ref[0, pl.ds(pl.multiple_of(kt * TQ, TQ), TQ), :]


def _diff_kernel(out_scale, q_ref, k_ref, v_ref, t_ref, lam_ref, sub_ref, o_ref,
                 qt_scr, m_scr, l_scr, acc_scr, s0_scr, p0_scr, a0_scr, s1_scr, p1_scr, a1_scr):
    qt = pl.program_id(2)
    q_t = (q_ref[0].astype(F32) * (DIFF_DIM ** -0.5 * LOG2E)).T
    for m in range(2):
        qt_scr[:, m * TQ:(m + 1) * TQ] = jnp.where(_half_rows(m, TQ), q_t, 0.0).astype(BF16)

    _flash_causal(
        qt,
        lambda kt: _dot(_key_tile(k_ref, kt), qt_scr[...]),
        lambda kt, kind, b: None if kind is None else t_ref[kind, :, b * TQ:(b + 1) * TQ],
        lambda kt: _key_tile(v_ref, kt),
        ((s0_scr, p0_scr, a0_scr), (s1_scr, p1_scr, a1_scr)), m_scr, l_scr, acc_scr)

    o = acc_scr[...] / l_scr[...]
    lam = lam_ref[...]
    lam_full = (jnp.exp(jnp.sum(lam[0:1] * lam[1:2], axis=1, keepdims=True))
                - jnp.exp(jnp.sum(lam[2:3] * lam[3:4], axis=1, keepdims=True)) + (1.0 - out_scale))
    od = (o[:, :TQ] - lam_full * o[:, TQ:]).T
    od = od * lax.rsqrt(jnp.mean(jnp.square(od), axis=-1, keepdims=True) + EPS) * sub_ref[...]
    o_ref[0] = (od * out_scale).astype(o_ref.dtype)


def _diff_call(qkv, t_tiles, lam, subln, lambda_init):
    bsz, s, _ = qkv.shape
    nh = DIFF_HEADS
    return pl.pallas_call(
        functools.partial(_diff_kernel, 1.0 - lambda_init),
        out_shape=jax.ShapeDtypeStruct((bsz, s, nh * LANES), BF16),
        grid=(bsz, nh, s // TQ),
        in_specs=[
            pl.BlockSpec((1, TQ, LANES), lambda b, h, i: (b, i, h)),
            pl.BlockSpec((1, s, LANES), lambda b, h, i: (b, 0, nh + h)),
            pl.BlockSpec((1, s, LANES), lambda b, h, i: (b, 0, 2 * nh + h)),
            pl.BlockSpec((2, TQ, 2 * TQ), lambda b, h, i: (0, 0, h)),
            _const_spec((4, DIFF_DIM)),
            _const_spec((1, LANES)),
        ],
        out_specs=pl.BlockSpec((1, TQ, LANES), lambda b, h, i: (b, i, h)),
        scratch_shapes=[pltpu.VMEM((LANES, 2 * TQ), BF16)] + _flash_scratch(2 * TQ),
        compiler_params=_params("parallel", "parallel", "arbitrary"),
        name="diff_attention",
    )(qkv, qkv, qkv, t_tiles, lam, subln)


def _compress_kernel(nc, xk_ref, xv_ref, pe_ref, w1_ref, w2_ref, kc_ref, vc_ref):
    half = CMP_LEN // 2
    for kv, (x_ref, o_ref) in enumerate(((xk_ref, kc_ref), (xv_ref, vc_ref))):
        first = jnp.zeros((nc, NSA_GROUPS * CMP_HIDDEN), F32)
        second = jnp.zeros_like(first)
        for l in range(half):
            x = x_ref[0, pl.ds(l, nc, stride=CMP_STRIDE), :]
            first += _dot((x + pe_ref[kv, l:l + 1]).astype(BF16), w1_ref[kv, l])
            second += _dot((x + pe_ref[kv, half + l:half + l + 1]).astype(BF16), w1_ref[kv, half + l])
        pre = first + pltpu.roll(second, nc - 1, 0)
        hid = jax.nn.gelu(pre, approximate=True)
        o_ref[0] = jnp.zeros(o_ref.shape[1:], o_ref.dtype)
        o_ref[0, CMP_PAD:CMP_PAD + nc, :] = _dot(hid.astype(BF16), w2_ref[kv]).astype(o_ref.dtype)


def _cmp_rows(s):
    nc = s // CMP_STRIDE
    return -(-(nc + LANES) // LANES) * LANES


def _compress_call(kvc, pe2, w1bd, w2bd):
    bsz, s, _ = kvc.shape
    nc = s // CMP_STRIDE
    rows = _cmp_rows(s)
    return pl.pallas_call(
        functools.partial(_compress_kernel, nc),
        out_shape=[jax.ShapeDtypeStruct((bsz, rows, LANES), BF16)] * 2,
        grid=(bsz,),
        in_specs=[
            pl.BlockSpec((1, s, LANES), lambda b: (b, 0, 0)),
            pl.BlockSpec((1, s, LANES), lambda b: (b, 0, 1)),
            _const_spec(pe2.shape),
            _const_spec(w1bd.shape),
            _const_spec(w2bd.shape),
        ],
        out_specs=[pl.BlockSpec((1, rows, LANES), lambda b: (b, 0, 0))] * 2,
        compiler_params=_params("parallel"),
        name="nsa_compress",
    )(kvc, kvc, pe2, w1bd, w2bd)


def _nsa_kernel(q_ref, gate_ref, kc_ref, vc_ref, ks_ref, vs_ref, kw_ref, vw_ref, imp_ref, eneg_ref, t_ref, tc_ref,
                o_ref, qpad_scr, lhs_scr, res_scr, m_scr, l_scr, acc_scr, s0_scr, p0_scr, a0_scr, s1_scr, p1_scr, a1_scr):
    qt = pl.program_id(1)
    q0 = qt * TQ
    n0 = pl.multiple_of(qt * (TQ // CMP_STRIDE), TQ // CMP_STRIDE)
    rows = kc_ref.shape[1]
    hq = NSA_HPG * TQ

    for c in range(NSA_HPG):
        q_t = (q_ref[0, :, c * LANES:(c + 1) * LANES].astype(F32) * (HEAD_DIM ** -0.5 * LOG2E)).T
        for g in range(NSA_GROUPS):
            qpad_scr[g, :, c * TQ:(c + 1) * TQ] = jnp.where(_half_rows(g, TQ), q_t, 0.0).astype(BF16)

    blk = lax.broadcasted_iota(jnp.int32, (LANES, TQ), 0)
    blkf = blk.astype(F32)
    cur = (q0 + lax.broadcasted_iota(jnp.int32, (LANES, TQ), 1)) // SEL_BLK
    allowed = blk <= cur
    forced = (blk == 0) | (blk == cur) | (blk == cur - 1)
    rowp = lax.broadcasted_iota(jnp.int32, (rows, TQ), 0)
    far_ok = (rowp >= CMP_PAD) & (rowp < n0)
    near_ok = (n0 + lax.broadcasted_iota(jnp.int32, (LANES, TQ), 0)) >= CMP_PAD
    key_i = lax.broadcasted_iota(jnp.int32, (TQ, hq), 0)
    qry_i = lax.broadcasted_iota(jnp.int32, (TQ, hq), 1) & (TQ - 1)
    win_edge = jnp.where(key_i > qry_i, 0.0, NEG_INF)

    for g in range(NSA_GROUPS):
        qp = qpad_scr[g]
        heads = slice(g * hq, (g + 1) * hq)
        kc_near = kc_ref[0, pl.ds(n0, LANES), :]
        vc_near = vc_ref[0, pl.ds(n0, LANES), :]
        s_far = _dot(kc_ref[0], qp)
        s_near = _dot(kc_near, qp) + tc_ref[:, heads]
        psum_far = jnp.zeros((rows, TQ), F32)
        psum_near = jnp.zeros((LANES, TQ), F32)
        p_far, p_near = [], []
        for c in range(NSA_HPG):
            sf = jnp.where(far_ok, s_far[:, c * TQ:(c + 1) * TQ], NEG_INF)
            sn = jnp.where(near_ok, s_near[:, c * TQ:(c + 1) * TQ], NEG_INF)
            mx = jnp.maximum(jnp.max(sf, axis=0, keepdims=True), jnp.max(sn, axis=0, keepdims=True))
            pf = jnp.where(sf > 0.5 * NEG_INF, jnp.exp2(sf - mx), 0.0)
            pn = jnp.where(sn > 0.5 * NEG_INF, jnp.exp2(sn - mx), 0.0)
            den = jnp.sum(pf, axis=0, keepdims=True) + jnp.sum(pn, axis=0, keepdims=True)
            inv = 1.0 / jnp.maximum(den, 1e-30)
            pf = pf * inv
            pn = pn * inv
            psum_far += pf
            psum_near += pn
            p_far.append(pf.astype(BF16))
            p_near.append(pn.astype(BF16))
        res_scr[0, g] = (_dot_tn(vc_ref[0], jnp.concatenate(p_far, axis=1))
                         + _dot_tn(vc_near, jnp.concatenate(p_near, axis=1)))
        score = jnp.zeros((LANES, TQ), F32)
        for psum, imp in ((psum_far, imp_ref[...]), (psum_near, imp_ref[pl.ds(n0, LANES), :])):
            hi = psum.astype(BF16)
            lo = (psum - hi.astype(F32)).astype(BF16)
            score += _dot_tn(imp, hi) + _dot_tn(imp, lo)
        score = jnp.where(forced, BIG, jnp.where(allowed, score, -BIG))
        unpicked = jnp.ones((LANES, TQ), F32)
        for _ in range(N_SEL):
            best = jnp.max(score, axis=0, keepdims=True)
            first = jnp.min(jnp.where(score == best, blkf, float(LANES)), axis=0, keepdims=True)
            pick = blkf == first
            unpicked = jnp.where(pick, 0.0, unpicked)
            score = jnp.where(pick, -jnp.inf, score)
        notsel = jnp.maximum(jnp.where(allowed, 0.0, 1.0), unpicked).astype(BF16)
        for c in range(NSA_HPG):
            lhs_scr[:LANES, c * TQ:(c + 1) * TQ] = notsel
        lhs_scr[LANES:, :] = qp

        def sel_logits(kt):
            k0 = pl.multiple_of(kt * TQ, TQ)
            return _dot(jnp.concatenate([eneg_ref[pl.ds(k0, TQ), :], _key_tile(ks_ref, kt)], axis=1), lhs_scr[...])

        _flash_causal(
            qt, sel_logits,
            lambda kt, kind, b: None if kind is None else t_ref[kind, :, (g * NSA_HPG + b) * TQ:(g * NSA_HPG + b + 1) * TQ],
            lambda kt: _key_tile(vs_ref, kt),
            ((s0_scr, p0_scr, a0_scr), (s1_scr, p1_scr, a1_scr)), m_scr, l_scr, acc_scr)
        res_scr[1, g] = acc_scr[...] / l_scr[...]

        _flash_init(m_scr, l_scr, acc_scr)

        def win_step(kt, bias):
            s = _dot(_key_tile(kw_ref, kt), qp) + bias
            _flash_update(s, _key_tile(vw_ref, kt), m_scr, l_scr, acc_scr)

        @pl.when(qt >= 2)
        def _():
            win_step(qt - 2, win_edge)

        @pl.when(qt >= 1)
        def _():
            win_step(qt - 1, t_ref[1, :, heads])

        win_step(qt, t_ref[0, :, heads])
        res_scr[2, g] = acc_scr[...] / l_scr[...]

    for c in range(NSA_HPG):
        out = jnp.zeros((TQ, LANES), F32)
        for br in range(3):
            gate = gate_ref[0, :, br * NSA_HEADS * HEAD_DIM + c * LANES:br * NSA_HEADS * HEAD_DIM + (c + 1) * LANES]
            both = jnp.concatenate([res_scr[br, 0, :HEAD_DIM, c * TQ:(c + 1) * TQ],
                                    res_scr[br, 1, HEAD_DIM:, c * TQ:(c + 1) * TQ]], axis=0)
            out += gate * both.T
        o_ref[0, :, c * LANES:(c + 1) * LANES] = out.astype(o_ref.dtype)


def _nsa_call(q, gates, kc, vc, kvs, kvw, imp, eneg, t_tiles, tc_tiles):
    bsz, s, _ = q.shape
    rows = kc.shape[1]
    hq = NSA_HPG * TQ
    whole = lambda col: pl.BlockSpec((1, s, LANES), lambda b, i: (b, 0, col))
    return pl.pallas_call(
        _nsa_kernel,
        out_shape=jax.ShapeDtypeStruct((bsz, s, NSA_HEADS * HEAD_DIM), BF16),
        grid=(bsz, s // TQ),
        in_specs=[
            pl.BlockSpec((1, TQ, NSA_HEADS * HEAD_DIM), lambda b, i: (b, i, 0)),
            pl.BlockSpec((1, TQ, 3 * NSA_HEADS * HEAD_DIM), lambda b, i: (b, i, 0)),
            pl.BlockSpec((1, rows, LANES), lambda b, i: (b, 0, 0)),
            pl.BlockSpec((1, rows, LANES), lambda b, i: (b, 0, 0)),
            whole(0), whole(1), whole(0), whole(1),
            _const_spec(imp.shape),
            _const_spec(eneg.shape),
            pl.BlockSpec((2, TQ, NSA_HEADS * TQ), lambda b, i: (0, 0, 0)),
            pl.BlockSpec((LANES, NSA_HEADS * TQ), lambda b, i: (0, 0)),
        ],
        out_specs=pl.BlockSpec((1, TQ, NSA_HEADS * HEAD_DIM), lambda b, i: (b, i, 0)),
        scratch_shapes=[
            pltpu.VMEM((NSA_GROUPS, LANES, hq), BF16),
            pltpu.VMEM((2 * LANES, hq), BF16),
            pltpu.VMEM((3, NSA_GROUPS, LANES, hq), F32),
        ] + _flash_scratch(hq),
        compiler_params=_params("parallel", "arbitrary"),
        name="nsa_attention",
    )(q, gates, kc, vc, kvs, kvs, kvw, kvw, imp, eneg, t_tiles, tc_tiles)


def _sort_key(x):
    b = lax.bitcast_convert_type(x, jnp.int32)
    return b ^ ((b >> 31) & jnp.int32(0x7FFFFFFF))


def _dsa_kernel(topk, dq_ref, ckv_ref, iq_ref, ik_ref, iw_ref, wuk_ref, wuv_ref, t_ref, o_ref,
                key_scr, lhs_scr, thr_scr, m_scr, l_scr, acc_scr, s0_scr, p0_scr, a0_scr, s1_scr, p1_scr, a1_scr):
    qt = pl.program_id(1)
    nh = DSA_HEADS
    key_i = lax.broadcasted_iota(jnp.int32, (TQ, TQ), 0)
    qry_i = lax.broadcasted_iota(jnp.int32, (TQ, TQ), 1)
    int_min = jnp.int32(-2 ** 31)

    for c in range(IDX_HEADS // 2):
        col_t = iq_ref[0, :, c * LANES:(c + 1) * LANES].astype(F32).T
        for hl in range(2):
            j = 2 * c + hl
            lhs_scr[:, j * TQ:(j + 1) * TQ] = jnp.where(_half_rows(hl, TQ), col_t, 0.0).astype(BF16)
    iw_t = iw_ref[0].T

    def idx_tile(kt, causal):
        s = _dot(_key_tile(ik_ref, kt), lhs_scr[...])
        isc = jnp.zeros((TQ, TQ), F32)
        for j in range(IDX_HEADS):
            isc += iw_t[j:j + 1, :] * jnp.maximum(s[:, j * TQ:(j + 1) * TQ], 0.0)
        if causal:
            isc = jnp.where(key_i <= qry_i, isc, NEG_INF)
        key_scr[pl.ds(pl.multiple_of(kt * TQ, TQ), TQ), :] = _sort_key(isc)

    def idx_far(kt, carry):
        idx_tile(kt, False)
        return carry

    lax.fori_loop(0, qt, idx_far, 0)
    idx_tile(qt, True)

    def count(pred_fn):
        def body(kt, cnt):
            tile = key_scr[pl.ds(pl.multiple_of(kt * TQ, TQ), TQ), :]
            hit = jnp.where(pred_fn(tile, kt), 1.0, 0.0)
            return cnt + jnp.sum(hit.reshape(TQ // 8, 8, TQ), axis=0)

        cnt = lax.fori_loop(0, qt + 1, body, jnp.zeros((8, TQ), F32))
        return jnp.sum(cnt, axis=0, keepdims=True)

    kf = float(topk)
    thr0 = jnp.where(count(lambda tile, kt: tile >= 0) >= kf, jnp.int32(0), int_min)

    def bit_body(i, thr):
        cand = thr | (jnp.int32(1) << (30 - i))
        return jnp.where(count(lambda tile, kt: tile >= cand) >= kf, cand, thr)

    thr = lax.fori_loop(0, 31, bit_body, thr0)
    thr_scr[...] = thr

    n_gt = count(lambda tile, kt: tile > thr)
    n_ge = count(lambda tile, kt: tile >= thr)
    need = kf - n_gt
    excess = jnp.max(jnp.where(n_ge - n_gt > need, 1.0, 0.0))

    @pl.when(excess > 0.0)
    def _():
        def pos_body(i, bound):
            cand = bound | (jnp.int32(1) << (30 - i))
            below = count(lambda tile, kt: (tile == thr) & (kt * TQ + key_i < cand))
            return jnp.where(below <= need - 1.0, cand, bound)

        bound = lax.fori_loop(0, 31, pos_body, jnp.zeros((1, TQ), jnp.int32))

        def demote(kt, carry):
            k0 = pl.multiple_of(kt * TQ, TQ)
            tile = key_scr[pl.ds(k0, TQ), :]
            key_scr[pl.ds(k0, TQ), :] = jnp.where((tile == thr) & (k0 + key_i > bound), tile - 1, tile)
            return carry

        lax.fori_loop(0, qt + 1, demote, 0)

    for c in range(nh // 2):
        col_t = dq_ref[0, :, c * LANES:(c + 1) * LANES].astype(F32).T
        for hl in range(2):
            h = 2 * c + hl
            qm = jnp.where(_half_rows(hl, TQ), col_t, 0.0).astype(BF16)
            lhs_scr[:, h * TQ:(h + 1) * TQ] = (_dot(wuk_ref[c], qm) * (HEAD_DIM ** -0.5 * LOG2E)).astype(BF16)

    def masked(kt, kind, b):
        keep = jnp.where(key_scr[pl.ds(pl.multiple_of(kt * TQ, TQ), TQ), :] >= thr_scr[...], 0.0, NEG_INF)
        return keep if kind is None else keep + t_ref[kind, :, b * TQ:(b + 1) * TQ]

    _flash_causal(
        qt,
        lambda kt: _dot(_key_tile(ckv_ref, kt), lhs_scr[...]),
        masked,
        lambda kt: _key_tile(ckv_ref, kt),
        ((s0_scr, p0_scr, a0_scr), (s1_scr, p1_scr, a1_scr)), m_scr, l_scr, acc_scr)

    o_lat = (acc_scr[...] / l_scr[...]).astype(BF16)
    for c in range(nh // 2):
        out = (_dot(wuv_ref[2 * c], o_lat[:, (2 * c) * TQ:(2 * c + 1) * TQ])
               + _dot(wuv_ref[2 * c + 1], o_lat[:, (2 * c + 1) * TQ:(2 * c + 2) * TQ]))
        o_ref[0, :, c * LANES:(c + 1) * LANES] = out.T.astype(o_ref.dtype)


def _dsa_call(dq, ckv, iq, ik2, iw, wuk, wuv, t_tiles, topk):
    bsz, s, _ = dq.shape
    nh = DSA_HEADS
    hq = nh * TQ
    return pl.pallas_call(
        functools.partial(_dsa_kernel, topk),
        out_shape=jax.ShapeDtypeStruct((bsz, s, nh * HEAD_DIM), BF16),
        grid=(bsz, s // TQ),
        in_specs=[
            pl.BlockSpec((1, TQ, nh * HEAD_DIM), lambda b, i: (b, i, 0)),
            pl.BlockSpec((1, s, LANES), lambda b, i: (b, 0, 0)),
            pl.BlockSpec((1, TQ, IDX_HEADS * IDX_DIM), lambda b, i: (b, i, 0)),
            pl.BlockSpec((1, s, LANES), lambda b, i: (b, 0, 0)),
            pl.BlockSpec((1, TQ, LANES), lambda b, i: (b, i, 0)),
            _const_spec(wuk.shape),
            _const_spec(wuv.shape),
            pl.BlockSpec((2, TQ, hq), lambda b, i: (0, 0, 1)),
        ],
        out_specs=pl.BlockSpec((1, TQ, nh * HEAD_DIM), lambda b, i: (b, i, 0)),
        scratch_shapes=[
            pltpu.VMEM((s, TQ), jnp.int32),
            pltpu.VMEM((LANES, hq), BF16),
            pltpu.VMEM((1, TQ), jnp.int32),
        ] + _flash_scratch(hq),
        compiler_params=_params("parallel", "arbitrary"),
        name="dsa_attention",
    )(dq, ckv, iq, ik2, iw, wuk, wuv, t_tiles)


_NSA_Q = NSA_HEADS * HEAD_DIM
_NSA_KV = 2 * NSA_GROUPS * HEAD_DIM
_EVEN_SIZES = (_NSA_Q, _NSA_KV, _NSA_KV, _NSA_KV, NSA_HEADS * 3, DSA_HEADS * HEAD_DIM, DSA_KV_RANK,
               IDX_HEADS * IDX_DIM, IDX_DIM, IDX_HEADS)
_EVEN_OFF = np.concatenate([[0], np.cumsum(_EVEN_SIZES)])
_ZERO_COL = int(_EVEN_OFF[-1])


def _nsa_head_perm():
    pos = np.zeros(_NSA_Q, np.int64)
    for c in range(NSA_HPG):
        for g in range(NSA_GROUPS):
            for d in range(HEAD_DIM):
                pos[c * LANES + g * HEAD_DIM + d] = (g * NSA_HPG + c) * HEAD_DIM + d
    return pos


def _even_columns():
    o = _EVEN_OFF
    perm = _nsa_head_perm()
    gate = np.concatenate([o[4] + (perm // HEAD_DIM) * 3 + br for br in range(3)])
    idx_w = np.concatenate([np.arange(o[9], o[10]), np.full(LANES - IDX_HEADS, _ZERO_COL)])
    cols = [
        o[0] + perm,
        np.arange(o[1], o[2]), np.arange(o[2], o[3]), np.arange(o[3], o[4]),
        gate,
        np.arange(o[5], o[6]), np.arange(o[6], o[7]), np.arange(o[7], o[8]),
        np.concatenate([np.arange(o[8], o[9])] * 2),
        idx_w,
    ]
    segs = (
        (_NSA_Q, BF16, None), (_NSA_KV, F32, None), (_NSA_KV, BF16, None), (_NSA_KV, BF16, None),
        (3 * _NSA_Q, F32, "sigmoid"),
        (DSA_HEADS * HEAD_DIM, BF16, None), (DSA_KV_RANK, BF16, "rms"), (IDX_HEADS * IDX_DIM, BF16, None),
        (LANES, BF16, None), (LANES, F32, float(IDX_HEADS ** -0.5 * IDX_DIM ** -0.5)),
    )
    return np.concatenate(cols).astype(np.int32), segs


def _block_diag2(w):
    z = jnp.zeros_like(w)
    return jnp.concatenate([jnp.concatenate([w, z], axis=-1), jnp.concatenate([z, w], axis=-1)], axis=-2)


def _selection_constants(s):
    nc = s // CMP_STRIDE
    nb = s // SEL_BLK
    rows = _cmp_rows(s)
    cs = np.arange(nc) * CMP_STRIDE
    ss = np.arange(nb) * SEL_BLK
    overlap = np.clip(np.minimum(cs[:, None] + CMP_LEN, ss[None, :] + SEL_BLK) - np.maximum(cs[:, None], ss[None, :]), 0, None)
    imp = np.zeros((rows, LANES), np.float32)
    imp[CMP_PAD:CMP_PAD + nc, :nb] = overlap / CMP_STRIDE
    imp[CMP_PAD + nc - 1] = 0.0
    eneg = np.zeros((s, LANES), np.float32)
    eneg[np.arange(s), np.arange(s) // SEL_BLK] = NEG_INF
    return jnp.asarray(imp, BF16), jnp.asarray(eneg, BF16)


def kernel(x, c, rel_bias, ada_w, ada_b, ln_g, ln_b, ev_w_in, ev_w_out, nsa_pe_k, nsa_pe_v, nsa_w1_k, nsa_w2_k, nsa_w1_v, nsa_w2_v, dsa_kv_norm, dsa_w_uk, dsa_w_uv, od_w_in, od_w_out, diff_lam, diff_subln, mlp_w1, mlp_w2):
    bsz, s, d = x.shape
    depth = ada_w.shape[0]
    assert s % TQ == 0 and s // SEL_BLK <= LANES
    alpha = (2 * depth) ** 0.25
    topk = min(DSA_TOPK, s // 4)

    ada = _ada_call(c, ada_w, ada_b)
    t_tiles, tc_tiles = _bias_call(rel_bias)
    imp, eneg = _selection_constants(s)
    even_cols, even_segs = _even_columns()
    perm = _nsa_head_perm()
    odd_segs = ((od_w_in.shape[-1], BF16, None),)
    unit_gain = jnp.ones((1, LANES), F32)

    for l in range(depth):
        sh1, sc1, g1, sh2, sc2, g2 = [a.reshape(bsz, 1, d) for a in jnp.split(ada[l], 6, axis=-1)]
        i = l // 2
        if l % 2 == 0:
            w_aug = jnp.concatenate([ev_w_in[i], jnp.zeros((d, 1), F32)], axis=1)
            w_in = jnp.take(w_aug, even_cols, axis=1).astype(BF16)
            nsa_q, kvc, kvs, kvw, gates, dsa_q, ckv, idx_q, ik2, idx_w = _proj_call(
                x, sh1, sc1, w_in, dsa_kv_norm[i].reshape(1, LANES), even_segs)
            pe2 = jnp.stack([jnp.tile(nsa_pe_k[i], (1, 2)), jnp.tile(nsa_pe_v[i], (1, 2))])
            w1bd = jnp.stack([_block_diag2(w.reshape(CMP_LEN, HEAD_DIM, CMP_HIDDEN)) for w in (nsa_w1_k[i], nsa_w1_v[i])]).astype(BF16)
            w2bd = jnp.stack([_block_diag2(nsa_w2_k[i]), _block_diag2(nsa_w2_v[i])]).astype(BF16)
            kc, vc = _compress_call(kvc, pe2, w1bd, w2bd)
            o_a = _nsa_call(nsa_q, gates, kc, vc, kvs, kvw, imp, eneg, t_tiles, tc_tiles)
            wuk = dsa_w_uk[i].reshape(DSA_KV_RANK, DSA_HEADS // 2, LANES).transpose(1, 0, 2).astype(BF16)
            wuv_h = dsa_w_uv[i].transpose(1, 0, 2)
            zeros = jnp.zeros_like(wuv_h)
            even_h = (np.arange(DSA_HEADS) % 2 == 0)[:, None, None]
            wuv = jnp.where(even_h, jnp.concatenate([wuv_h, zeros], -1), jnp.concatenate([zeros, wuv_h], -1))
            wuv = wuv.transpose(0, 2, 1).astype(BF16)
            o_b = _dsa_call(dsa_q, ckv, idx_q, ik2, idx_w, wuk, wuv, t_tiles, topk)
            attn = jnp.concatenate([o_a, o_b], axis=-1)
            w_out = jnp.concatenate([ev_w_out[i][perm], ev_w_out[i][_NSA_Q:]], axis=0).astype(BF16)
        else:
            lambda_init = 0.8 - 0.6 * math.exp(-0.3 * l)
            (qkv,) = _proj_call(x, sh1, sc1, od_w_in[i].astype(BF16), unit_gain, odd_segs)
            attn = _diff_call(qkv, t_tiles, diff_lam[i], diff_subln[i].reshape(1, LANES), lambda_init)
            w_out = od_w_out[i].astype(BF16)
        x = _out_call(x, attn, w_out, g1, ln_g[l, 0].reshape(1, d), ln_b[l, 0].reshape(1, d), alpha)
        x = _mlp_call(x, sh2, sc2, g2, mlp_w1[l].astype(BF16), mlp_w2[l].astype(BF16),
                      ln_g[l, 1].reshape(1, d), ln_b[l, 1].reshape(1, d), alpha)
    return x
```

```python
import functools
import math

import jax
import jax.numpy as jnp
import numpy as np
from jax import lax
from jax.experimental import pallas as pl
from jax.experimental.pallas import tpu as pltpu

HEAD_DIM = 64
NSA_HEADS = 8
NSA_GROUPS = 2
NSA_HPG = NSA_HEADS // NSA_GROUPS
CMP_LEN = 32
CMP_STRIDE = 16
CMP_HIDDEN = 256
SEL_BLK = 64
N_SEL = 16
WINDOW = 512
DSA_HEADS = 8
DSA_KV_RANK = 128
IDX_HEADS = 8
IDX_DIM = 64
DSA_TOPK = 256
DIFF_HEADS = 8
DIFF_DIM = 64
NUM_BUCKETS = 32
MAX_DISTANCE = 128
NEG_INF = -1e30
BIG = 1e9
EPS = 1e-5

LANES = 128
TQ = 256
DIFF_NQ = 1
COUNT_ROWS = 64
CMP_PAD = 16
CMP_WIN = 32
VMEM_LIMIT = 56 * 1024 * 1024
LOG2E = math.log2(math.e)

F32 = jnp.float32
BF16 = jnp.bfloat16


def _bucket_table():
    n = np.arange(MAX_DISTANCE)
    exact = NUM_BUCKETS // 2
    nf = np.maximum(n, 1).astype(np.float64)
    large = exact + (np.log(nf / exact) / math.log(MAX_DISTANCE / exact) * (NUM_BUCKETS - exact)).astype(np.int64)
    return np.where(n < exact, n, np.minimum(large, NUM_BUCKETS - 1)).astype(np.int32)


def _dot(a, b):
    return jnp.dot(a, b, preferred_element_type=F32)


def _dot_tn(a, b):
    return lax.dot_general(a, b, (((0,), (0,)), ((), ())), preferred_element_type=F32)


def _params(*sem):
    return pltpu.CompilerParams(dimension_semantics=sem, vmem_limit_bytes=VMEM_LIMIT)


def _const_spec(shape):
    nd = len(shape)
    return pl.BlockSpec(shape, lambda *_: (0,) * nd)


def _ada_kernel(c_ref, w_ref, b_ref, o_ref):
    c = c_ref[...]
    ca = c * jax.nn.sigmoid(c)
    o_ref[0] = _dot(ca, w_ref[0]) + b_ref[0]


def _ada_call(c, ada_w, ada_b):
    depth, d, n = ada_w.shape
    bsz = c.shape[0]
    rows = 8
    cp = jnp.pad(c, ((0, rows - bsz), (0, 0)))
    tn = 1536
    out = pl.pallas_call(
        _ada_kernel,
        out_shape=jax.ShapeDtypeStruct((depth, rows, n), F32),
        grid=(depth, n // tn),
        in_specs=[
            pl.BlockSpec((rows, d), lambda l, j: (0, 0)),
            pl.BlockSpec((1, d, tn), lambda l, j: (l, 0, j)),
            pl.BlockSpec((1, 1, tn), lambda l, j: (l, 0, j)),
        ],
        out_specs=pl.BlockSpec((1, rows, tn), lambda l, j: (l, 0, j)),
        compiler_params=_params("arbitrary", "arbitrary"),
        name="ada_table",
    )(cp, ada_w, ada_b.reshape(depth, 1, n))
    return out[:, :bsz]


def _proj_kernel(segs, x_ref, sh_ref, sc_ref, w_ref, gain_ref, *o_refs):
    h = (x_ref[0] * (1.0 + sc_ref[0]) + sh_ref[0]).astype(BF16)
    off = 0
    for (width, _, epi), o_ref in zip(segs, o_refs):
        y = _dot(h, w_ref[:, off:off + width])
        if epi == "sigmoid":
            y = jax.nn.sigmoid(y)
        elif epi == "rms":
            y = y * lax.rsqrt(jnp.mean(jnp.square(y), axis=-1, keepdims=True) + EPS) * gain_ref[...]
        elif isinstance(epi, float):
            y = y * epi
        o_ref[0] = y.astype(o_ref.dtype)
        off += width


def _proj_call(x, sh, sc, w, gain, segs, tm=256):
    bsz, s, d = x.shape
    ntot = w.shape[1]
    return pl.pallas_call(
        functools.partial(_proj_kernel, segs),
        out_shape=[jax.ShapeDtypeStruct((bsz, s, width), dt) for width, dt, _ in segs],
        grid=(bsz, s // tm),
        in_specs=[
            pl.BlockSpec((1, tm, d), lambda b, i: (b, i, 0)),
            pl.BlockSpec((1, 1, d), lambda b, i: (b, 0, 0)),
            pl.BlockSpec((1, 1, d), lambda b, i: (b, 0, 0)),
            _const_spec((d, ntot)),
            _const_spec((1, LANES)),
        ],
        out_specs=[pl.BlockSpec((1, tm, width), lambda b, i: (b, i, 0)) for width, _, _ in segs],
        compiler_params=_params("parallel", "parallel"),
        name="modulate_in_proj",
    )(x, sh, sc, w, gain)


def _layer_norm(z, g, b):
    mu = jnp.mean(z, axis=-1, keepdims=True)
    zc = z - mu
    var = jnp.mean(jnp.square(zc), axis=-1, keepdims=True)
    return zc * lax.rsqrt(var + EPS) * g + b


def _out_kernel(alpha, x_ref, a_ref, w_ref, g1_ref, lg_ref, lb_ref, o_ref):
    y = _dot(a_ref[0], w_ref[...])
    z = alpha * x_ref[0] + (1.0 + g1_ref[0]) * y
    o_ref[0] = _layer_norm(z, lg_ref[...], lb_ref[...])


def _out_call(x, attn, w_out, g1, ln_g, ln_b, alpha, tm=512):
    bsz, s, d = x.shape
    k = attn.shape[-1]
    return pl.pallas_call(
        functools.partial(_out_kernel, alpha),
        out_shape=jax.ShapeDtypeStruct((bsz, s, d), F32),
        grid=(bsz, s // tm),
        in_specs=[
            pl.BlockSpec((1, tm, d), lambda b, i: (b, i, 0)),
            pl.BlockSpec((1, tm, k), lambda b, i: (b, i, 0)),
            _const_spec((k, d)),
            pl.BlockSpec((1, 1, d), lambda b, i: (b, 0, 0)),
            _const_spec((1, d)),
            _const_spec((1, d)),
        ],
        out_specs=pl.BlockSpec((1, tm, d), lambda b, i: (b, i, 0)),
        compiler_params=_params("parallel", "parallel"),
        name="out_proj_ln",
    )(x, attn, w_out, g1, ln_g, ln_b)


def _mlp_kernel(alpha, x_ref, sh_ref, sc_ref, g2_ref, w1_ref, w2_ref, lg_ref, lb_ref, o_ref, h_scr, acc_scr):
    f = pl.program_id(2)

    @pl.when(f == 0)
    def _():
        h_scr[...] = (x_ref[0] * (1.0 + sc_ref[0]) + sh_ref[0]).astype(BF16)
        acc_scr[...] = jnp.zeros_like(acc_scr)

    a = jnp.maximum(_dot(h_scr[...], w1_ref[...]), 0.0)
    acc_scr[...] += _dot((a * a).astype(BF16), w2_ref[...])

    @pl.when(f == pl.num_programs(2) - 1)
    def _():
        z = alpha * x_ref[0] + (1.0 + g2_ref[0]) * acc_scr[...]
        o_ref[0] = _layer_norm(z, lg_ref[...], lb_ref[...])


def _mlp_call(x, sh, sc, g2, w1, w2, ln_g, ln_b, alpha, tm=1024, tf=512):
    bsz, s, d = x.shape
    ff = w1.shape[1]
    tm = min(tm, s)
    assert s % tm == 0
    return pl.pallas_call(
        functools.partial(_mlp_kernel, alpha),
        out_shape=jax.ShapeDtypeStruct((bsz, s, d), F32),
        grid=(bsz, s // tm, ff // tf),
        in_specs=[
            pl.BlockSpec((1, tm, d), lambda b, i, f: (b, i, 0)),
            pl.BlockSpec((1, 1, d), lambda b, i, f: (b, 0, 0)),
            pl.BlockSpec((1, 1, d), lambda b, i, f: (b, 0, 0)),
            pl.BlockSpec((1, 1, d), lambda b, i, f: (b, 0, 0)),
            pl.BlockSpec((d, tf), lambda b, i, f: (0, f)),
            pl.BlockSpec((tf, d), lambda b, i, f: (f, 0)),
            _const_spec((1, d)),
            _const_spec((1, d)),
        ],
        out_specs=pl.BlockSpec((1, tm, d), lambda b, i, f: (b, i, 0)),
        scratch_shapes=[pltpu.VMEM((tm, d), BF16), pltpu.VMEM((tm, d), F32)],
        compiler_params=_params("parallel", "parallel", "arbitrary"),
        name="mlp_ln",
    )(x, sh, sc, g2, w1, w2, ln_g, ln_b)


def _bias_kernel(tab_ref, t_ref, tc_ref):
    h = pl.program_id(0)
    key = lax.broadcasted_iota(jnp.int32, (TQ, TQ), 0)
    qry = lax.broadcasted_iota(jnp.int32, (TQ, TQ), 1)
    win = lax.broadcasted_iota(jnp.int32, (LANES, TQ), 0)
    qry_c = lax.broadcasted_iota(jnp.int32, (LANES, TQ), 1)
    dists = (qry - key, TQ + qry - key, qry_c - CMP_STRIDE * (win - CMP_PAD) - (CMP_LEN - 1))

    def body(d, accs):
        val = tab_ref[h * MAX_DISTANCE + d] * LOG2E
        return tuple(jnp.where(dist == d, val, acc) for dist, acc in zip(dists, accs))

    accs = lax.fori_loop(0, MAX_DISTANCE, body, tuple(jnp.zeros(dist.shape, F32) for dist in dists))
    t_ref[0] = jnp.where(dists[0] < 0, NEG_INF, accs[0])
    t_ref[1] = accs[1]
    tc_ref[...] = jnp.where((dists[2] < 0) | (win >= CMP_WIN), NEG_INF, accs[2])


def _bias_call(rel_bias):
    nh = rel_bias.shape[1]
    shifted = rel_bias[_bucket_table()] - rel_bias[NUM_BUCKETS - 1]
    tab = shifted.T.reshape(-1)
    return pl.pallas_call(
        _bias_kernel,
        out_shape=[
            jax.ShapeDtypeStruct((2, TQ, nh * TQ), F32),
            jax.ShapeDtypeStruct((LANES, nh * TQ), F32),
        ],
        grid=(nh,),
        in_specs=[pl.BlockSpec(memory_space=pltpu.SMEM)],
        out_specs=[
            pl.BlockSpec((2, TQ, TQ), lambda h: (0, 0, h)),
            pl.BlockSpec((LANES, TQ), lambda h: (0, h)),
        ],
        compiler_params=_params("arbitrary"),
        name="t5_bias_tiles",
    )(tab)


def _flash_init(m_ref, l_ref, acc_ref):
    m_ref[...] = jnp.full(m_ref.shape, NEG_INF, F32)
    l_ref[...] = jnp.zeros(l_ref.shape, F32)
    acc_ref[...] = jnp.zeros(acc_ref.shape, F32)


def _flash_update(s, v_tile, m_ref, l_ref, acc_ref):
    m_prev = m_ref[...]
    m_new = jnp.maximum(m_prev, jnp.max(s, axis=0, keepdims=True))
    alpha = jnp.exp2(m_prev - m_new)
    p = jnp.exp2(s - m_new)
    l_ref[...] = alpha * l_ref[...] + jnp.sum(p, axis=0, keepdims=True)
    acc_ref[...] = alpha * acc_ref[...] + _dot_tn(v_tile, p.astype(BF16))
    m_ref[...] = m_new


def _near_bias(t_ref, head, near, u):
    d = u - (near - 1)
    if d < 0:
        return NEG_INF
    if d >= 2:
        return None
    return t_ref[d, :, head * TQ:(head + 1) * TQ]


def _flash_causal(qt, nq, qk_fn, extra_fn, v_fn, bufs, m_ref, l_ref, acc_ref):
    even_set, odd_set = bufs
    sets = (even_set, odd_set)
    last = qt * nq + nq - 1
    n_far = jnp.maximum(qt * nq - 1, 0)
    _flash_init(m_ref, l_ref, acc_ref)
    for _, p_buf, a_buf in bufs:
        p_buf[...] = jnp.zeros(p_buf.shape, p_buf.dtype)
        a_buf[...] = jnp.ones(a_buf.shape, F32)

    def logits(kt, dst):
        dst[0][...] = qk_fn(kt)

    def softmax(kt, cur, kind):
        width = cur[0].shape[1]
        blocks = [slice(b * TQ, (b + 1) * TQ) for b in range(width // TQ)]
        adds = [extra_fn(kt, kind, b) for b in range(len(blocks))]
        tops = []
        for blk, add in zip(blocks, adds):
            s = cur[0][:, blk]
            tops.append(jnp.max(s if add is None else s + add, axis=0, keepdims=True))
        m_prev = m_ref[...]
        m_new = jnp.maximum(m_prev, jnp.concatenate(tops, axis=1))
        alpha = jnp.exp2(m_prev - m_new)
        sums = []
        for blk, add in zip(blocks, adds):
            z = cur[0][:, blk] - m_new[:, blk]
            p = jnp.exp2(z if add is None else z + add)
            sums.append(jnp.sum(p, axis=0, keepdims=True))
            cur[1][:, blk] = p.astype(BF16)
        l_ref[...] = alpha * l_ref[...] + jnp.concatenate(sums, axis=1)
        m_ref[...] = m_new
        cur[2][...] = alpha

    def values(kt, src):
        acc_ref[...] = src[2][...] * acc_ref[...] + _dot_tn(v_fn(jnp.maximum(kt, 0)), src[1][...])

    def trip(kt, cur, nxt):
        logits(kt + 1, nxt)
        softmax(kt, cur, None)
        values(kt - 1, nxt)

    second = sets[(nq + 1) & 1]
    first = sets[nq & 1]
    first_is_even = (last & 1) == 0

    @pl.when(first_is_even)
    def _():
        logits(0, even_set)

    @pl.when(jnp.logical_not(first_is_even))
    def _():
        logits(0, odd_set)

    start = n_far & 1

    @pl.when(start == 1)
    def _():
        trip(0, second, first)

    def pair(j, carry):
        kt = start + 2 * j
        trip(kt, first, second)
        trip(kt + 1, second, first)
        return carry

    lax.fori_loop(0, (n_far - start) // 2, pair, 0)

    def near_tile(near):
        cur = sets[(nq - near) & 1]
        kt = last - (nq - near)
        if near < nq:
            logits(kt + 1, sets[(nq - near - 1) & 1])
        softmax(kt, cur, near)
        values(kt - 1, sets[(nq - near + 1) & 1])

    pl.when(qt >= 1)(functools.partial(near_tile, 0))
    for near in range(1, nq + 1):
        near_tile(near)
    values(last, even_set)


def _flash_scratch(width):
    stats = [pltpu.VMEM((1, width), F32), pltpu.VMEM((1, width), F32), pltpu.VMEM((LANES, width), F32)]
    one_set = [pltpu.VMEM((TQ, width), F32), pltpu.VMEM((TQ, width), BF16), pltpu.VMEM((1, width), F32)]
    return stats + 2 * one_set


def _half_rows(half, width):
    row = lax.broadcasted_iota(jnp.int32, (LANES, width), 0)
    return (row >= half * HEAD_DIM) & (row < (half + 1) * HEAD_DIM)


def _key_tile(ref, kt):
    return ref[0, pl.ds(pl.multiple_of(kt * TQ, TQ), TQ), :]


def _diff_kernel(out_scale, q_ref, k_ref, v_ref, t_ref, lam_ref, sub_ref, o_ref,
                 qt_scr, m_scr, l_scr, acc_scr, s0_scr, p0_scr, a0_scr, s1_scr, p1_scr, a1_scr):
    qt = pl.program_id(2)
    qw = DIFF_NQ * TQ
    q_t = (q_ref[0].astype(F32) * (DIFF_DIM ** -0.5 * LOG2E)).T
    for m in range(2):
        qt_scr[:, m * qw:(m + 1) * qw] = jnp.where(_half_rows(m, qw), q_t, 0.0).astype(BF16)

    _flash_causal(
        qt, DIFF_NQ,
        lambda kt: _dot(_key_tile(k_ref, kt), qt_scr[...]),
        lambda kt, near, b: None if near is None else _near_bias(t_ref, b // DIFF_NQ, near, b % DIFF_NQ),
        lambda kt: _key_tile(v_ref, kt),
        ((s0_scr, p0_scr, a0_scr), (s1_scr, p1_scr, a1_scr)), m_scr, l_scr, acc_scr)

    o = acc_scr[...] / l_scr[...]
    lam = lam_ref[...]
    lam_full = (jnp.exp(jnp.sum(lam[0:1] * lam[1:2], axis=1, keepdims=True))
                - jnp.exp(jnp.sum(lam[2:3] * lam[3:4], axis=1, keepdims=True)) + (1.0 - out_scale))
    od = (o[:, :qw] - lam_full * o[:, qw:]).T
    od = od * lax.rsqrt(jnp.mean(jnp.square(od), axis=-1, keepdims=True) + EPS) * sub_ref[...]
    o_ref[0] = (od * out_scale).astype(o_ref.dtype)


def _diff_call(qkv, t_tiles, lam, subln, lambda_init):
    bsz, s, _ = qkv.shape
    nh = DIFF_HEADS
    qw = DIFF_NQ * TQ
    assert s % qw == 0
    return pl.pallas_call(
        functools.partial(_diff_kernel, 1.0 - lambda_init),
        out_shape=jax.ShapeDtypeStruct((bsz, s, nh * LANES), BF16),
        grid=(bsz, nh, s // qw),
        in_specs=[
            pl.BlockSpec((1, qw, LANES), lambda b, h, i: (b, i, h)),
            pl.BlockSpec((1, s, LANES), lambda b, h, i: (b, 0, nh + h)),
            pl.BlockSpec((1, s, LANES), lambda b, h, i: (b, 0, 2 * nh + h)),
            pl.BlockSpec((2, TQ, 2 * TQ), lambda b, h, i: (0, 0, h)),
            _const_spec((4, DIFF_DIM)),
            _const_spec((1, LANES)),
        ],
        out_specs=pl.BlockSpec((1, qw, LANES), lambda b, h, i: (b, i, h)),
        scratch_shapes=[pltpu.VMEM((LANES, 2 * qw), BF16)] + _flash_scratch(2 * qw),
        compiler_params=_params("parallel", "parallel", "arbitrary"),
        name="diff_attention",
    )(qkv, qkv, qkv, t_tiles, lam, subln)


def _compress_kernel(nc, xk_ref, xv_ref, pe_ref, w1_ref, w2_ref, kc_ref, vc_ref):
    half = CMP_LEN // 2
    for kv, (x_ref, o_ref) in enumerate(((xk_ref, kc_ref), (xv_ref, vc_ref))):
        first = jnp.zeros((nc, NSA_GROUPS * CMP_HIDDEN), F32)
        second = jnp.zeros_like(first)
        for l in range(half):
            x = x_ref[0, pl.ds(l, nc, stride=CMP_STRIDE), :]
            first += _dot((x + pe_ref[kv, l:l + 1]).astype(BF16), w1_ref[kv, l])
            second += _dot((x + pe_ref[kv, half + l:half + l + 1]).astype(BF16), w1_ref[kv, half + l])
        pre = first + pltpu.roll(second, nc - 1, 0)
        hid = jax.nn.gelu(pre, approximate=True)
        o_ref[0] = jnp.zeros(o_ref.shape[1:], o_ref.dtype)
        o_ref[0, CMP_PAD:CMP_PAD + nc, :] = _dot(hid.astype(BF16), w2_ref[kv]).astype(o_ref.dtype)


def _cmp_rows(s):
    nc = s // CMP_STRIDE
    return -(-(nc + LANES) // LANES) * LANES


def _compress_call(kvc, pe2, w1bd, w2bd):
    bsz, s, _ = kvc.shape
    nc = s // CMP_STRIDE
    rows = _cmp_rows(s)
    return pl.pallas_call(
        functools.partial(_compress_kernel, nc),
        out_shape=[jax.ShapeDtypeStruct((bsz, rows, LANES), BF16)] * 2,
        grid=(bsz,),
        in_specs=[
            pl.BlockSpec((1, s, LANES), lambda b: (b, 0, 0)),
            pl.BlockSpec((1, s, LANES), lambda b: (b, 0, 1)),
            _const_spec(pe2.shape),
            _const_spec(w1bd.shape),
            _const_spec(w2bd.shape),
        ],
        out_specs=[pl.BlockSpec((1, rows, LANES), lambda b: (b, 0, 0))] * 2,
        compiler_params=_params("parallel"),
        name="nsa_compress",
    )(kvc, kvc, pe2, w1bd, w2bd)


def _nsa_kernel(q_ref, gate_ref, kc_ref, vc_ref, ks_ref, vs_ref, kw_ref, vw_ref, imp_ref, eneg_ref, t_ref, tc_ref,
                o_ref, qpad_scr, lhs_scr, res_scr, m_scr, l_scr, acc_scr, s0_scr, p0_scr, a0_scr, s1_scr, p1_scr, a1_scr):
    qt = pl.program_id(1)
    q0 = qt * TQ
    n0 = pl.multiple_of(qt * (TQ // CMP_STRIDE), TQ // CMP_STRIDE)
    rows = kc_ref.shape[1]
    hq = NSA_HPG * TQ

    for c in range(NSA_HPG):
        q_t = (q_ref[0, :, c * LANES:(c + 1) * LANES].astype(F32) * (HEAD_DIM ** -0.5 * LOG2E)).T
        for g in range(NSA_GROUPS):
            qpad_scr[g, :, c * TQ:(c + 1) * TQ] = jnp.where(_half_rows(g, TQ), q_t, 0.0).astype(BF16)

    blk = lax.broadcasted_iota(jnp.int32, (LANES, TQ), 0)
    blkf = blk.astype(F32)
    cur = (q0 + lax.broadcasted_iota(jnp.int32, (LANES, TQ), 1)) // SEL_BLK
    allowed = blk <= cur
    forced = (blk == 0) | (blk == cur) | (blk == cur - 1)
    rowp = lax.broadcasted_iota(jnp.int32, (rows, TQ), 0)
    far_ok = (rowp >= CMP_PAD) & (rowp < n0)
    near_ok = (n0 + lax.broadcasted_iota(jnp.int32, (LANES, TQ), 0)) >= CMP_PAD
    key_i = lax.broadcasted_iota(jnp.int32, (TQ, hq), 0)
    qry_i = lax.broadcasted_iota(jnp.int32, (TQ, hq), 1) & (TQ - 1)
    win_edge = jnp.where(key_i > qry_i, 0.0, NEG_INF)

    for g in range(NSA_GROUPS):
        qp = qpad_scr[g]
        heads = slice(g * hq, (g + 1) * hq)
        kc_near = kc_ref[0, pl.ds(n0, LANES), :]
        vc_near = vc_ref[0, pl.ds(n0, LANES), :]
        s_far = _dot(kc_ref[0], qp)
        s_near = _dot(kc_near, qp) + tc_ref[:, heads]
        psum_far = jnp.zeros((rows, TQ), F32)
        psum_near = jnp.zeros((LANES, TQ), F32)
        p_far, p_near = [], []
        for c in range(NSA_HPG):
            sf = jnp.where(far_ok, s_far[:, c * TQ:(c + 1) * TQ], NEG_INF)
            sn = jnp.where(near_ok, s_near[:, c * TQ:(c + 1) * TQ], NEG_INF)
            mx = jnp.maximum(jnp.max(sf, axis=0, keepdims=True), jnp.max(sn, axis=0, keepdims=True))
            pf = jnp.where(sf > 0.5 * NEG_INF, jnp.exp2(sf - mx), 0.0)
            pn = jnp.where(sn > 0.5 * NEG_INF, jnp.exp2(sn - mx), 0.0)
            den = jnp.sum(pf, axis=0, keepdims=True) + jnp.sum(pn, axis=0, keepdims=True)
            inv = 1.0 / jnp.maximum(den, 1e-30)
            pf = pf * inv
            pn = pn * inv
            psum_far += pf
            psum_near += pn
            p_far.append(pf.astype(BF16))
            p_near.append(pn.astype(BF16))
        res_scr[0, g] = (_dot_tn(vc_ref[0], jnp.concatenate(p_far, axis=1))
                         + _dot_tn(vc_near, jnp.concatenate(p_near, axis=1)))
        score = jnp.zeros((LANES, TQ), F32)
        for psum, imp in ((psum_far, imp_ref[...]), (psum_near, imp_ref[pl.ds(n0, LANES), :])):
            hi = psum.astype(BF16)
            lo = (psum - hi.astype(F32)).astype(BF16)
            score += _dot_tn(imp, hi) + _dot_tn(imp, lo)
        score = jnp.where(forced, BIG, jnp.where(allowed, score, -BIG))
        unpicked = jnp.ones((LANES, TQ), F32)
        for _ in range(N_SEL):
            best = jnp.max(score, axis=0, keepdims=True)
            first = jnp.min(jnp.where(score == best, blkf, float(LANES)), axis=0, keepdims=True)
            pick = blkf == first
            unpicked = jnp.where(pick, 0.0, unpicked)
            score = jnp.where(pick, -jnp.inf, score)
        notsel = jnp.maximum(jnp.where(allowed, 0.0, 1.0), unpicked).astype(BF16)
        for c in range(NSA_HPG):
            lhs_scr[:LANES, c * TQ:(c + 1) * TQ] = notsel
        lhs_scr[LANES:, :] = qp

        def sel_logits(kt):
            k0 = pl.multiple_of(kt * TQ, TQ)
            return _dot(jnp.concatenate([eneg_ref[pl.ds(k0, TQ), :], _key_tile(ks_ref, kt)], axis=1), lhs_scr[...])

        _flash_causal(
            qt, 1, sel_logits,
            lambda kt, near, b: None if near is None else _near_bias(t_ref, g * NSA_HPG + b, near, 0),
            lambda kt: _key_tile(vs_ref, kt),
            ((s0_scr, p0_scr, a0_scr), (s1_scr, p1_scr, a1_scr)), m_scr, l_scr, acc_scr)
        res_scr[1, g] = acc_scr[...] / l_scr[...]

        _flash_init(m_scr, l_scr, acc_scr)

        def win_step(kt, bias):
            s = _dot(_key_tile(kw_ref, kt), qp) + bias
            _flash_update(s, _key_tile(vw_ref, kt), m_scr, l_scr, acc_scr)

        @pl.when(qt >= 2)
        def _():
            win_step(qt - 2, win_edge)

        @pl.when(qt >= 1)
        def _():
            win_step(qt - 1, t_ref[1, :, heads])

        win_step(qt, t_ref[0, :, heads])
        res_scr[2, g] = acc_scr[...] / l_scr[...]

    for c in range(NSA_HPG):
        out = jnp.zeros((TQ, LANES), F32)
        for br in range(3):
            gate = gate_ref[0, :, br * NSA_HEADS * HEAD_DIM + c * LANES:br * NSA_HEADS * HEAD_DIM + (c + 1) * LANES]
            both = jnp.concatenate([res_scr[br, 0, :HEAD_DIM, c * TQ:(c + 1) * TQ],
                                    res_scr[br, 1, HEAD_DIM:, c * TQ:(c + 1) * TQ]], axis=0)
            out += gate * both.T
        o_ref[0, :, c * LANES:(c + 1) * LANES] = out.astype(o_ref.dtype)


def _nsa_call(q, gates, kc, vc, kvs, kvw, imp, eneg, t_tiles, tc_tiles):
    bsz, s, _ = q.shape
    rows = kc.shape[1]
    hq = NSA_HPG * TQ
    whole = lambda col: pl.BlockSpec((1, s, LANES), lambda b, i: (b, 0, col))
    return pl.pallas_call(
        _nsa_kernel,
        out_shape=jax.ShapeDtypeStruct((bsz, s, NSA_HEADS * HEAD_DIM), BF16),
        grid=(bsz, s // TQ),
        in_specs=[
            pl.BlockSpec((1, TQ, NSA_HEADS * HEAD_DIM), lambda b, i: (b, i, 0)),
            pl.BlockSpec((1, TQ, 3 * NSA_HEADS * HEAD_DIM), lambda b, i: (b, i, 0)),
            pl.BlockSpec((1, rows, LANES), lambda b, i: (b, 0, 0)),
            pl.BlockSpec((1, rows, LANES), lambda b, i: (b, 0, 0)),
            whole(0), whole(1), whole(0), whole(1),
            _const_spec(imp.shape),
            _const_spec(eneg.shape),
            pl.BlockSpec((2, TQ, NSA_HEADS * TQ), lambda b, i: (0, 0, 0)),
            pl.BlockSpec((LANES, NSA_HEADS * TQ), lambda b, i: (0, 0)),
        ],
        out_specs=pl.BlockSpec((1, TQ, NSA_HEADS * HEAD_DIM), lambda b, i: (b, i, 0)),
        scratch_shapes=[
            pltpu.VMEM((NSA_GROUPS, LANES, hq), BF16),
            pltpu.VMEM((2 * LANES, hq), BF16),
            pltpu.VMEM((3, NSA_GROUPS, LANES, hq), F32),
        ] + _flash_scratch(hq),
        compiler_params=_params("parallel", "arbitrary"),
        name="nsa_attention",
    )(q, gates, kc, vc, kvs, kvs, kvw, kvw, imp, eneg, t_tiles, tc_tiles)


def _sort_key(x):
    b = lax.bitcast_convert_type(x, jnp.int32)
    return b ^ ((b >> 31) & jnp.int32(0x7FFFFFFF))


def _dsa_kernel(topk, dq_ref, ckv_ref, iq_ref, ik_ref, iw_ref, wuk_ref, wuv_ref, t_ref, o_ref,
                key_scr, lhs_scr, thr_scr, m_scr, l_scr, acc_scr, s0_scr, p0_scr, a0_scr, s1_scr, p1_scr, a1_scr):
    qt = pl.program_id(1)
    nh = DSA_HEADS
    key_i = lax.broadcasted_iota(jnp.int32, (TQ, TQ), 0)
    qry_i = lax.broadcasted_iota(jnp.int32, (TQ, TQ), 1)
    int_min = jnp.int32(-2 ** 31)

    for c in range(IDX_HEADS // 2):
        col_t = iq_ref[0, :, c * LANES:(c + 1) * LANES].astype(F32).T
        for hl in range(2):
            j = 2 * c + hl
            lhs_scr[:, j * TQ:(j + 1) * TQ] = jnp.where(_half_rows(hl, TQ), col_t, 0.0).astype(BF16)
    iw_t = iw_ref[0].T

    def idx_tile(kt, causal):
        s = _dot(_key_tile(ik_ref, kt), lhs_scr[...])
        isc = jnp.zeros((TQ, TQ), F32)
        for j in range(IDX_HEADS):
            isc += iw_t[j:j + 1, :] * jnp.maximum(s[:, j * TQ:(j + 1) * TQ], 0.0)
        if causal:
            isc = jnp.where(key_i <= qry_i, isc, NEG_INF)
        key_scr[pl.ds(pl.multiple_of(kt * TQ, TQ), TQ), :] = _sort_key(isc)

    def idx_far(kt, carry):
        idx_tile(kt, False)
        return carry

    lax.fori_loop(0, qt, idx_far, 0)
    idx_tile(qt, True)

    @pl.when((qt & 1) == 0)
    def _():
        key_scr[pl.ds(pl.multiple_of((qt + 1) * TQ, TQ), TQ), :] = jnp.full((TQ, TQ), int_min, jnp.int32)

    def count(pred_fn):
        def body(kp, cnt):
            k0 = pl.multiple_of(kp * (2 * TQ), 2 * TQ)
            tile = key_scr[pl.ds(k0, 2 * TQ), :]
            hit = jnp.where(pred_fn(tile, k0), 1.0, 0.0)
            return cnt + jnp.sum(hit.reshape(2 * TQ // COUNT_ROWS, COUNT_ROWS, TQ), axis=0)

        cnt = lax.fori_loop(0, (qt + 2) // 2, body, jnp.zeros((COUNT_ROWS, TQ), F32))
        return jnp.sum(cnt, axis=0, keepdims=True)

    kf = float(topk)
    n_pos = count(lambda tile, k0: tile >= 0)
    thr0 = jnp.where(n_pos >= kf, jnp.int32(0), int_min)
    n_all = (qt + 1).astype(F32) * TQ

    def bit_body(i, carry):
        thr, n_ge = carry
        cand = thr | (jnp.int32(1) << (30 - i))
        n_cand = count(lambda tile, k0: tile >= cand)
        ok = n_cand >= kf
        return jnp.where(ok, cand, thr), jnp.where(ok, n_cand, n_ge)

    thr, n_ge = lax.fori_loop(0, 31, bit_body, (thr0, jnp.where(n_pos >= kf, n_pos, n_all)))
    thr_scr[...] = thr

    excess = jnp.max(jnp.where(n_ge > kf, 1.0, 0.0))

    @pl.when(excess > 0.0)
    def _():
        need = kf - count(lambda tile, k0: tile > thr)
        pos_i = lax.broadcasted_iota(jnp.int32, (2 * TQ, TQ), 0)

        def pos_body(i, bound):
            cand = bound | (jnp.int32(1) << (30 - i))
            below = count(lambda tile, k0: (tile == thr) & (k0 + pos_i < cand))
            return jnp.where(below <= need - 1.0, cand, bound)

        bound = lax.fori_loop(0, 31, pos_body, jnp.zeros((1, TQ), jnp.int32))

        def demote(kt, carry):
            k0 = pl.multiple_of(kt * TQ, TQ)
            tile = key_scr[pl.ds(k0, TQ), :]
            key_scr[pl.ds(k0, TQ), :] = jnp.where((tile == thr) & (k0 + key_i > bound), tile - 1, tile)
            return carry

        lax.fori_loop(0, qt + 1, demote, 0)

    for c in range(nh // 2):
        col_t = dq_ref[0, :, c * LANES:(c + 1) * LANES].astype(F32).T
        for hl in range(2):
            h = 2 * c + hl
            qm = jnp.where(_half_rows(hl, TQ), col_t, 0.0).astype(BF16)
            lhs_scr[:, h * TQ:(h + 1) * TQ] = (_dot(wuk_ref[c], qm) * (HEAD_DIM ** -0.5 * LOG2E)).astype(BF16)

    def masked(kt, near, b):
        keep = jnp.where(key_scr[pl.ds(pl.multiple_of(kt * TQ, TQ), TQ), :] >= thr_scr[...], 0.0, NEG_INF)
        return keep if near is None else keep + _near_bias(t_ref, b, near, 0)

    _flash_causal(
        qt, 1,
        lambda kt: _dot(_key_tile(ckv_ref, kt), lhs_scr[...]),
        masked,
        lambda kt: _key_tile(ckv_ref, kt),
        ((s0_scr, p0_scr, a0_scr), (s1_scr, p1_scr, a1_scr)), m_scr, l_scr, acc_scr)

    o_lat = (acc_scr[...] / l_scr[...]).astype(BF16)
    for c in range(nh // 2):
        out = (_dot(wuv_ref[2 * c], o_lat[:, (2 * c) * TQ:(2 * c + 1) * TQ])
               + _dot(wuv_ref[2 * c + 1], o_lat[:, (2 * c + 1) * TQ:(2 * c + 2) * TQ]))
        o_ref[0, :, c * LANES:(c + 1) * LANES] = out.T.astype(o_ref.dtype)


def _dsa_call(dq, ckv, iq, ik2, iw, wuk, wuv, t_tiles, topk):
    bsz, s, _ = dq.shape
    nh = DSA_HEADS
    hq = nh * TQ
    return pl.pallas_call(
        functools.partial(_dsa_kernel, topk),
        out_shape=jax.ShapeDtypeStruct((bsz, s, nh * HEAD_DIM), BF16),
        grid=(bsz, s // TQ),
        in_specs=[
            pl.BlockSpec((1, TQ, nh * HEAD_DIM), lambda b, i: (b, i, 0)),
            pl.BlockSpec((1, s, LANES), lambda b, i: (b, 0, 0)),
            pl.BlockSpec((1, TQ, IDX_HEADS * IDX_DIM), lambda b, i: (b, i, 0)),
            pl.BlockSpec((1, s, LANES), lambda b, i: (b, 0, 0)),
            pl.BlockSpec((1, TQ, LANES), lambda b, i: (b, i, 0)),
            _const_spec(wuk.shape),
            _const_spec(wuv.shape),
            pl.BlockSpec((2, TQ, hq), lambda b, i: (0, 0, 1)),
        ],
        out_specs=pl.BlockSpec((1, TQ, nh * HEAD_DIM), lambda b, i: (b, i, 0)),
        scratch_shapes=[
            pltpu.VMEM((s + TQ, TQ), jnp.int32),
            pltpu.VMEM((LANES, hq), BF16),
            pltpu.VMEM((1, TQ), jnp.int32),
        ] + _flash_scratch(hq),
        compiler_params=_params("parallel", "arbitrary"),
        name="dsa_attention",
    )(dq, ckv, iq, ik2, iw, wuk, wuv, t_tiles)


_NSA_Q = NSA_HEADS * HEAD_DIM
_NSA_KV = 2 * NSA_GROUPS * HEAD_DIM
_EVEN_SIZES = (_NSA_Q, _NSA_KV, _NSA_KV, _NSA_KV, NSA_HEADS * 3, DSA_HEADS * HEAD_DIM, DSA_KV_RANK,
               IDX_HEADS * IDX_DIM, IDX_DIM, IDX_HEADS)
_EVEN_OFF = np.concatenate([[0], np.cumsum(_EVEN_SIZES)])
_ZERO_COL = int(_EVEN_OFF[-1])


def _nsa_head_perm():
    pos = np.zeros(_NSA_Q, np.int64)
    for c in range(NSA_HPG):
        for g in range(NSA_GROUPS):
            for d in range(HEAD_DIM):
                pos[c * LANES + g * HEAD_DIM + d] = (g * NSA_HPG + c) * HEAD_DIM + d
    return pos


def _even_columns():
    o = _EVEN_OFF
    perm = _nsa_head_perm()
    gate = np.concatenate([o[4] + (perm // HEAD_DIM) * 3 + br for br in range(3)])
    idx_w = np.concatenate([np.arange(o[9], o[10]), np.full(LANES - IDX_HEADS, _ZERO_COL)])
    cols = [
        o[0] + perm,
        np.arange(o[1], o[2]), np.arange(o[2], o[3]), np.arange(o[3], o[4]),
        gate,
        np.arange(o[5], o[6]), np.arange(o[6], o[7]), np.arange(o[7], o[8]),
        np.concatenate([np.arange(o[8], o[9])] * 2),
        idx_w,
    ]
    segs = (
        (_NSA_Q, BF16, None), (_NSA_KV, F32, None), (_NSA_KV, BF16, None), (_NSA_KV, BF16, None),
        (3 * _NSA_Q, F32, "sigmoid"),
        (DSA_HEADS * HEAD_DIM, BF16, None), (DSA_KV_RANK, BF16, "rms"), (IDX_HEADS * IDX_DIM, BF16, None),
        (LANES, BF16, None), (LANES, F32, float(IDX_HEADS ** -0.5 * IDX_DIM ** -0.5)),
    )
    return np.concatenate(cols).astype(np.int32), segs


def _block_diag2(w):
    z = jnp.zeros_like(w)
    return jnp.concatenate([jnp.concatenate([w, z], axis=-1), jnp.concatenate([z, w], axis=-1)], axis=-2)


def _selection_constants(s):
    nc = s // CMP_STRIDE
    nb = s // SEL_BLK
    rows = _cmp_rows(s)
    cs = np.arange(nc) * CMP_STRIDE
    ss = np.arange(nb) * SEL_BLK
    overlap = np.clip(np.minimum(cs[:, None] + CMP_LEN, ss[None, :] + SEL_BLK) - np.maximum(cs[:, None], ss[None, :]), 0, None)
    imp = np.zeros((rows, LANES), np.float32)
    imp[CMP_PAD:CMP_PAD + nc, :nb] = overlap / CMP_STRIDE
    imp[CMP_PAD + nc - 1] = 0.0
    eneg = np.zeros((s, LANES), np.float32)
    eneg[np.arange(s), np.arange(s) // SEL_BLK] = NEG_INF
    return jnp.asarray(imp, BF16), jnp.asarray(eneg, BF16)


def kernel(x, c, rel_bias, ada_w, ada_b, ln_g, ln_b, ev_w_in, ev_w_out, nsa_pe_k, nsa_pe_v, nsa_w1_k, nsa_w2_k, nsa_w1_v, nsa_w2_v, dsa_kv_norm, dsa_w_uk, dsa_w_uv, od_w_in, od_w_out, diff_lam, diff_subln, mlp_w1, mlp_w2):
    bsz, s, d = x.shape
    depth = ada_w.shape[0]
    assert s % TQ == 0 and s // SEL_BLK <= LANES
    alpha = (2 * depth) ** 0.25
    topk = min(DSA_TOPK, s // 4)

    ada = _ada_call(c, ada_w, ada_b)
    t_tiles, tc_tiles = _bias_call(rel_bias)
    imp, eneg = _selection_constants(s)
    even_cols, even_segs = _even_columns()
    perm = _nsa_head_perm()
    odd_segs = ((od_w_in.shape[-1], BF16, None),)
    unit_gain = jnp.ones((1, LANES), F32)

    for l in range(depth):
        sh1, sc1, g1, sh2, sc2, g2 = [a.reshape(bsz, 1, d) for a in jnp.split(ada[l], 6, axis=-1)]
        i = l // 2
        if l % 2 == 0:
            w_aug = jnp.concatenate([ev_w_in[i], jnp.zeros((d, 1), F32)], axis=1)
            w_in = jnp.take(w_aug, even_cols, axis=1).astype(BF16)
            nsa_q, kvc, kvs, kvw, gates, dsa_q, ckv, idx_q, ik2, idx_w = _proj_call(
                x, sh1, sc1, w_in, dsa_kv_norm[i].reshape(1, LANES), even_segs)
            pe2 = jnp.stack([jnp.tile(nsa_pe_k[i], (1, 2)), jnp.tile(nsa_pe_v[i], (1, 2))])
            w1bd = jnp.stack([_block_diag2(w.reshape(CMP_LEN, HEAD_DIM, CMP_HIDDEN)) for w in (nsa_w1_k[i], nsa_w1_v[i])]).astype(BF16)
            w2bd = jnp.stack([_block_diag2(nsa_w2_k[i]), _block_diag2(nsa_w2_v[i])]).astype(BF16)
            kc, vc = _compress_call(kvc, pe2, w1bd, w2bd)
            o_a = _nsa_call(nsa_q, gates, kc, vc, kvs, kvw, imp, eneg, t_tiles, tc_tiles)
            wuk = dsa_w_uk[i].reshape(DSA_KV_RANK, DSA_HEADS // 2, LANES).transpose(1, 0, 2).astype(BF16)
            wuv_h = dsa_w_uv[i].transpose(1, 0, 2)
            zeros = jnp.zeros_like(wuv_h)
            even_h = (np.arange(DSA_HEADS) % 2 == 0)[:, None, None]
            wuv = jnp.where(even_h, jnp.concatenate([wuv_h, zeros], -1), jnp.concatenate([zeros, wuv_h], -1))
            wuv = wuv.transpose(0, 2, 1).astype(BF16)
            o_b = _dsa_call(dsa_q, ckv, idx_q, ik2, idx_w, wuk, wuv, t_tiles, topk)
            attn = jnp.concatenate([o_a, o_b], axis=-1)
            w_out = jnp.concatenate([ev_w_out[i][perm], ev_w_out[i][_NSA_Q:]], axis=0).astype(BF16)
        else:
            lambda_init = 0.8 - 0.6 * math.exp(-0.3 * l)
            (qkv,) = _proj_call(x, sh1, sc1, od_w_in[i].astype(BF16), unit_gain, odd_segs)
            attn = _diff_call(qkv, t_tiles, diff_lam[i], diff_subln[i].reshape(1, LANES), lambda_init)
            w_out = od_w_out[i].astype(BF16)
        x = _out_call(x, attn, w_out, g1, ln_g[l, 0].reshape(1, d), ln_b[l, 0].reshape(1, d), alpha)
        x = _mlp_call(x, sh2, sc2, g2, mlp_w1[l].astype(BF16), mlp_w2[l].astype(BF16),
                      ln_g[l, 1].reshape(1, d), ln_b[l, 1].reshape(1, d), alpha)
    return x
```

```python
import functools
import math

import jax
import jax.numpy as jnp
import numpy as np
from jax import lax
from jax.experimental import pallas as pl
from jax.experimental.pallas import tpu as pltpu

HEAD_DIM = 64
NSA_HEADS = 8
NSA_GROUPS = 2
NSA_HPG = NSA_HEADS // NSA_GROUPS
CMP_LEN = 32
CMP_STRIDE = 16
CMP_HIDDEN = 256
SEL_BLK = 64
N_SEL = 16
WINDOW = 512
DSA_HEADS = 8
DSA_KV_RANK = 128
IDX_HEADS = 8
IDX_DIM = 64
DSA_TOPK = 256
DIFF_HEADS = 8
DIFF_DIM = 64
NUM_BUCKETS = 32
MAX_DISTANCE = 128
NEG_INF = -1e30
BIG = 1e9
EPS = 1e-5

LANES = 128
TQ = 256
DIFF_NQ = 1
COUNT_ROWS = 64
MIN_NORMAL = float(np.finfo(np.float32).tiny)
CMP_PAD = 16
CMP_WIN = 32
VMEM_LIMIT = 56 * 1024 * 1024
LOG2E = math.log2(math.e)

F32 = jnp.float32
BF16 = jnp.bfloat16


def _bucket_table():
    n = np.arange(MAX_DISTANCE)
    exact = NUM_BUCKETS // 2
    nf = np.maximum(n, 1).astype(np.float64)
    large = exact + (np.log(nf / exact) / math.log(MAX_DISTANCE / exact) * (NUM_BUCKETS - exact)).astype(np.int64)
    return np.where(n < exact, n, np.minimum(large, NUM_BUCKETS - 1)).astype(np.int32)


def _dot(a, b):
    return jnp.dot(a, b, preferred_element_type=F32)


def _dot_tn(a, b):
    return lax.dot_general(a, b, (((0,), (0,)), ((), ())), preferred_element_type=F32)


def _params(*sem):
    return pltpu.CompilerParams(dimension_semantics=sem, vmem_limit_bytes=VMEM_LIMIT)


def _const_spec(shape):
    nd = len(shape)
    return pl.BlockSpec(shape, lambda *_: (0,) * nd)


def _ada_kernel(c_ref, w_ref, b_ref, o_ref):
    c = c_ref[...]
    ca = c * jax.nn.sigmoid(c)
    o_ref[0] = _dot(ca, w_ref[0]) + b_ref[0]


def _ada_call(c, ada_w, ada_b):
    depth, d, n = ada_w.shape
    bsz = c.shape[0]
    rows = 8
    cp = jnp.pad(c, ((0, rows - bsz), (0, 0)))
    tn = 1536
    out = pl.pallas_call(
        _ada_kernel,
        out_shape=jax.ShapeDtypeStruct((depth, rows, n), F32),
        grid=(depth, n // tn),
        in_specs=[
            pl.BlockSpec((rows, d), lambda l, j: (0, 0)),
            pl.BlockSpec((1, d, tn), lambda l, j: (l, 0, j)),
            pl.BlockSpec((1, 1, tn), lambda l, j: (l, 0, j)),
        ],
        out_specs=pl.BlockSpec((1, rows, tn), lambda l, j: (l, 0, j)),
        compiler_params=_params("arbitrary", "arbitrary"),
        name="ada_table",
    )(cp, ada_w, ada_b.reshape(depth, 1, n))
    return out[:, :bsz]


def _proj_kernel(segs, x_ref, sh_ref, sc_ref, w_ref, gain_ref, *o_refs):
    h = (x_ref[0] * (1.0 + sc_ref[0]) + sh_ref[0]).astype(BF16)
    off = 0
    for (width, _, epi), o_ref in zip(segs, o_refs):
        y = _dot(h, w_ref[:, off:off + width])
        if epi == "sigmoid":
            y = jax.nn.sigmoid(y)
        elif epi == "rms":
            y = y * lax.rsqrt(jnp.mean(jnp.square(y), axis=-1, keepdims=True) + EPS) * gain_ref[...]
        elif isinstance(epi, float):
            y = y * epi
        o_ref[0] = y.astype(o_ref.dtype)
        off += width


def _proj_call(x, sh, sc, w, gain, segs, tm=256):
    bsz, s, d = x.shape
    ntot = w.shape[1]
    return pl.pallas_call(
        functools.partial(_proj_kernel, segs),
        out_shape=[jax.ShapeDtypeStruct((bsz, s, width), dt) for width, dt, _ in segs],
        grid=(bsz, s // tm),
        in_specs=[
            pl.BlockSpec((1, tm, d), lambda b, i: (b, i, 0)),
            pl.BlockSpec((1, 1, d), lambda b, i: (b, 0, 0)),
            pl.BlockSpec((1, 1, d), lambda b, i: (b, 0, 0)),
            _const_spec((d, ntot)),
            _const_spec((1, LANES)),
        ],
        out_specs=[pl.BlockSpec((1, tm, width), lambda b, i: (b, i, 0)) for width, _, _ in segs],
        compiler_params=_params("parallel", "parallel"),
        name="modulate_in_proj",
    )(x, sh, sc, w, gain)


def _layer_norm(z, g, b):
    mu = jnp.mean(z, axis=-1, keepdims=True)
    zc = z - mu
    var = jnp.mean(jnp.square(zc), axis=-1, keepdims=True)
    return zc * lax.rsqrt(var + EPS) * g + b


def _out_kernel(alpha, x_ref, a_ref, w_ref, g1_ref, lg_ref, lb_ref, o_ref):
    y = _dot(a_ref[0], w_ref[...])
    z = alpha * x_ref[0] + (1.0 + g1_ref[0]) * y
    o_ref[0] = _layer_norm(z, lg_ref[...], lb_ref[...])


def _out_call(x, attn, w_out, g1, ln_g, ln_b, alpha, tm=512):
    bsz, s, d = x.shape
    k = attn.shape[-1]
    return pl.pallas_call(
        functools.partial(_out_kernel, alpha),
        out_shape=jax.ShapeDtypeStruct((bsz, s, d), F32),
        grid=(bsz, s // tm),
        in_specs=[
            pl.BlockSpec((1, tm, d), lambda b, i: (b, i, 0)),
            pl.BlockSpec((1, tm, k), lambda b, i: (b, i, 0)),
            _const_spec((k, d)),
            pl.BlockSpec((1, 1, d), lambda b, i: (b, 0, 0)),
            _const_spec((1, d)),
            _const_spec((1, d)),
        ],
        out_specs=pl.BlockSpec((1, tm, d), lambda b, i: (b, i, 0)),
        compiler_params=_params("parallel", "parallel"),
        name="out_proj_ln",
    )(x, attn, w_out, g1, ln_g, ln_b)


def _mlp_kernel(alpha, x_ref, sh_ref, sc_ref, g2_ref, w1_ref, w2_ref, lg_ref, lb_ref, o_ref, h_scr, acc_scr):
    f = pl.program_id(2)

    @pl.when(f == 0)
    def _():
        h_scr[...] = (x_ref[0] * (1.0 + sc_ref[0]) + sh_ref[0]).astype(BF16)
        acc_scr[...] = jnp.zeros_like(acc_scr)

    a = jnp.maximum(_dot(h_scr[...], w1_ref[...]), 0.0)
    acc_scr[...] += _dot((a * a).astype(BF16), w2_ref[...])

    @pl.when(f == pl.num_programs(2) - 1)
    def _():
        z = alpha * x_ref[0] + (1.0 + g2_ref[0]) * acc_scr[...]
        o_ref[0] = _layer_norm(z, lg_ref[...], lb_ref[...])


def _mlp_call(x, sh, sc, g2, w1, w2, ln_g, ln_b, alpha, tm=1024, tf=512):
    bsz, s, d = x.shape
    ff = w1.shape[1]
    tm = min(tm, s)
    assert s % tm == 0
    return pl.pallas_call(
        functools.partial(_mlp_kernel, alpha),
        out_shape=jax.ShapeDtypeStruct((bsz, s, d), F32),
        grid=(bsz, s // tm, ff // tf),
        in_specs=[
            pl.BlockSpec((1, tm, d), lambda b, i, f: (b, i, 0)),
            pl.BlockSpec((1, 1, d), lambda b, i, f: (b, 0, 0)),
            pl.BlockSpec((1, 1, d), lambda b, i, f: (b, 0, 0)),
            pl.BlockSpec((1, 1, d), lambda b, i, f: (b, 0, 0)),
            pl.BlockSpec((d, tf), lambda b, i, f: (0, f)),
            pl.BlockSpec((tf, d), lambda b, i, f: (f, 0)),
            _const_spec((1, d)),
            _const_spec((1, d)),
        ],
        out_specs=pl.BlockSpec((1, tm, d), lambda b, i, f: (b, i, 0)),
        scratch_shapes=[pltpu.VMEM((tm, d), BF16), pltpu.VMEM((tm, d), F32)],
        compiler_params=_params("parallel", "parallel", "arbitrary"),
        name="mlp_ln",
    )(x, sh, sc, g2, w1, w2, ln_g, ln_b)


def _bias_kernel(tab_ref, t_ref, tc_ref):
    h = pl.program_id(0)
    key = lax.broadcasted_iota(jnp.int32, (TQ, TQ), 0)
    qry = lax.broadcasted_iota(jnp.int32, (TQ, TQ), 1)
    win = lax.broadcasted_iota(jnp.int32, (LANES, TQ), 0)
    qry_c = lax.broadcasted_iota(jnp.int32, (LANES, TQ), 1)
    dists = (qry - key, TQ + qry - key, qry_c - CMP_STRIDE * (win - CMP_PAD) - (CMP_LEN - 1))

    def body(d, accs):
        val = tab_ref[h * MAX_DISTANCE + d] * LOG2E
        return tuple(jnp.where(dist == d, val, acc) for dist, acc in zip(dists, accs))

    accs = lax.fori_loop(0, MAX_DISTANCE, body, tuple(jnp.zeros(dist.shape, F32) for dist in dists))
    t_ref[0] = jnp.where(dists[0] < 0, NEG_INF, accs[0])
    t_ref[1] = accs[1]
    tc_ref[...] = jnp.where((dists[2] < 0) | (win >= CMP_WIN), NEG_INF, accs[2])


def _bias_call(rel_bias):
    nh = rel_bias.shape[1]
    shifted = rel_bias[_bucket_table()] - rel_bias[NUM_BUCKETS - 1]
    tab = shifted.T.reshape(-1)
    return pl.pallas_call(
        _bias_kernel,
        out_shape=[
            jax.ShapeDtypeStruct((2, TQ, nh * TQ), F32),
            jax.ShapeDtypeStruct((LANES, nh * TQ), F32),
        ],
        grid=(nh,),
        in_specs=[pl.BlockSpec(memory_space=pltpu.SMEM)],
        out_specs=[
            pl.BlockSpec((2, TQ, TQ), lambda h: (0, 0, h)),
            pl.BlockSpec((LANES, TQ), lambda h: (0, h)),
        ],
        compiler_params=_params("arbitrary"),
        name="t5_bias_tiles",
    )(tab)


def _flash_init(m_ref, l_ref, acc_ref):
    m_ref[...] = jnp.full(m_ref.shape, NEG_INF, F32)
    l_ref[...] = jnp.zeros(l_ref.shape, F32)
    acc_ref[...] = jnp.zeros(acc_ref.shape, F32)


def _flash_update(s, v_tile, m_ref, l_ref, acc_ref):
    m_prev = m_ref[...]
    m_new = jnp.maximum(m_prev, jnp.max(s, axis=0, keepdims=True))
    alpha = jnp.exp2(m_prev - m_new)
    p = jnp.exp2(s - m_new)
    l_ref[...] = alpha * l_ref[...] + jnp.sum(p, axis=0, keepdims=True)
    acc_ref[...] = alpha * acc_ref[...] + _dot_tn(v_tile, p.astype(BF16))
    m_ref[...] = m_new


def _near_bias(t_ref, head, near, u):
    d = u - (near - 1)
    if d < 0:
        return NEG_INF
    if d >= 2:
        return None
    return t_ref[d, :, head * TQ:(head + 1) * TQ]


def _flash_causal(qt, nq, qk_fn, extra_fn, v_fn, bufs, m_ref, l_ref, acc_ref):
    even_set, odd_set = bufs
    sets = (even_set, odd_set)
    last = qt * nq + nq - 1
    n_far = jnp.maximum(qt * nq - 1, 0)
    _flash_init(m_ref, l_ref, acc_ref)
    for _, p_buf, a_buf in bufs:
        p_buf[...] = jnp.zeros(p_buf.shape, p_buf.dtype)
        a_buf[...] = jnp.ones(a_buf.shape, F32)

    def logits(kt, dst):
        dst[0][...] = qk_fn(kt)

    def softmax(kt, cur, kind):
        width = cur[0].shape[1]
        blocks = [slice(b * TQ, (b + 1) * TQ) for b in range(width // TQ)]
        adds = [extra_fn(kt, kind, b) for b in range(len(blocks))]
        tops = []
        for blk, add in zip(blocks, adds):
            s = cur[0][:, blk]
            tops.append(jnp.max(s if add is None else s + add, axis=0, keepdims=True))
        m_prev = m_ref[...]
        m_new = jnp.maximum(m_prev, jnp.concatenate(tops, axis=1))
        alpha = jnp.exp2(m_prev - m_new)
        sums = []
        for blk, add in zip(blocks, adds):
            z = cur[0][:, blk] - m_new[:, blk]
            p = jnp.exp2(z if add is None else z + add)
            sums.append(jnp.sum(p, axis=0, keepdims=True))
            cur[1][:, blk] = p.astype(BF16)
        l_ref[...] = alpha * l_ref[...] + jnp.concatenate(sums, axis=1)
        m_ref[...] = m_new
        cur[2][...] = alpha

    def values(kt, src):
        acc_ref[...] = src[2][...] * acc_ref[...] + _dot_tn(v_fn(jnp.maximum(kt, 0)), src[1][...])

    def trip(kt, cur, nxt):
        logits(kt + 1, nxt)
        softmax(kt, cur, None)
        values(kt - 1, nxt)

    second = sets[(nq + 1) & 1]
    first = sets[nq & 1]
    first_is_even = (last & 1) == 0

    @pl.when(first_is_even)
    def _():
        logits(0, even_set)

    @pl.when(jnp.logical_not(first_is_even))
    def _():
        logits(0, odd_set)

    start = n_far & 1

    @pl.when(start == 1)
    def _():
        trip(0, second, first)

    def pair(j, carry):
        kt = start + 2 * j
        trip(kt, first, second)
        trip(kt + 1, second, first)
        return carry

    lax.fori_loop(0, (n_far - start) // 2, pair, 0)

    def near_tile(near):
        cur = sets[(nq - near) & 1]
        kt = last - (nq - near)
        if near < nq:
            logits(kt + 1, sets[(nq - near - 1) & 1])
        softmax(kt, cur, near)
        values(kt - 1, sets[(nq - near + 1) & 1])

    pl.when(qt >= 1)(functools.partial(near_tile, 0))
    for near in range(1, nq + 1):
        near_tile(near)
    values(last, even_set)


def _flash_scratch(width):
    stats = [pltpu.VMEM((1, width), F32), pltpu.VMEM((1, width), F32), pltpu.VMEM((LANES, width), F32)]
    one_set = [pltpu.VMEM((TQ, width), F32), pltpu.VMEM((TQ, width), BF16), pltpu.VMEM((1, width), F32)]
    return stats + 2 * one_set


def _half_rows(half, width):
    row = lax.broadcasted_iota(jnp.int32, (LANES, width), 0)
    return (row >= half * HEAD_DIM) & (row < (half + 1) * HEAD_DIM)


def _key_tile(ref, kt):
    return ref[0, pl.ds(pl.multiple_of(kt * TQ, TQ), TQ), :]


def _diff_kernel(out_scale, q_ref, k_ref, v_ref, t_ref, lam_ref, sub_ref, o_ref,
                 qt_scr, m_scr, l_scr, acc_scr, s0_scr, p0_scr, a0_scr, s1_scr, p1_scr, a1_scr):
    qt = pl.program_id(2)
    qw = DIFF_NQ * TQ
    q_t = (q_ref[0].astype(F32) * (DIFF_DIM ** -0.5 * LOG2E)).T
    for m in range(2):
        qt_scr[:, m * qw:(m + 1) * qw] = jnp.where(_half_rows(m, qw), q_t, 0.0).astype(BF16)

    _flash_causal(
        qt, DIFF_NQ,
        lambda kt: _dot(_key_tile(k_ref, kt), qt_scr[...]),
        lambda kt, near, b: None if near is None else _near_bias(t_ref, b // DIFF_NQ, near, b % DIFF_NQ),
        lambda kt: _key_tile(v_ref, kt),
        ((s0_scr, p0_scr, a0_scr), (s1_scr, p1_scr, a1_scr)), m_scr, l_scr, acc_scr)

    o = acc_scr[...] / l_scr[...]
    lam = lam_ref[...]
    lam_full = (jnp.exp(jnp.sum(lam[0:1] * lam[1:2], axis=1, keepdims=True))
                - jnp.exp(jnp.sum(lam[2:3] * lam[3:4], axis=1, keepdims=True)) + (1.0 - out_scale))
    od = (o[:, :qw] - lam_full * o[:, qw:]).T
    od = od * lax.rsqrt(jnp.mean(jnp.square(od), axis=-1, keepdims=True) + EPS) * sub_ref[...]
    o_ref[0] = (od * out_scale).astype(o_ref.dtype)


def _diff_call(qkv, t_tiles, lam, subln, lambda_init):
    bsz, s, _ = qkv.shape
    nh = DIFF_HEADS
    qw = DIFF_NQ * TQ
    assert s % qw == 0
    return pl.pallas_call(
        functools.partial(_diff_kernel, 1.0 - lambda_init),
        out_shape=jax.ShapeDtypeStruct((bsz, s, nh * LANES), BF16),
        grid=(bsz, nh, s // qw),
        in_specs=[
            pl.BlockSpec((1, qw, LANES), lambda b, h, i: (b, i, h)),
            pl.BlockSpec((1, s, LANES), lambda b, h, i: (b, 0, nh + h)),
            pl.BlockSpec((1, s, LANES), lambda b, h, i: (b, 0, 2 * nh + h)),
            pl.BlockSpec((2, TQ, 2 * TQ), lambda b, h, i: (0, 0, h)),
            _const_spec((4, DIFF_DIM)),
            _const_spec((1, LANES)),
        ],
        out_specs=pl.BlockSpec((1, qw, LANES), lambda b, h, i: (b, i, h)),
        scratch_shapes=[pltpu.VMEM((LANES, 2 * qw), BF16)] + _flash_scratch(2 * qw),
        compiler_params=_params("parallel", "parallel", "arbitrary"),
        name="diff_attention",
    )(qkv, qkv, qkv, t_tiles, lam, subln)


def _compress_kernel(nc, xk_ref, xv_ref, pe_ref, w1_ref, w2_ref, kc_ref, vc_ref):
    half = CMP_LEN // 2
    for kv, (x_ref, o_ref) in enumerate(((xk_ref, kc_ref), (xv_ref, vc_ref))):
        first = jnp.zeros((nc, NSA_GROUPS * CMP_HIDDEN), F32)
        second = jnp.zeros_like(first)
        for l in range(half):
            x = x_ref[0, pl.ds(l, nc, stride=CMP_STRIDE), :]
            first += _dot((x + pe_ref[kv, l:l + 1]).astype(BF16), w1_ref[kv, l])
            second += _dot((x + pe_ref[kv, half + l:half + l + 1]).astype(BF16), w1_ref[kv, half + l])
        pre = first + pltpu.roll(second, nc - 1, 0)
        hid = jax.nn.gelu(pre, approximate=True)
        o_ref[0] = jnp.zeros(o_ref.shape[1:], o_ref.dtype)
        o_ref[0, CMP_PAD:CMP_PAD + nc, :] = _dot(hid.astype(BF16), w2_ref[kv]).astype(o_ref.dtype)


def _cmp_rows(s):
    nc = s // CMP_STRIDE
    return -(-(nc + LANES) // LANES) * LANES


def _compress_call(kvc, pe2, w1bd, w2bd):
    bsz, s, _ = kvc.shape
    nc = s // CMP_STRIDE
    rows = _cmp_rows(s)
    return pl.pallas_call(
        functools.partial(_compress_kernel, nc),
        out_shape=[jax.ShapeDtypeStruct((bsz, rows, LANES), BF16)] * 2,
        grid=(bsz,),
        in_specs=[
            pl.BlockSpec((1, s, LANES), lambda b: (b, 0, 0)),
            pl.BlockSpec((1, s, LANES), lambda b: (b, 0, 1)),
            _const_spec(pe2.shape),
            _const_spec(w1bd.shape),
            _const_spec(w2bd.shape),
        ],
        out_specs=[pl.BlockSpec((1, rows, LANES), lambda b: (b, 0, 0))] * 2,
        compiler_params=_params("parallel"),
        name="nsa_compress",
    )(kvc, kvc, pe2, w1bd, w2bd)


def _nsa_kernel(q_ref, gate_ref, kc_ref, vc_ref, ks_ref, vs_ref, kw_ref, vw_ref, imp_ref, eneg_ref, t_ref, tc_ref,
                o_ref, qpad_scr, lhs_scr, res_scr, m_scr, l_scr, acc_scr, s0_scr, p0_scr, a0_scr, s1_scr, p1_scr, a1_scr):
    qt = pl.program_id(1)
    q0 = qt * TQ
    n0 = pl.multiple_of(qt * (TQ // CMP_STRIDE), TQ // CMP_STRIDE)
    rows = kc_ref.shape[1]
    hq = NSA_HPG * TQ

    for c in range(NSA_HPG):
        q_t = (q_ref[0, :, c * LANES:(c + 1) * LANES].astype(F32) * (HEAD_DIM ** -0.5 * LOG2E)).T
        for g in range(NSA_GROUPS):
            qpad_scr[g, :, c * TQ:(c + 1) * TQ] = jnp.where(_half_rows(g, TQ), q_t, 0.0).astype(BF16)

    blk = lax.broadcasted_iota(jnp.int32, (LANES, TQ), 0)
    blkf = blk.astype(F32)
    cur = (q0 + lax.broadcasted_iota(jnp.int32, (LANES, TQ), 1)) // SEL_BLK
    allowed = blk <= cur
    forced = (blk == 0) | (blk == cur) | (blk == cur - 1)
    rowp = lax.broadcasted_iota(jnp.int32, (rows, TQ), 0)
    far_ok = (rowp >= CMP_PAD) & (rowp < n0)
    near_ok = (n0 + lax.broadcasted_iota(jnp.int32, (LANES, TQ), 0)) >= CMP_PAD
    key_i = lax.broadcasted_iota(jnp.int32, (TQ, hq), 0)
    qry_i = lax.broadcasted_iota(jnp.int32, (TQ, hq), 1) & (TQ - 1)
    win_edge = jnp.where(key_i > qry_i, 0.0, NEG_INF)

    for g in range(NSA_GROUPS):
        qp = qpad_scr[g]
        heads = slice(g * hq, (g + 1) * hq)
        kc_near = kc_ref[0, pl.ds(n0, LANES), :]
        vc_near = vc_ref[0, pl.ds(n0, LANES), :]
        s_far = _dot(kc_ref[0], qp)
        s_near = _dot(kc_near, qp) + tc_ref[:, heads]
        psum_far = jnp.zeros((rows, TQ), F32)
        psum_near = jnp.zeros((LANES, TQ), F32)
        p_far, p_near = [], []
        for c in range(NSA_HPG):
            sf = jnp.where(far_ok, s_far[:, c * TQ:(c + 1) * TQ], NEG_INF)
            sn = jnp.where(near_ok, s_near[:, c * TQ:(c + 1) * TQ], NEG_INF)
            mx = jnp.maximum(jnp.max(sf, axis=0, keepdims=True), jnp.max(sn, axis=0, keepdims=True))
            pf = jnp.where(sf > 0.5 * NEG_INF, jnp.exp2(sf - mx), 0.0)
            pn = jnp.where(sn > 0.5 * NEG_INF, jnp.exp2(sn - mx), 0.0)
            den = jnp.sum(pf, axis=0, keepdims=True) + jnp.sum(pn, axis=0, keepdims=True)
            inv = 1.0 / jnp.maximum(den, 1e-30)
            pf = pf * inv
            pn = pn * inv
            psum_far += pf
            psum_near += pn
            p_far.append(pf.astype(BF16))
            p_near.append(pn.astype(BF16))
        res_scr[0, g] = (_dot_tn(vc_ref[0], jnp.concatenate(p_far, axis=1))
                         + _dot_tn(vc_near, jnp.concatenate(p_near, axis=1)))
        score = jnp.zeros((LANES, TQ), F32)
        for psum, imp in ((psum_far, imp_ref[...]), (psum_near, imp_ref[pl.ds(n0, LANES), :])):
            hi = psum.astype(BF16)
            lo = (psum - hi.astype(F32)).astype(BF16)
            score += _dot_tn(imp, hi) + _dot_tn(imp, lo)
        score = jnp.where(forced, BIG, jnp.where(allowed, score, -BIG))
        unpicked = jnp.ones((LANES, TQ), F32)
        for _ in range(N_SEL):
            best = jnp.max(score, axis=0, keepdims=True)
            first = jnp.min(jnp.where(score == best, blkf, float(LANES)), axis=0, keepdims=True)
            pick = blkf == first
            unpicked = jnp.where(pick, 0.0, unpicked)
            score = jnp.where(pick, -jnp.inf, score)
        notsel = jnp.maximum(jnp.where(allowed, 0.0, 1.0), unpicked).astype(BF16)
        for c in range(NSA_HPG):
            lhs_scr[:LANES, c * TQ:(c + 1) * TQ] = notsel
        lhs_scr[LANES:, :] = qp

        def sel_logits(kt):
            k0 = pl.multiple_of(kt * TQ, TQ)
            return _dot(jnp.concatenate([eneg_ref[pl.ds(k0, TQ), :], _key_tile(ks_ref, kt)], axis=1), lhs_scr[...])

        _flash_causal(
            qt, 1, sel_logits,
            lambda kt, near, b: None if near is None else _near_bias(t_ref, g * NSA_HPG + b, near, 0),
            lambda kt: _key_tile(vs_ref, kt),
            ((s0_scr, p0_scr, a0_scr), (s1_scr, p1_scr, a1_scr)), m_scr, l_scr, acc_scr)
        res_scr[1, g] = acc_scr[...] / l_scr[...]

        _flash_init(m_scr, l_scr, acc_scr)

        def win_step(kt, bias):
            s = _dot(_key_tile(kw_ref, kt), qp) + bias
            _flash_update(s, _key_tile(vw_ref, kt), m_scr, l_scr, acc_scr)

        @pl.when(qt >= 2)
        def _():
            win_step(qt - 2, win_edge)

        @pl.when(qt >= 1)
        def _():
            win_step(qt - 1, t_ref[1, :, heads])

        win_step(qt, t_ref[0, :, heads])
        res_scr[2, g] = acc_scr[...] / l_scr[...]

    for c in range(NSA_HPG):
        out = jnp.zeros((TQ, LANES), F32)
        for br in range(3):
            gate = gate_ref[0, :, br * NSA_HEADS * HEAD_DIM + c * LANES:br * NSA_HEADS * HEAD_DIM + (c + 1) * LANES]
            both = jnp.concatenate([res_scr[br, 0, :HEAD_DIM, c * TQ:(c + 1) * TQ],
                                    res_scr[br, 1, HEAD_DIM:, c * TQ:(c + 1) * TQ]], axis=0)
            out += gate * both.T
        o_ref[0, :, c * LANES:(c + 1) * LANES] = out.astype(o_ref.dtype)


def _nsa_call(q, gates, kc, vc, kvs, kvw, imp, eneg, t_tiles, tc_tiles):
    bsz, s, _ = q.shape
    rows = kc.shape[1]
    hq = NSA_HPG * TQ
    whole = lambda col: pl.BlockSpec((1, s, LANES), lambda b, i: (b, 0, col))
    return pl.pallas_call(
        _nsa_kernel,
        out_shape=jax.ShapeDtypeStruct((bsz, s, NSA_HEADS * HEAD_DIM), BF16),
        grid=(bsz, s // TQ),
        in_specs=[
            pl.BlockSpec((1, TQ, NSA_HEADS * HEAD_DIM), lambda b, i: (b, i, 0)),
            pl.BlockSpec((1, TQ, 3 * NSA_HEADS * HEAD_DIM), lambda b, i: (b, i, 0)),
            pl.BlockSpec((1, rows, LANES), lambda b, i: (b, 0, 0)),
            pl.BlockSpec((1, rows, LANES), lambda b, i: (b, 0, 0)),
            whole(0), whole(1), whole(0), whole(1),
            _const_spec(imp.shape),
            _const_spec(eneg.shape),
            pl.BlockSpec((2, TQ, NSA_HEADS * TQ), lambda b, i: (0, 0, 0)),
            pl.BlockSpec((LANES, NSA_HEADS * TQ), lambda b, i: (0, 0)),
        ],
        out_specs=pl.BlockSpec((1, TQ, NSA_HEADS * HEAD_DIM), lambda b, i: (b, i, 0)),
        scratch_shapes=[
            pltpu.VMEM((NSA_GROUPS, LANES, hq), BF16),
            pltpu.VMEM((2 * LANES, hq), BF16),
            pltpu.VMEM((3, NSA_GROUPS, LANES, hq), F32),
        ] + _flash_scratch(hq),
        compiler_params=_params("parallel", "arbitrary"),
        name="nsa_attention",
    )(q, gates, kc, vc, kvs, kvs, kvw, kvw, imp, eneg, t_tiles, tc_tiles)


def _sort_key(x):
    b = lax.bitcast_convert_type(x, jnp.int32)
    return b ^ ((b >> 31) & jnp.int32(0x7FFFFFFF))


def _dsa_kernel(topk, dq_ref, ckv_ref, iq_ref, ik_ref, iw_ref, wuk_ref, wuv_ref, t_ref, o_ref,
                key_scr, lhs_scr, thr_scr, m_scr, l_scr, acc_scr, s0_scr, p0_scr, a0_scr, s1_scr, p1_scr, a1_scr):
    qt = pl.program_id(1)
    nh = DSA_HEADS
    key_i = lax.broadcasted_iota(jnp.int32, (TQ, TQ), 0)
    qry_i = lax.broadcasted_iota(jnp.int32, (TQ, TQ), 1)
    int_min = jnp.int32(-2 ** 31)

    for c in range(IDX_HEADS // 2):
        col_t = iq_ref[0, :, c * LANES:(c + 1) * LANES].astype(F32).T
        for hl in range(2):
            j = 2 * c + hl
            lhs_scr[:, j * TQ:(j + 1) * TQ] = jnp.where(_half_rows(hl, TQ), col_t, 0.0).astype(BF16)
    iw_t = iw_ref[0].T

    def idx_tile(kt, causal):
        s = _dot(_key_tile(ik_ref, kt), lhs_scr[...])
        isc = jnp.zeros((TQ, TQ), F32)
        for j in range(IDX_HEADS):
            isc += iw_t[j:j + 1, :] * jnp.maximum(s[:, j * TQ:(j + 1) * TQ], 0.0)
        if causal:
            isc = jnp.where(key_i <= qry_i, isc, NEG_INF)
        k0 = pl.multiple_of(kt * TQ, TQ)
        zero_key = -2 - (k0 + key_i)
        key_scr[pl.ds(k0, TQ), :] = jnp.where(jnp.abs(isc) < MIN_NORMAL, zero_key, _sort_key(isc))

    def idx_far(kt, carry):
        idx_tile(kt, False)
        return carry

    lax.fori_loop(0, qt, idx_far, 0)
    idx_tile(qt, True)

    @pl.when((qt & 1) == 0)
    def _():
        key_scr[pl.ds(pl.multiple_of((qt + 1) * TQ, TQ), TQ), :] = jnp.full((TQ, TQ), int_min, jnp.int32)

    def count(pred_fn):
        def body(kp, cnt):
            k0 = pl.multiple_of(kp * (2 * TQ), 2 * TQ)
            tile = key_scr[pl.ds(k0, 2 * TQ), :]
            hit = jnp.where(pred_fn(tile, k0), 1.0, 0.0)
            return cnt + jnp.sum(hit.reshape(2 * TQ // COUNT_ROWS, COUNT_ROWS, TQ), axis=0)

        cnt = lax.fori_loop(0, (qt + 2) // 2, body, jnp.zeros((COUNT_ROWS, TQ), F32))
        return jnp.sum(cnt, axis=0, keepdims=True)

    kf = float(topk)
    n_pos = count(lambda tile, k0: tile >= 0)
    thr0 = jnp.where(n_pos >= kf, jnp.int32(0), int_min)
    n_all = (qt + 1).astype(F32) * TQ

    def bit_body(i, carry):
        thr, n_ge = carry
        cand = thr | (jnp.int32(1) << (30 - i))
        n_cand = count(lambda tile, k0: tile >= cand)
        ok = n_cand >= kf
        return jnp.where(ok, cand, thr), jnp.where(ok, n_cand, n_ge)

    thr, n_ge = lax.fori_loop(0, 31, bit_body, (thr0, jnp.where(n_pos >= kf, n_pos, n_all)))
    thr_scr[...] = thr

    excess = jnp.max(jnp.where(n_ge > kf, 1.0, 0.0))

    @pl.when(excess > 0.0)
    def _():
        need = kf - count(lambda tile, k0: tile > thr)
        pos_i = lax.broadcasted_iota(jnp.int32, (2 * TQ, TQ), 0)

        def pos_body(i, bound):
            cand = bound | (jnp.int32(1) << (30 - i))
            below = count(lambda tile, k0: (tile == thr) & (k0 + pos_i < cand))
            return jnp.where(below <= need - 1.0, cand, bound)

        bound = lax.fori_loop(0, 31, pos_body, jnp.zeros((1, TQ), jnp.int32))

        def demote(kt, carry):
            k0 = pl.multiple_of(kt * TQ, TQ)
            tile = key_scr[pl.ds(k0, TQ), :]
            key_scr[pl.ds(k0, TQ), :] = jnp.where((tile == thr) & (k0 + key_i > bound), tile - 1, tile)
            return carry

        lax.fori_loop(0, qt + 1, demote, 0)

    for c in range(nh // 2):
        col_t = dq_ref[0, :, c * LANES:(c + 1) * LANES].astype(F32).T
        for hl in range(2):
            h = 2 * c + hl
            qm = jnp.where(_half_rows(hl, TQ), col_t, 0.0).astype(BF16)
            lhs_scr[:, h * TQ:(h + 1) * TQ] = (_dot(wuk_ref[c], qm) * (HEAD_DIM ** -0.5 * LOG2E)).astype(BF16)

    def masked(kt, near, b):
        keep = jnp.where(key_scr[pl.ds(pl.multiple_of(kt * TQ, TQ), TQ), :] >= thr_scr[...], 0.0, NEG_INF)
        return keep if near is None else keep + _near_bias(t_ref, b, near, 0)

    _flash_causal(
        qt, 1,
        lambda kt: _dot(_key_tile(ckv_ref, kt), lhs_scr[...]),
        masked,
        lambda kt: _key_tile(ckv_ref, kt),
        ((s0_scr, p0_scr, a0_scr), (s1_scr, p1_scr, a1_scr)), m_scr, l_scr, acc_scr)

    o_lat = (acc_scr[...] / l_scr[...]).astype(BF16)
    for c in range(nh // 2):
        out = (_dot(wuv_ref[2 * c], o_lat[:, (2 * c) * TQ:(2 * c + 1) * TQ])
               + _dot(wuv_ref[2 * c + 1], o_lat[:, (2 * c + 1) * TQ:(2 * c + 2) * TQ]))
        o_ref[0, :, c * LANES:(c + 1) * LANES] = out.T.astype(o_ref.dtype)


def _dsa_call(dq, ckv, iq, ik2, iw, wuk, wuv, t_tiles, topk):
    bsz, s, _ = dq.shape
    nh = DSA_HEADS
    hq = nh * TQ
    return pl.pallas_call(
        functools.partial(_dsa_kernel, topk),
        out_shape=jax.ShapeDtypeStruct((bsz, s, nh * HEAD_DIM), BF16),
        grid=(bsz, s // TQ),
        in_specs=[
            pl.BlockSpec((1, TQ, nh * HEAD_DIM), lambda b, i: (b, i, 0)),
            pl.BlockSpec((1, s, LANES), lambda b, i: (b, 0, 0)),
            pl.BlockSpec((1, TQ, IDX_HEADS * IDX_DIM), lambda b, i: (b, i, 0)),
            pl.BlockSpec((1, s, LANES), lambda b, i: (b, 0, 0)),
            pl.BlockSpec((1, TQ, LANES), lambda b, i: (b, i, 0)),
            _const_spec(wuk.shape),
            _const_spec(wuv.shape),
            pl.BlockSpec((2, TQ, hq), lambda b, i: (0, 0, 1)),
        ],
        out_specs=pl.BlockSpec((1, TQ, nh * HEAD_DIM), lambda b, i: (b, i, 0)),
        scratch_shapes=[
            pltpu.VMEM((s + TQ, TQ), jnp.int32),
            pltpu.VMEM((LANES, hq), BF16),
            pltpu.VMEM((1, TQ), jnp.int32),
        ] + _flash_scratch(hq),
        compiler_params=_params("parallel", "arbitrary"),
        name="dsa_attention",
    )(dq, ckv, iq, ik2, iw, wuk, wuv, t_tiles)


_NSA_Q = NSA_HEADS * HEAD_DIM
_NSA_KV = 2 * NSA_GROUPS * HEAD_DIM
_EVEN_SIZES = (_NSA_Q, _NSA_KV, _NSA_KV, _NSA_KV, NSA_HEADS * 3, DSA_HEADS * HEAD_DIM, DSA_KV_RANK,
               IDX_HEADS * IDX_DIM, IDX_DIM, IDX_HEADS)
_EVEN_OFF = np.concatenate([[0], np.cumsum(_EVEN_SIZES)])
_ZERO_COL = int(_EVEN_OFF[-1])


def _nsa_head_perm():
    pos = np.zeros(_NSA_Q, np.int64)
    for c in range(NSA_HPG):
        for g in range(NSA_GROUPS):
            for d in range(HEAD_DIM):
                pos[c * LANES + g * HEAD_DIM + d] = (g * NSA_HPG + c) * HEAD_DIM + d
    return pos


def _even_columns():
    o = _EVEN_OFF
    perm = _nsa_head_perm()
    gate = np.concatenate([o[4] + (perm // HEAD_DIM) * 3 + br for br in range(3)])
    idx_w = np.concatenate([np.arange(o[9], o[10]), np.full(LANES - IDX_HEADS, _ZERO_COL)])
    cols = [
        o[0] + perm,
        np.arange(o[1], o[2]), np.arange(o[2], o[3]), np.arange(o[3], o[4]),
        gate,
        np.arange(o[5], o[6]), np.arange(o[6], o[7]), np.arange(o[7], o[8]),
        np.concatenate([np.arange(o[8], o[9])] * 2),
        idx_w,
    ]
    segs = (
        (_NSA_Q, BF16, None), (_NSA_KV, F32, None), (_NSA_KV, BF16, None), (_NSA_KV, BF16, None),
        (3 * _NSA_Q, F32, "sigmoid"),
        (DSA_HEADS * HEAD_DIM, BF16, None), (DSA_KV_RANK, BF16, "rms"), (IDX_HEADS * IDX_DIM, BF16, None),
        (LANES, BF16, None), (LANES, F32, float(IDX_HEADS ** -0.5 * IDX_DIM ** -0.5)),
    )
    return np.concatenate(cols).astype(np.int32), segs


def _block_diag2(w):
    z = jnp.zeros_like(w)
    return jnp.concatenate([jnp.concatenate([w, z], axis=-1), jnp.concatenate([z, w], axis=-1)], axis=-2)


def _selection_constants(s):
    nc = s // CMP_STRIDE
    nb = s // SEL_BLK
    rows = _cmp_rows(s)
    cs = np.arange(nc) * CMP_STRIDE
    ss = np.arange(nb) * SEL_BLK
    overlap = np.clip(np.minimum(cs[:, None] + CMP_LEN, ss[None, :] + SEL_BLK) - np.maximum(cs[:, None], ss[None, :]), 0, None)
    imp = np.zeros((rows, LANES), np.float32)
    imp[CMP_PAD:CMP_PAD + nc, :nb] = overlap / CMP_STRIDE
    imp[CMP_PAD + nc - 1] = 0.0
    eneg = np.zeros((s, LANES), np.float32)
    eneg[np.arange(s), np.arange(s) // SEL_BLK] = NEG_INF
    return jnp.asarray(imp, BF16), jnp.asarray(eneg, BF16)


def kernel(x, c, rel_bias, ada_w, ada_b, ln_g, ln_b, ev_w_in, ev_w_out, nsa_pe_k, nsa_pe_v, nsa_w1_k, nsa_w2_k, nsa_w1_v, nsa_w2_v, dsa_kv_norm, dsa_w_uk, dsa_w_uv, od_w_in, od_w_out, diff_lam, diff_subln, mlp_w1, mlp_w2):
    bsz, s, d = x.shape
    depth = ada_w.shape[0]
    assert s % TQ == 0 and s // SEL_BLK <= LANES
    alpha = (2 * depth) ** 0.25
    topk = min(DSA_TOPK, s // 4)

    ada = _ada_call(c, ada_w, ada_b)
    t_tiles, tc_tiles = _bias_call(rel_bias)
    imp, eneg = _selection_constants(s)
    even_cols, even_segs = _even_columns()
    perm = _nsa_head_perm()
    odd_segs = ((od_w_in.shape[-1], BF16, None),)
    unit_gain = jnp.ones((1, LANES), F32)

    for l in range(depth):
        sh1, sc1, g1, sh2, sc2, g2 = [a.reshape(bsz, 1, d) for a in jnp.split(ada[l], 6, axis=-1)]
        i = l // 2
        if l % 2 == 0:
            w_aug = jnp.concatenate([ev_w_in[i], jnp.zeros((d, 1), F32)], axis=1)
            w_in = jnp.take(w_aug, even_cols, axis=1).astype(BF16)
            nsa_q, kvc, kvs, kvw, gates, dsa_q, ckv, idx_q, ik2, idx_w = _proj_call(
                x, sh1, sc1, w_in, dsa_kv_norm[i].reshape(1, LANES), even_segs)
            pe2 = jnp.stack([jnp.tile(nsa_pe_k[i], (1, 2)), jnp.tile(nsa_pe_v[i], (1, 2))])
            w1bd = jnp.stack([_block_diag2(w.reshape(CMP_LEN, HEAD_DIM, CMP_HIDDEN)) for w in (nsa_w1_k[i], nsa_w1_v[i])]).astype(BF16)
            w2bd = jnp.stack([_block_diag2(nsa_w2_k[i]), _block_diag2(nsa_w2_v[i])]).astype(BF16)
            kc, vc = _compress_call(kvc, pe2, w1bd, w2bd)
            o_a = _nsa_call(nsa_q, gates, kc, vc, kvs, kvw, imp, eneg, t_tiles, tc_tiles)
            wuk = dsa_w_uk[i].reshape(DSA_KV_RANK, DSA_HEADS // 2, LANES).transpose(1, 0, 2).astype(BF16)
            wuv_h = dsa_w_uv[i].transpose(1, 0, 2)
            zeros = jnp.zeros_like(wuv_h)
            even_h = (np.arange(DSA_HEADS) % 2 == 0)[:, None, None]
            wuv = jnp.where(even_h, jnp.concatenate([wuv_h, zeros], -1), jnp.concatenate([zeros, wuv_h], -1))
            wuv = wuv.transpose(0, 2, 1).astype(BF16)
            o_b = _dsa_call(dsa_q, ckv, idx_q, ik2, idx_w, wuk, wuv, t_tiles, topk)
            attn = jnp.concatenate([o_a, o_b], axis=-1)
            w_out = jnp.concatenate([ev_w_out[i][perm], ev_w_out[i][_NSA_Q:]], axis=0).astype(BF16)
        else:
            lambda_init = 0.8 - 0.6 * math.exp(-0.3 * l)
            (qkv,) = _proj_call(x, sh1, sc1, od_w_in[i].astype(BF16), unit_gain, odd_segs)
            attn = _diff_call(qkv, t_tiles, diff_lam[i], diff_subln[i].reshape(1, LANES), lambda_init)
            w_out = od_w_out[i].astype(BF16)
        x = _out_call(x, attn, w_out, g1, ln_g[l, 0].reshape(1, d), ln_b[l, 0].reshape(1, d), alpha)
        x = _mlp_call(x, sh2, sc2, g2, mlp_w1[l].astype(BF16), mlp_w2[l].astype(BF16),
                      ln_g[l, 1].reshape(1, d), ln_b[l, 1].reshape(1, d), alpha)
    return x
```

```python
import functools
import math

import jax
import jax.numpy as jnp
import numpy as np
from jax import lax
from jax.experimental import pallas as pl
from jax.experimental.pallas import tpu as pltpu

HEAD_DIM = 64
NSA_HEADS = 8
NSA_GROUPS = 2
NSA_HPG = NSA_HEADS // NSA_GROUPS
CMP_LEN = 32
CMP_STRIDE = 16
CMP_HIDDEN = 256
SEL_BLK = 64
N_SEL = 16
WINDOW = 512
DSA_HEADS = 8
DSA_KV_RANK = 128
IDX_HEADS = 8
IDX_DIM = 64
DSA_TOPK = 256
DIFF_HEADS = 8
DIFF_DIM = 64
NUM_BUCKETS = 32
MAX_DISTANCE = 128
NEG_INF = -1e30
BIG = 1e9
EPS = 1e-5

LANES = 128
TQ = 256
DIFF_NQ = 1
COUNT_ROWS = 64
MIN_NORMAL = float(np.finfo(np.float32).tiny)
CMP_PAD = 16
CMP_WIN = 32
VMEM_LIMIT = 56 * 1024 * 1024
LOG2E = math.log2(math.e)

F32 = jnp.float32
BF16 = jnp.bfloat16


def _bucket_table():
    n = np.arange(MAX_DISTANCE)
    exact = NUM_BUCKETS // 2
    nf = np.maximum(n, 1).astype(np.float64)
    large = exact + (np.log(nf / exact) / math.log(MAX_DISTANCE / exact) * (NUM_BUCKETS - exact)).astype(np.int64)
    return np.where(n < exact, n, np.minimum(large, NUM_BUCKETS - 1)).astype(np.int32)


def _dot(a, b):
    return jnp.dot(a, b, preferred_element_type=F32)


def _dot_tn(a, b):
    return lax.dot_general(a, b, (((0,), (0,)), ((), ())), preferred_element_type=F32)


def _params(*sem):
    return pltpu.CompilerParams(dimension_semantics=sem, vmem_limit_bytes=VMEM_LIMIT)


def _const_spec(shape):
    nd = len(shape)
    return pl.BlockSpec(shape, lambda *_: (0,) * nd)


def _ada_kernel(c_ref, w_ref, b_ref, o_ref):
    c = c_ref[...]
    ca = c * jax.nn.sigmoid(c)
    o_ref[0] = _dot(ca, w_ref[0]) + b_ref[0]


def _ada_call(c, ada_w, ada_b):
    depth, d, n = ada_w.shape
    bsz = c.shape[0]
    rows = 8
    cp = jnp.pad(c, ((0, rows - bsz), (0, 0)))
    tn = 1536
    out = pl.pallas_call(
        _ada_kernel,
        out_shape=jax.ShapeDtypeStruct((depth, rows, n), F32),
        grid=(depth, n // tn),
        in_specs=[
            pl.BlockSpec((rows, d), lambda l, j: (0, 0)),
            pl.BlockSpec((1, d, tn), lambda l, j: (l, 0, j)),
            pl.BlockSpec((1, 1, tn), lambda l, j: (l, 0, j)),
        ],
        out_specs=pl.BlockSpec((1, rows, tn), lambda l, j: (l, 0, j)),
        compiler_params=_params("arbitrary", "arbitrary"),
        name="ada_table",
    )(cp, ada_w, ada_b.reshape(depth, 1, n))
    return out[:, :bsz]


def _proj_kernel(segs, x_ref, sh_ref, sc_ref, w_ref, gain_ref, *o_refs):
    h = (x_ref[0] * (1.0 + sc_ref[0]) + sh_ref[0]).astype(BF16)
    off = 0
    for (width, _, epi), o_ref in zip(segs, o_refs):
        y = _dot(h, w_ref[:, off:off + width])
        if epi == "sigmoid":
            y = jax.nn.sigmoid(y)
        elif epi == "rms":
            y = y * lax.rsqrt(jnp.mean(jnp.square(y), axis=-1, keepdims=True) + EPS) * gain_ref[...]
        elif isinstance(epi, float):
            y = y * epi
        o_ref[0] = y.astype(o_ref.dtype)
        off += width


def _proj_call(x, sh, sc, w, gain, segs, tm=256):
    bsz, s, d = x.shape
    ntot = w.shape[1]
    return pl.pallas_call(
        functools.partial(_proj_kernel, segs),
        out_shape=[jax.ShapeDtypeStruct((bsz, s, width), dt) for width, dt, _ in segs],
        grid=(bsz, s // tm),
        in_specs=[
            pl.BlockSpec((1, tm, d), lambda b, i: (b, i, 0)),
            pl.BlockSpec((1, 1, d), lambda b, i: (b, 0, 0)),
            pl.BlockSpec((1, 1, d), lambda b, i: (b, 0, 0)),
            _const_spec((d, ntot)),
            _const_spec((1, LANES)),
        ],
        out_specs=[pl.BlockSpec((1, tm, width), lambda b, i: (b, i, 0)) for width, _, _ in segs],
        compiler_params=_params("parallel", "parallel"),
        name="modulate_in_proj",
    )(x, sh, sc, w, gain)


def _layer_norm(z, g, b):
    mu = jnp.mean(z, axis=-1, keepdims=True)
    zc = z - mu
    var = jnp.mean(jnp.square(zc), axis=-1, keepdims=True)
    return zc * lax.rsqrt(var + EPS) * g + b


def _out_kernel(alpha, x_ref, a_ref, w_ref, g1_ref, lg_ref, lb_ref, o_ref):
    y = _dot(a_ref[0], w_ref[...])
    z = alpha * x_ref[0] + (1.0 + g1_ref[0]) * y
    o_ref[0] = _layer_norm(z, lg_ref[...], lb_ref[...])


def _out_call(x, attn, w_out, g1, ln_g, ln_b, alpha, tm=512):
    bsz, s, d = x.shape
    k = attn.shape[-1]
    return pl.pallas_call(
        functools.partial(_out_kernel, alpha),
        out_shape=jax.ShapeDtypeStruct((bsz, s, d), F32),
        grid=(bsz, s // tm),
        in_specs=[
            pl.BlockSpec((1, tm, d), lambda b, i: (b, i, 0)),
            pl.BlockSpec((1, tm, k), lambda b, i: (b, i, 0)),
            _const_spec((k, d)),
            pl.BlockSpec((1, 1, d), lambda b, i: (b, 0, 0)),
            _const_spec((1, d)),
            _const_spec((1, d)),
        ],
        out_specs=pl.BlockSpec((1, tm, d), lambda b, i: (b, i, 0)),
        compiler_params=_params("parallel", "parallel"),
        name="out_proj_ln",
    )(x, attn, w_out, g1, ln_g, ln_b)


def _mlp_kernel(alpha, x_ref, sh_ref, sc_ref, g2_ref, w1_ref, w2_ref, lg_ref, lb_ref, o_ref, h_scr, acc_scr):
    f = pl.program_id(2)

    @pl.when(f == 0)
    def _():
        h_scr[...] = (x_ref[0] * (1.0 + sc_ref[0]) + sh_ref[0]).astype(BF16)
        acc_scr[...] = jnp.zeros_like(acc_scr)

    a = jnp.maximum(_dot(h_scr[...], w1_ref[...]), 0.0)
    acc_scr[...] += _dot((a * a).astype(BF16), w2_ref[...])

    @pl.when(f == pl.num_programs(2) - 1)
    def _():
        z = alpha * x_ref[0] + (1.0 + g2_ref[0]) * acc_scr[...]
        o_ref[0] = _layer_norm(z, lg_ref[...], lb_ref[...])


def _mlp_call(x, sh, sc, g2, w1, w2, ln_g, ln_b, alpha, tm=1024, tf=512):
    bsz, s, d = x.shape
    ff = w1.shape[1]
    tm = min(tm, s)
    assert s % tm == 0
    return pl.pallas_call(
        functools.partial(_mlp_kernel, alpha),
        out_shape=jax.ShapeDtypeStruct((bsz, s, d), F32),
        grid=(bsz, s // tm, ff // tf),
        in_specs=[
            pl.BlockSpec((1, tm, d), lambda b, i, f: (b, i, 0)),
            pl.BlockSpec((1, 1, d), lambda b, i, f: (b, 0, 0)),
            pl.BlockSpec((1, 1, d), lambda b, i, f: (b, 0, 0)),
            pl.BlockSpec((1, 1, d), lambda b, i, f: (b, 0, 0)),
            pl.BlockSpec((d, tf), lambda b, i, f: (0, f)),
            pl.BlockSpec((tf, d), lambda b, i, f: (f, 0)),
            _const_spec((1, d)),
            _const_spec((1, d)),
        ],
        out_specs=pl.BlockSpec((1, tm, d), lambda b, i, f: (b, i, 0)),
        scratch_shapes=[pltpu.VMEM((tm, d), BF16), pltpu.VMEM((tm, d), F32)],
        compiler_params=_params("parallel", "parallel", "arbitrary"),
        name="mlp_ln",
    )(x, sh, sc, g2, w1, w2, ln_g, ln_b)


def _bias_kernel(tab_ref, t_ref, tc_ref):
    h = pl.program_id(0)
    key = lax.broadcasted_iota(jnp.int32, (TQ, TQ), 0)
    qry = lax.broadcasted_iota(jnp.int32, (TQ, TQ), 1)
    win = lax.broadcasted_iota(jnp.int32, (LANES, TQ), 0)
    qry_c = lax.broadcasted_iota(jnp.int32, (LANES, TQ), 1)
    dists = (qry - key, TQ + qry - key, qry_c - CMP_STRIDE * (win - CMP_PAD) - (CMP_LEN - 1))

    def body(d, accs):
        val = tab_ref[h * MAX_DISTANCE + d] * LOG2E
        return tuple(jnp.where(dist == d, val, acc) for dist, acc in zip(dists, accs))

    accs = lax.fori_loop(0, MAX_DISTANCE, body, tuple(jnp.zeros(dist.shape, F32) for dist in dists))
    t_ref[0] = jnp.where(dists[0] < 0, NEG_INF, accs[0])
    t_ref[1] = accs[1]
    tc_ref[...] = jnp.where((dists[2] < 0) | (win >= CMP_WIN), NEG_INF, accs[2])


def _bias_call(rel_bias):
    nh = rel_bias.shape[1]
    shifted = rel_bias[_bucket_table()] - rel_bias[NUM_BUCKETS - 1]
    tab = shifted.T.reshape(-1)
    return pl.pallas_call(
        _bias_kernel,
        out_shape=[
            jax.ShapeDtypeStruct((2, TQ, nh * TQ), F32),
            jax.ShapeDtypeStruct((LANES, nh * TQ), F32),
        ],
        grid=(nh,),
        in_specs=[pl.BlockSpec(memory_space=pltpu.SMEM)],
        out_specs=[
            pl.BlockSpec((2, TQ, TQ), lambda h: (0, 0, h)),
            pl.BlockSpec((LANES, TQ), lambda h: (0, h)),
        ],
        compiler_params=_params("arbitrary"),
        name="t5_bias_tiles",
    )(tab)


def _flash_init(m_ref, l_ref, acc_ref):
    m_ref[...] = jnp.full(m_ref.shape, NEG_INF, F32)
    l_ref[...] = jnp.zeros(l_ref.shape, F32)
    acc_ref[...] = jnp.zeros(acc_ref.shape, F32)


def _flash_update(s, v_tile, m_ref, l_ref, acc_ref):
    m_prev = m_ref[...]
    m_new = jnp.maximum(m_prev, jnp.max(s, axis=0, keepdims=True))
    alpha = jnp.exp2(m_prev - m_new)
    p = jnp.exp2(s - m_new)
    l_ref[...] = alpha * l_ref[...] + jnp.sum(p, axis=0, keepdims=True)
    acc_ref[...] = alpha * acc_ref[...] + _dot_tn(v_tile, p.astype(BF16))
    m_ref[...] = m_new


def _near_bias(t_ref, head, near, u):
    d = u - (near - 1)
    if d < 0:
        return NEG_INF
    if d >= 2:
        return None
    return t_ref[d, :, head * TQ:(head + 1) * TQ]


def _flash_causal(qt, nq, qk_fn, extra_fn, v_fn, bufs, m_ref, l_ref, acc_ref):
    even_set, odd_set = bufs
    sets = (even_set, odd_set)
    last = qt * nq + nq - 1
    n_far = jnp.maximum(qt * nq - 1, 0)
    _flash_init(m_ref, l_ref, acc_ref)
    for _, p_buf, a_buf in bufs:
        p_buf[...] = jnp.zeros(p_buf.shape, p_buf.dtype)
        a_buf[...] = jnp.ones(a_buf.shape, F32)

    def logits(kt, dst):
        dst[0][...] = qk_fn(kt)

    def softmax(kt, cur, kind):
        width = cur[0].shape[1]
        blocks = [slice(b * TQ, (b + 1) * TQ) for b in range(width // TQ)]
        adds = [extra_fn(kt, kind, b) for b in range(len(blocks))]
        tops = []
        for blk, add in zip(blocks, adds):
            s = cur[0][:, blk]
            tops.append(jnp.max(s if add is None else s + add, axis=0, keepdims=True))
        m_prev = m_ref[...]
        m_new = jnp.maximum(m_prev, jnp.concatenate(tops, axis=1))
        alpha = jnp.exp2(m_prev - m_new)
        sums = []
        for blk, add in zip(blocks, adds):
            z = cur[0][:, blk] - m_new[:, blk]
            p = jnp.exp2(z if add is None else z + add)
            sums.append(jnp.sum(p, axis=0, keepdims=True))
            cur[1][:, blk] = p.astype(BF16)
        l_ref[...] = alpha * l_ref[...] + jnp.concatenate(sums, axis=1)
        m_ref[...] = m_new
        cur[2][...] = alpha

    def values(kt, src):
        acc_ref[...] = src[2][...] * acc_ref[...] + _dot_tn(v_fn(jnp.maximum(kt, 0)), src[1][...])

    def trip(kt, cur, nxt):
        logits(kt + 1, nxt)
        softmax(kt, cur, None)
        values(kt - 1, nxt)

    second = sets[(nq + 1) & 1]
    first = sets[nq & 1]
    first_is_even = (last & 1) == 0

    @pl.when(first_is_even)
    def _():
        logits(0, even_set)

    @pl.when(jnp.logical_not(first_is_even))
    def _():
        logits(0, odd_set)

    start = n_far & 1

    @pl.when(start == 1)
    def _():
        trip(0, second, first)

    def pair(j, carry):
        kt = start + 2 * j
        trip(kt, first, second)
        trip(kt + 1, second, first)
        return carry

    lax.fori_loop(0, (n_far - start) // 2, pair, 0)

    def near_tile(near):
        cur = sets[(nq - near) & 1]
        kt = last - (nq - near)
        if near < nq:
            logits(kt + 1, sets[(nq - near - 1) & 1])
        softmax(kt, cur, near)
        values(kt - 1, sets[(nq - near + 1) & 1])

    pl.when(qt >= 1)(functools.partial(near_tile, 0))
    for near in range(1, nq + 1):
        near_tile(near)
    values(last, even_set)


def _flash_scratch(width):
    stats = [pltpu.VMEM((1, width), F32), pltpu.VMEM((1, width), F32), pltpu.VMEM((LANES, width), F32)]
    one_set = [pltpu.VMEM((TQ, width), F32), pltpu.VMEM((TQ, width), BF16), pltpu.VMEM((1, width), F32)]
    return stats + 2 * one_set


def _half_rows(half, width):
    row = lax.broadcasted_iota(jnp.int32, (LANES, width), 0)
    return (row >= half * HEAD_DIM) & (row < (half + 1) * HEAD_DIM)


def _key_tile(ref, kt):
    return ref[0, pl.ds(pl.multiple_of(kt * TQ, TQ), TQ), :]


def _diff_kernel(out_scale, q_ref, k_ref, v_ref, t_ref, lam_ref, sub_ref, o_ref,
                 qt_scr, m_scr, l_scr, acc_scr, s0_scr, p0_scr, a0_scr, s1_scr, p1_scr, a1_scr):
    qt = pl.program_id(2)
    qw = DIFF_NQ * TQ
    q_t = (q_ref[0].astype(F32) * (DIFF_DIM ** -0.5 * LOG2E)).T
    for m in range(2):
        qt_scr[:, m * qw:(m + 1) * qw] = jnp.where(_half_rows(m, qw), q_t, 0.0).astype(BF16)

    _flash_causal(
        qt, DIFF_NQ,
        lambda kt: _dot(_key_tile(k_ref, kt), qt_scr[...]),
        lambda kt, near, b: None if near is None else _near_bias(t_ref, b // DIFF_NQ, near, b % DIFF_NQ),
        lambda kt: _key_tile(v_ref, kt),
        ((s0_scr, p0_scr, a0_scr), (s1_scr, p1_scr, a1_scr)), m_scr, l_scr, acc_scr)

    o = acc_scr[...] / l_scr[...]
    lam = lam_ref[...]
    lam_full = (jnp.exp(jnp.sum(lam[0:1] * lam[1:2], axis=1, keepdims=True))
                - jnp.exp(jnp.sum(lam[2:3] * lam[3:4], axis=1, keepdims=True)) + (1.0 - out_scale))
    od = (o[:, :qw] - lam_full * o[:, qw:]).T
    od = od * lax.rsqrt(jnp.mean(jnp.square(od), axis=-1, keepdims=True) + EPS) * sub_ref[...]
    o_ref[0] = (od * out_scale).astype(o_ref.dtype)


def _diff_call(qkv, t_tiles, lam, subln, lambda_init):
    bsz, s, _ = qkv.shape
    nh = DIFF_HEADS
    qw = DIFF_NQ * TQ
    assert s % qw == 0
    return pl.pallas_call(
        functools.partial(_diff_kernel, 1.0 - lambda_init),
        out_shape=jax.ShapeDtypeStruct((bsz, s, nh * LANES), BF16),
        grid=(bsz, nh, s // qw),
        in_specs=[
            pl.BlockSpec((1, qw, LANES), lambda b, h, i: (b, i, h)),
            pl.BlockSpec((1, s, LANES), lambda b, h, i: (b, 0, nh + h)),
            pl.BlockSpec((1, s, LANES), lambda b, h, i: (b, 0, 2 * nh + h)),
            pl.BlockSpec((2, TQ, 2 * TQ), lambda b, h, i: (0, 0, h)),
            _const_spec((4, DIFF_DIM)),
            _const_spec((1, LANES)),
        ],
        out_specs=pl.BlockSpec((1, qw, LANES), lambda b, h, i: (b, i, h)),
        scratch_shapes=[pltpu.VMEM((LANES, 2 * qw), BF16)] + _flash_scratch(2 * qw),
        compiler_params=_params("parallel", "parallel", "arbitrary"),
        name="diff_attention",
    )(qkv, qkv, qkv, t_tiles, lam, subln)


def _compress_kernel(nc, xk_ref, xv_ref, pe_ref, w1_ref, w2_ref, kc_ref, vc_ref):
    half = CMP_LEN // 2
    for kv, (x_ref, o_ref) in enumerate(((xk_ref, kc_ref), (xv_ref, vc_ref))):
        first = jnp.zeros((nc, NSA_GROUPS * CMP_HIDDEN), F32)
        second = jnp.zeros_like(first)
        for l in range(half):
            x = x_ref[0, pl.ds(l, nc, stride=CMP_STRIDE), :]
            first += _dot((x + pe_ref[kv, l:l + 1]).astype(BF16), w1_ref[kv, l])
            second += _dot((x + pe_ref[kv, half + l:half + l + 1]).astype(BF16), w1_ref[kv, half + l])
        pre = first + pltpu.roll(second, nc - 1, 0)
        hid = jax.nn.gelu(pre, approximate=True)
        o_ref[0] = jnp.zeros(o_ref.shape[1:], o_ref.dtype)
        o_ref[0, CMP_PAD:CMP_PAD + nc, :] = _dot(hid.astype(BF16), w2_ref[kv]).astype(o_ref.dtype)


def _cmp_rows(s):
    nc = s // CMP_STRIDE
    return -(-(nc + LANES) // LANES) * LANES


def _compress_call(kvc, pe2, w1bd, w2bd):
    bsz, s, _ = kvc.shape
    nc = s // CMP_STRIDE
    rows = _cmp_rows(s)
    return pl.pallas_call(
        functools.partial(_compress_kernel, nc),
        out_shape=[jax.ShapeDtypeStruct((bsz, rows, LANES), BF16)] * 2,
        grid=(bsz,),
        in_specs=[
            pl.BlockSpec((1, s, LANES), lambda b: (b, 0, 0)),
            pl.BlockSpec((1, s, LANES), lambda b: (b, 0, 1)),
            _const_spec(pe2.shape),
            _const_spec(w1bd.shape),
            _const_spec(w2bd.shape),
        ],
        out_specs=[pl.BlockSpec((1, rows, LANES), lambda b: (b, 0, 0))] * 2,
        compiler_params=_params("parallel"),
        name="nsa_compress",
    )(kvc, kvc, pe2, w1bd, w2bd)


def _nsa_kernel(q_ref, gate_ref, kc_ref, vc_ref, ks_ref, vs_ref, kw_ref, vw_ref, imp_ref, eneg_ref, t_ref, tc_ref,
                o_ref, qpad_scr, lhs_scr, res_scr, m_scr, l_scr, acc_scr, s0_scr, p0_scr, a0_scr, s1_scr, p1_scr, a1_scr):
    qt = pl.program_id(1)
    q0 = qt * TQ
    n0 = pl.multiple_of(qt * (TQ // CMP_STRIDE), TQ // CMP_STRIDE)
    rows = kc_ref.shape[1]
    hq = NSA_HPG * TQ

    for c in range(NSA_HPG):
        q_t = (q_ref[0, :, c * LANES:(c + 1) * LANES].astype(F32) * (HEAD_DIM ** -0.5 * LOG2E)).T
        for g in range(NSA_GROUPS):
            qpad_scr[g, :, c * TQ:(c + 1) * TQ] = jnp.where(_half_rows(g, TQ), q_t, 0.0).astype(BF16)

    blk = lax.broadcasted_iota(jnp.int32, (LANES, TQ), 0)
    blkf = blk.astype(F32)
    cur = (q0 + lax.broadcasted_iota(jnp.int32, (LANES, TQ), 1)) // SEL_BLK
    allowed = blk <= cur
    forced = (blk == 0) | (blk == cur) | (blk == cur - 1)
    rowp = lax.broadcasted_iota(jnp.int32, (rows, TQ), 0)
    far_bias = jnp.where((rowp >= CMP_PAD) & (rowp < n0), 0.0, NEG_INF)
    near_bias = jnp.where((n0 + lax.broadcasted_iota(jnp.int32, (LANES, TQ), 0)) >= CMP_PAD, 0.0, NEG_INF)
    key_i = lax.broadcasted_iota(jnp.int32, (TQ, TQ), 0)
    qry_i = lax.broadcasted_iota(jnp.int32, (TQ, TQ), 1)
    win_edge = jnp.where(key_i > qry_i, 0.0, NEG_INF)

    for g in range(NSA_GROUPS):
        qp = qpad_scr[g]
        heads = slice(g * hq, (g + 1) * hq)
        kc_near = kc_ref[0, pl.ds(n0, LANES), :]
        vc_near = vc_ref[0, pl.ds(n0, LANES), :]
        s_far = _dot(kc_ref[0], qp)
        s_near = _dot(kc_near, qp) + tc_ref[:, heads]
        psum_far = jnp.zeros((rows, TQ), F32)
        psum_near = jnp.zeros((LANES, TQ), F32)
        p_far, p_near = [], []
        for c in range(NSA_HPG):
            sf = s_far[:, c * TQ:(c + 1) * TQ] + far_bias
            sn = s_near[:, c * TQ:(c + 1) * TQ] + near_bias
            mx = jnp.maximum(jnp.maximum(jnp.max(sf, axis=0, keepdims=True), jnp.max(sn, axis=0, keepdims=True)),
                             0.1 * NEG_INF)
            pf = jnp.exp2(sf - mx)
            pn = jnp.exp2(sn - mx)
            den = jnp.sum(pf, axis=0, keepdims=True) + jnp.sum(pn, axis=0, keepdims=True)
            inv = 1.0 / jnp.maximum(den, 1e-30)
            pf = pf * inv
            pn = pn * inv
            psum_far += pf
            psum_near += pn
            p_far.append(pf.astype(BF16))
            p_near.append(pn.astype(BF16))
        res_scr[0, g] = (_dot_tn(vc_ref[0], jnp.concatenate(p_far, axis=1))
                         + _dot_tn(vc_near, jnp.concatenate(p_near, axis=1)))
        score = jnp.zeros((LANES, TQ), F32)
        for psum, imp in ((psum_far, imp_ref[...]), (psum_near, imp_ref[pl.ds(n0, LANES), :])):
            hi = psum.astype(BF16)
            lo = (psum - hi.astype(F32)).astype(BF16)
            score += _dot_tn(imp, hi) + _dot_tn(imp, lo)
        score = jnp.where(forced, BIG, jnp.where(allowed, score, -BIG))
        unpicked = jnp.ones((LANES, TQ), F32)
        for _ in range(N_SEL):
            best = jnp.max(score, axis=0, keepdims=True)
            first = jnp.min(jnp.where(score == best, blkf, float(LANES)), axis=0, keepdims=True)
            pick = blkf == first
            unpicked = jnp.where(pick, 0.0, unpicked)
            score = jnp.where(pick, -jnp.inf, score)
        notsel = jnp.maximum(jnp.where(allowed, 0.0, 1.0), unpicked).astype(BF16)
        for c in range(NSA_HPG):
            lhs_scr[:LANES, c * TQ:(c + 1) * TQ] = notsel
        lhs_scr[LANES:, :] = qp

        def sel_logits(kt):
            k0 = pl.multiple_of(kt * TQ, TQ)
            return _dot(jnp.concatenate([eneg_ref[pl.ds(k0, TQ), :], _key_tile(ks_ref, kt)], axis=1), lhs_scr[...])

        _flash_causal(
            qt, 1, sel_logits,
            lambda kt, near, b: None if near is None else _near_bias(t_ref, g * NSA_HPG + b, near, 0),
            lambda kt: _key_tile(vs_ref, kt),
            ((s0_scr, p0_scr, a0_scr), (s1_scr, p1_scr, a1_scr)), m_scr, l_scr, acc_scr)
        res_scr[1, g] = acc_scr[...] / l_scr[...]

        @pl.when(qt >= 2)
        def _():
            k0 = pl.multiple_of((qt - 2) * TQ, TQ)
            s = _dot(kw_ref[0, pl.ds(k0, 3 * TQ), :], qp)
            probs, sums = [], []
            for c in range(NSA_HPG):
                blk = slice(c * TQ, (c + 1) * TQ)
                hblk = slice((g * NSA_HPG + c) * TQ, (g * NSA_HPG + c + 1) * TQ)
                sb = s[:, blk] + jnp.concatenate([win_edge, t_ref[1, :, hblk], t_ref[0, :, hblk]], axis=0)
                p = jnp.exp2(sb - jnp.max(sb, axis=0, keepdims=True))
                sums.append(jnp.sum(p, axis=0, keepdims=True))
                probs.append(p.astype(BF16))
            out = _dot_tn(vw_ref[0, pl.ds(k0, 3 * TQ), :], jnp.concatenate(probs, axis=1))
            res_scr[2, g] = out / jnp.concatenate(sums, axis=1)

        @pl.when(qt < 2)
        def _():
            _flash_init(m_scr, l_scr, acc_scr)

            def win_step(kt, near):
                s = _dot(_key_tile(kw_ref, kt), qp) + t_ref[near, :, heads]
                _flash_update(s, _key_tile(vw_ref, kt), m_scr, l_scr, acc_scr)

            pl.when(qt >= 1)(functools.partial(win_step, qt - 1, 1))
            win_step(qt, 0)
            res_scr[2, g] = acc_scr[...] / l_scr[...]

    for c in range(NSA_HPG):
        out = jnp.zeros((TQ, LANES), F32)
        for br in range(3):
            gate = gate_ref[0, :, br * NSA_HEADS * HEAD_DIM + c * LANES:br * NSA_HEADS * HEAD_DIM + (c + 1) * LANES]
            both = jnp.concatenate([res_scr[br, 0, :HEAD_DIM, c * TQ:(c + 1) * TQ],
                                    res_scr[br, 1, HEAD_DIM:, c * TQ:(c + 1) * TQ]], axis=0)
            out += gate * both.T
        o_ref[0, :, c * LANES:(c + 1) * LANES] = out.astype(o_ref.dtype)


def _nsa_call(q, gates, kc, vc, kvs, kvw, imp, eneg, t_tiles, tc_tiles):
    bsz, s, _ = q.shape
    rows = kc.shape[1]
    hq = NSA_HPG * TQ
    whole = lambda col: pl.BlockSpec((1, s, LANES), lambda b, i: (b, 0, col))
    return pl.pallas_call(
        _nsa_kernel,
        out_shape=jax.ShapeDtypeStruct((bsz, s, NSA_HEADS * HEAD_DIM), BF16),
        grid=(bsz, s // TQ),
        in_specs=[
            pl.BlockSpec((1, TQ, NSA_HEADS * HEAD_DIM), lambda b, i: (b, i, 0)),
            pl.BlockSpec((1, TQ, 3 * NSA_HEADS * HEAD_DIM), lambda b, i: (b, i, 0)),
            pl.BlockSpec((1, rows, LANES), lambda b, i: (b, 0, 0)),
            pl.BlockSpec((1, rows, LANES), lambda b, i: (b, 0, 0)),
            whole(0), whole(1), whole(0), whole(1),
            _const_spec(imp.shape),
            _const_spec(eneg.shape),
            pl.BlockSpec((2, TQ, NSA_HEADS * TQ), lambda b, i: (0, 0, 0)),
            pl.BlockSpec((LANES, NSA_HEADS * TQ), lambda b, i: (0, 0)),
        ],
        out_specs=pl.BlockSpec((1, TQ, NSA_HEADS * HEAD_DIM), lambda b, i: (b, i, 0)),
        scratch_shapes=[
            pltpu.VMEM((NSA_GROUPS, LANES, hq), BF16),
            pltpu.VMEM((2 * LANES, hq), BF16),
            pltpu.VMEM((3, NSA_GROUPS, LANES, hq), F32),
        ] + _flash_scratch(hq),
        compiler_params=_params("parallel", "arbitrary"),
        name="nsa_attention",
    )(q, gates, kc, vc, kvs, kvs, kvw, kvw, imp, eneg, t_tiles, tc_tiles)


def _sort_key(x):
    b = lax.bitcast_convert_type(x, jnp.int32)
    return b ^ ((b >> 31) & jnp.int32(0x7FFFFFFF))


def _dsa_kernel(topk, dq_ref, ckv_ref, iq_ref, ik_ref, iw_ref, wuk_ref, wuv_ref, t_ref, o_ref,
                key_scr, lhs_scr, thr_scr, m_scr, l_scr, acc_scr, s0_scr, p0_scr, a0_scr, s1_scr, p1_scr, a1_scr):
    qt = pl.program_id(1)
    nh = DSA_HEADS
    key_i = lax.broadcasted_iota(jnp.int32, (TQ, TQ), 0)
    qry_i = lax.broadcasted_iota(jnp.int32, (TQ, TQ), 1)
    int_min = jnp.int32(-2 ** 31)

    for c in range(IDX_HEADS // 2):
        col_t = iq_ref[0, :, c * LANES:(c + 1) * LANES].astype(F32).T.astype(BF16)
        for hl in range(2):
            j = 2 * c + hl
            lhs_scr[:IDX_DIM, j * TQ:(j + 1) * TQ] = col_t[hl * IDX_DIM:(hl + 1) * IDX_DIM]
    iw_t = iw_ref[0].T

    def idx_tile(kt, causal):
        s = _dot(_key_tile(ik_ref, kt)[:, :IDX_DIM], lhs_scr[:IDX_DIM, :])
        isc = jnp.zeros((TQ, TQ), F32)
        for j in range(IDX_HEADS):
            isc += iw_t[j:j + 1, :] * jnp.maximum(s[:, j * TQ:(j + 1) * TQ], 0.0)
        if causal:
            isc = jnp.where(key_i <= qry_i, isc, NEG_INF)
        k0 = pl.multiple_of(kt * TQ, TQ)
        zero_key = -2 - (k0 + key_i)
        key_scr[pl.ds(k0, TQ), :] = jnp.where(jnp.abs(isc) < MIN_NORMAL, zero_key, _sort_key(isc))

    def idx_far(kt, carry):
        idx_tile(kt, False)
        return carry

    lax.fori_loop(0, qt, idx_far, 0)
    idx_tile(qt, True)

    @pl.when((qt & 1) == 0)
    def _():
        key_scr[pl.ds(pl.multiple_of((qt + 1) * TQ, TQ), TQ), :] = jnp.full((TQ, TQ), int_min, jnp.int32)

    def count(pred_fn):
        def body(kp, cnt):
            k0 = pl.multiple_of(kp * (2 * TQ), 2 * TQ)
            tile = key_scr[pl.ds(k0, 2 * TQ), :]
            hit = jnp.where(pred_fn(tile, k0), 1.0, 0.0)
            return cnt + jnp.sum(hit.reshape(2 * TQ // COUNT_ROWS, COUNT_ROWS, TQ), axis=0)

        cnt = lax.fori_loop(0, (qt + 2) // 2, body, jnp.zeros((COUNT_ROWS, TQ), F32))
        return jnp.sum(cnt, axis=0, keepdims=True)

    kf = float(topk)
    n_pos = count(lambda tile, k0: tile >= 0)
    thr0 = jnp.where(n_pos >= kf, jnp.int32(0), int_min)
    n_all = (qt + 1).astype(F32) * TQ

    def bit_body(i, carry):
        thr, n_ge = carry
        cand = thr | (jnp.int32(1) << (30 - i))
        n_cand = count(lambda tile, k0: tile >= cand)
        ok = n_cand >= kf
        return jnp.where(ok, cand, thr), jnp.where(ok, n_cand, n_ge)

    thr, n_ge = lax.fori_loop(0, 31, bit_body, (thr0, jnp.where(n_pos >= kf, n_pos, n_all)))
    thr_scr[...] = thr

    excess = jnp.max(jnp.where(n_ge > kf, 1.0, 0.0))

    @pl.when(excess > 0.0)
    def _():
        need = kf - count(lambda tile, k0: tile > thr)
        pos_i = lax.broadcasted_iota(jnp.int32, (2 * TQ, TQ), 0)

        def pos_body(i, bound):
            cand = bound | (jnp.int32(1) << (30 - i))
            below = count(lambda tile, k0: (tile == thr) & (k0 + pos_i < cand))
            return jnp.where(below <= need - 1.0, cand, bound)

        bound = lax.fori_loop(0, 31, pos_body, jnp.zeros((1, TQ), jnp.int32))

        def demote(kt, carry):
            k0 = pl.multiple_of(kt * TQ, TQ)
            tile = key_scr[pl.ds(k0, TQ), :]
            key_scr[pl.ds(k0, TQ), :] = jnp.where((tile == thr) & (k0 + key_i > bound), tile - 1, tile)
            return carry

        lax.fori_loop(0, qt + 1, demote, 0)

    for c in range(nh // 2):
        col_t = dq_ref[0, :, c * LANES:(c + 1) * LANES].astype(F32).T
        for hl in range(2):
            h = 2 * c + hl
            qm = jnp.where(_half_rows(hl, TQ), col_t, 0.0).astype(BF16)
            lhs_scr[:, h * TQ:(h + 1) * TQ] = (_dot(wuk_ref[c], qm) * (HEAD_DIM ** -0.5 * LOG2E)).astype(BF16)

    def masked(kt, near, b):
        keep = jnp.where(key_scr[pl.ds(pl.multiple_of(kt * TQ, TQ), TQ), :] >= thr_scr[...], 0.0, NEG_INF)
        return keep if near is None else keep + _near_bias(t_ref, b, near, 0)

    _flash_causal(
        qt, 1,
        lambda kt: _dot(_key_tile(ckv_ref, kt), lhs_scr[...]),
        masked,
        lambda kt: _key_tile(ckv_ref, kt),
        ((s0_scr, p0_scr, a0_scr), (s1_scr, p1_scr, a1_scr)), m_scr, l_scr, acc_scr)

    o_lat = (acc_scr[...] / l_scr[...]).astype(BF16)
    for c in range(nh // 2):
        out = (_dot(wuv_ref[2 * c], o_lat[:, (2 * c) * TQ:(2 * c + 1) * TQ])
               + _dot(wuv_ref[2 * c + 1], o_lat[:, (2 * c + 1) * TQ:(2 * c + 2) * TQ]))
        o_ref[0, :, c * LANES:(c + 1) * LANES] = out.T.astype(o_ref.dtype)


def _dsa_call(dq, ckv, iq, ik2, iw, wuk, wuv, t_tiles, topk):
    bsz, s, _ = dq.shape
    nh = DSA_HEADS
    hq = nh * TQ
    return pl.pallas_call(
        functools.partial(_dsa_kernel, topk),
        out_shape=jax.ShapeDtypeStruct((bsz, s, nh * HEAD_DIM), BF16),
        grid=(bsz, s // TQ),
        in_specs=[
            pl.BlockSpec((1, TQ, nh * HEAD_DIM), lambda b, i: (b, i, 0)),
            pl.BlockSpec((1, s, LANES), lambda b, i: (b, 0, 0)),
            pl.BlockSpec((1, TQ, IDX_HEADS * IDX_DIM), lambda b, i: (b, i, 0)),
            pl.BlockSpec((1, s, LANES), lambda b, i: (b, 0, 0)),
            pl.BlockSpec((1, TQ, LANES), lambda b, i: (b, i, 0)),
            _const_spec(wuk.shape),
            _const_spec(wuv.shape),
            pl.BlockSpec((2, TQ, hq), lambda b, i: (0, 0, 1)),
        ],
        out_specs=pl.BlockSpec((1, TQ, nh * HEAD_DIM), lambda b, i: (b, i, 0)),
        scratch_shapes=[
            pltpu.VMEM((s + TQ, TQ), jnp.int32),
            pltpu.VMEM((LANES, hq), BF16),
            pltpu.VMEM((1, TQ), jnp.int32),
        ] + _flash_scratch(hq),
        compiler_params=_params("parallel", "arbitrary"),
        name="dsa_attention",
    )(dq, ckv, iq, ik2, iw, wuk, wuv, t_tiles)


_NSA_Q = NSA_HEADS * HEAD_DIM
_NSA_KV = 2 * NSA_GROUPS * HEAD_DIM
_EVEN_SIZES = (_NSA_Q, _NSA_KV, _NSA_KV, _NSA_KV, NSA_HEADS * 3, DSA_HEADS * HEAD_DIM, DSA_KV_RANK,
               IDX_HEADS * IDX_DIM, IDX_DIM, IDX_HEADS)
_EVEN_OFF = np.concatenate([[0], np.cumsum(_EVEN_SIZES)])
_ZERO_COL = int(_EVEN_OFF[-1])


def _nsa_head_perm():
    pos = np.zeros(_NSA_Q, np.int64)
    for c in range(NSA_HPG):
        for g in range(NSA_GROUPS):
            for d in range(HEAD_DIM):
                pos[c * LANES + g * HEAD_DIM + d] = (g * NSA_HPG + c) * HEAD_DIM + d
    return pos


def _even_columns():
    o = _EVEN_OFF
    perm = _nsa_head_perm()
    gate = np.concatenate([o[4] + (perm // HEAD_DIM) * 3 + br for br in range(3)])
    idx_w = np.concatenate([np.arange(o[9], o[10]), np.full(LANES - IDX_HEADS, _ZERO_COL)])
    cols = [
        o[0] + perm,
        np.arange(o[1], o[2]), np.arange(o[2], o[3]), np.arange(o[3], o[4]),
        gate,
        np.arange(o[5], o[6]), np.arange(o[6], o[7]), np.arange(o[7], o[8]),
        np.concatenate([np.arange(o[8], o[9])] * 2),
        idx_w,
    ]
    segs = (
        (_NSA_Q, BF16, None), (_NSA_KV, F32, None), (_NSA_KV, BF16, None), (_NSA_KV, BF16, None),
        (3 * _NSA_Q, F32, "sigmoid"),
        (DSA_HEADS * HEAD_DIM, BF16, None), (DSA_KV_RANK, BF16, "rms"), (IDX_HEADS * IDX_DIM, BF16, None),
        (LANES, BF16, None), (LANES, F32, float(IDX_HEADS ** -0.5 * IDX_DIM ** -0.5)),
    )
    return np.concatenate(cols).astype(np.int32), segs


def _block_diag2(w):
    z = jnp.zeros_like(w)
    return jnp.concatenate([jnp.concatenate([w, z], axis=-1), jnp.concatenate([z, w], axis=-1)], axis=-2)


def _selection_constants(s):
    nc = s // CMP_STRIDE
    nb = s // SEL_BLK
    rows = _cmp_rows(s)
    cs = np.arange(nc) * CMP_STRIDE
    ss = np.arange(nb) * SEL_BLK
    overlap = np.clip(np.minimum(cs[:, None] + CMP_LEN, ss[None, :] + SEL_BLK) - np.maximum(cs[:, None], ss[None, :]), 0, None)
    imp = np.zeros((rows, LANES), np.float32)
    imp[CMP_PAD:CMP_PAD + nc, :nb] = overlap / CMP_STRIDE
    imp[CMP_PAD + nc - 1] = 0.0
    eneg = np.zeros((s, LANES), np.float32)
    eneg[np.arange(s), np.arange(s) // SEL_BLK] = NEG_INF
    return jnp.asarray(imp, BF16), jnp.asarray(eneg, BF16)


def kernel(x, c, rel_bias, ada_w, ada_b, ln_g, ln_b, ev_w_in, ev_w_out, nsa_pe_k, nsa_pe_v, nsa_w1_k, nsa_w2_k, nsa_w1_v, nsa_w2_v, dsa_kv_norm, dsa_w_uk, dsa_w_uv, od_w_in, od_w_out, diff_lam, diff_subln, mlp_w1, mlp_w2):
    bsz, s, d = x.shape
    depth = ada_w.shape[0]
    assert s % TQ == 0 and s // SEL_BLK <= LANES
    alpha = (2 * depth) ** 0.25
    topk = min(DSA_TOPK, s // 4)

    ada = _ada_call(c, ada_w, ada_b)
    t_tiles, tc_tiles = _bias_call(rel_bias)
    imp, eneg = _selection_constants(s)
    even_cols, even_segs = _even_columns()
    perm = _nsa_head_perm()
    odd_segs = ((od_w_in.shape[-1], BF16, None),)
    unit_gain = jnp.ones((1, LANES), F32)

    for l in range(depth):
        sh1, sc1, g1, sh2, sc2, g2 = [a.reshape(bsz, 1, d) for a in jnp.split(ada[l], 6, axis=-1)]
        i = l // 2
        if l % 2 == 0:
            w_aug = jnp.concatenate([ev_w_in[i], jnp.zeros((d, 1), F32)], axis=1)
            w_in = jnp.take(w_aug, even_cols, axis=1).astype(BF16)
            nsa_q, kvc, kvs, kvw, gates, dsa_q, ckv, idx_q, ik2, idx_w = _proj_call(
                x, sh1, sc1, w_in, dsa_kv_norm[i].reshape(1, LANES), even_segs)
            pe2 = jnp.stack([jnp.tile(nsa_pe_k[i], (1, 2)), jnp.tile(nsa_pe_v[i], (1, 2))])
            w1bd = jnp.stack([_block_diag2(w.reshape(CMP_LEN, HEAD_DIM, CMP_HIDDEN)) for w in (nsa_w1_k[i], nsa_w1_v[i])]).astype(BF16)
            w2bd = jnp.stack([_block_diag2(nsa_w2_k[i]), _block_diag2(nsa_w2_v[i])]).astype(BF16)
            kc, vc = _compress_call(kvc, pe2, w1bd, w2bd)
            o_a = _nsa_call(nsa_q, gates, kc, vc, kvs, kvw, imp, eneg, t_tiles, tc_tiles)
            wuk = dsa_w_uk[i].reshape(DSA_KV_RANK, DSA_HEADS // 2, LANES).transpose(1, 0, 2).astype(BF16)
            wuv_h = dsa_w_uv[i].transpose(1, 0, 2)
            zeros = jnp.zeros_like(wuv_h)
            even_h = (np.arange(DSA_HEADS) % 2 == 0)[:, None, None]
            wuv = jnp.where(even_h, jnp.concatenate([wuv_h, zeros], -1), jnp.concatenate([zeros, wuv_h], -1))
            wuv = wuv.transpose(0, 2, 1).astype(BF16)
            o_b = _dsa_call(dsa_q, ckv, idx_q, ik2, idx_w, wuk, wuv, t_tiles, topk)
            attn = jnp.concatenate([o_a, o_b], axis=-1)
            w_out = jnp.concatenate([ev_w_out[i][perm], ev_w_out[i][_NSA_Q:]], axis=0).astype(BF16)
        else:
            lambda_init = 0.8 - 0.6 * math.exp(-0.3 * l)
            (qkv,) = _proj_call(x, sh1, sc1, od_w_in[i].astype(BF16), unit_gain, odd_segs)
            attn = _diff_call(qkv, t_tiles, diff_lam[i], diff_subln[i].reshape(1, LANES), lambda_init)
            w_out = od_w_out[i].astype(BF16)
        x = _out_call(x, attn, w_out, g1, ln_g[l, 0].reshape(1, d), ln_b[l, 0].reshape(1, d), alpha)
        x = _mlp_call(x, sh2, sc2, g2, mlp_w1[l].astype(BF16), mlp_w2[l].astype(BF16),
                      ln_g[l, 1].reshape(1, d), ln_b[l, 1].reshape(1, d), alpha)
    return x
```

```python
import functools
import math

import jax
import jax.numpy as jnp
import numpy as np
from jax import lax
from jax.experimental import pallas as pl
from jax.experimental.pallas import tpu as pltpu

HEAD_DIM = 64
NSA_HEADS = 8
NSA_GROUPS = 2
NSA_HPG = NSA_HEADS // NSA_GROUPS
CMP_LEN = 32
CMP_STRIDE = 16
CMP_HIDDEN = 256
SEL_BLK = 64
N_SEL = 16
WINDOW = 512
DSA_HEADS = 8
DSA_KV_RANK = 128
IDX_HEADS = 8
IDX_DIM = 64
DSA_TOPK = 256
DIFF_HEADS = 8
DIFF_DIM = 64
NUM_BUCKETS = 32
MAX_DISTANCE = 128
NEG_INF = -1e30
BIG = 1e9
EPS = 1e-5

LANES = 128
TQ = 256
DIFF_NQ = 1
COUNT_ROWS = 64
MIN_NORMAL = float(np.finfo(np.float32).tiny)
CMP_PAD = 16
CMP_WIN = 32
VMEM_LIMIT = 56 * 1024 * 1024
LOG2E = math.log2(math.e)

F32 = jnp.float32
BF16 = jnp.bfloat16


def _bucket_table():
    n = np.arange(MAX_DISTANCE)
    exact = NUM_BUCKETS // 2
    nf = np.maximum(n, 1).astype(np.float64)
    large = exact + (np.log(nf / exact) / math.log(MAX_DISTANCE / exact) * (NUM_BUCKETS - exact)).astype(np.int64)
    return np.where(n < exact, n, np.minimum(large, NUM_BUCKETS - 1)).astype(np.int32)


def _dot(a, b):
    return jnp.dot(a, b, preferred_element_type=F32)


def _dot_tn(a, b):
    return lax.dot_general(a, b, (((0,), (0,)), ((), ())), preferred_element_type=F32)


def _params(*sem):
    return pltpu.CompilerParams(dimension_semantics=sem, vmem_limit_bytes=VMEM_LIMIT)


def _const_spec(shape):
    nd = len(shape)
    return pl.BlockSpec(shape, lambda *_: (0,) * nd)


def _ada_kernel(c_ref, w_ref, b_ref, o_ref):
    c = c_ref[...]
    ca = c * jax.nn.sigmoid(c)
    o_ref[0] = _dot(ca, w_ref[0]) + b_ref[0]


def _ada_call(c, ada_w, ada_b):
    depth, d, n = ada_w.shape
    bsz = c.shape[0]
    rows = 8
    cp = jnp.pad(c, ((0, rows - bsz), (0, 0)))
    tn = 1536
    out = pl.pallas_call(
        _ada_kernel,
        out_shape=jax.ShapeDtypeStruct((depth, rows, n), F32),
        grid=(depth, n // tn),
        in_specs=[
            pl.BlockSpec((rows, d), lambda l, j: (0, 0)),
            pl.BlockSpec((1, d, tn), lambda l, j: (l, 0, j)),
            pl.BlockSpec((1, 1, tn), lambda l, j: (l, 0, j)),
        ],
        out_specs=pl.BlockSpec((1, rows, tn), lambda l, j: (l, 0, j)),
        compiler_params=_params("arbitrary", "arbitrary"),
        name="ada_table",
    )(cp, ada_w, ada_b.reshape(depth, 1, n))
    return out[:, :bsz]


def _proj_kernel(segs, x_ref, sh_ref, sc_ref, w_ref, gain_ref, *o_refs):
    h = (x_ref[0] * (1.0 + sc_ref[0]) + sh_ref[0]).astype(BF16)
    off = 0
    for (width, _, epi), o_ref in zip(segs, o_refs):
        y = _dot(h, w_ref[:, off:off + width])
        if epi == "sigmoid":
            y = jax.nn.sigmoid(y)
        elif epi == "rms":
            y = y * lax.rsqrt(jnp.mean(jnp.square(y), axis=-1, keepdims=True) + EPS) * gain_ref[...]
        elif isinstance(epi, float):
            y = y * epi
        o_ref[0] = y.astype(o_ref.dtype)
        off += width


def _proj_call(x, sh, sc, w, gain, segs, tm=256):
    bsz, s, d = x.shape
    ntot = w.shape[1]
    return pl.pallas_call(
        functools.partial(_proj_kernel, segs),
        out_shape=[jax.ShapeDtypeStruct((bsz, s, width), dt) for width, dt, _ in segs],
        grid=(bsz, s // tm),
        in_specs=[
            pl.BlockSpec((1, tm, d), lambda b, i: (b, i, 0)),
            pl.BlockSpec((1, 1, d), lambda b, i: (b, 0, 0)),
            pl.BlockSpec((1, 1, d), lambda b, i: (b, 0, 0)),
            _const_spec((d, ntot)),
            _const_spec((1, LANES)),
        ],
        out_specs=[pl.BlockSpec((1, tm, width), lambda b, i: (b, i, 0)) for width, _, _ in segs],
        compiler_params=_params("parallel", "parallel"),
        name="modulate_in_proj",
    )(x, sh, sc, w, gain)


def _layer_norm(z, g, b):
    mu = jnp.mean(z, axis=-1, keepdims=True)
    zc = z - mu
    var = jnp.mean(jnp.square(zc), axis=-1, keepdims=True)
    return zc * lax.rsqrt(var + EPS) * g + b


def _out_kernel(alpha, x_ref, a_ref, w_ref, g1_ref, lg_ref, lb_ref, o_ref):
    y = _dot(a_ref[0], w_ref[...])
    z = alpha * x_ref[0] + (1.0 + g1_ref[0]) * y
    o_ref[0] = _layer_norm(z, lg_ref[...], lb_ref[...])


def _out_call(x, attn, w_out, g1, ln_g, ln_b, alpha, tm=512):
    bsz, s, d = x.shape
    k = attn.shape[-1]
    return pl.pallas_call(
        functools.partial(_out_kernel, alpha),
        out_shape=jax.ShapeDtypeStruct((bsz, s, d), F32),
        grid=(bsz, s // tm),
        in_specs=[
            pl.BlockSpec((1, tm, d), lambda b, i: (b, i, 0)),
            pl.BlockSpec((1, tm, k), lambda b, i: (b, i, 0)),
            _const_spec((k, d)),
            pl.BlockSpec((1, 1, d), lambda b, i: (b, 0, 0)),
            _const_spec((1, d)),
            _const_spec((1, d)),
        ],
        out_specs=pl.BlockSpec((1, tm, d), lambda b, i: (b, i, 0)),
        compiler_params=_params("parallel", "parallel"),
        name="out_proj_ln",
    )(x, attn, w_out, g1, ln_g, ln_b)


def _mlp_kernel(alpha, x_ref, sh_ref, sc_ref, g2_ref, w1_ref, w2_ref, lg_ref, lb_ref, o_ref, h_scr, acc_scr):
    f = pl.program_id(2)

    @pl.when(f == 0)
    def _():
        h_scr[...] = (x_ref[0] * (1.0 + sc_ref[0]) + sh_ref[0]).astype(BF16)
        acc_scr[...] = jnp.zeros_like(acc_scr)

    a = jnp.maximum(_dot(h_scr[...], w1_ref[...]), 0.0)
    acc_scr[...] += _dot((a * a).astype(BF16), w2_ref[...])

    @pl.when(f == pl.num_programs(2) - 1)
    def _():
        z = alpha * x_ref[0] + (1.0 + g2_ref[0]) * acc_scr[...]
        o_ref[0] = _layer_norm(z, lg_ref[...], lb_ref[...])


def _mlp_call(x, sh, sc, g2, w1, w2, ln_g, ln_b, alpha, tm=1024, tf=512):
    bsz, s, d = x.shape
    ff = w1.shape[1]
    tm = min(tm, s)
    assert s % tm == 0
    return pl.pallas_call(
        functools.partial(_mlp_kernel, alpha),
        out_shape=jax.ShapeDtypeStruct((bsz, s, d), F32),
        grid=(bsz, s // tm, ff // tf),
        in_specs=[
            pl.BlockSpec((1, tm, d), lambda b, i, f: (b, i, 0)),
            pl.BlockSpec((1, 1, d), lambda b, i, f: (b, 0, 0)),
            pl.BlockSpec((1, 1, d), lambda b, i, f: (b, 0, 0)),
            pl.BlockSpec((1, 1, d), lambda b, i, f: (b, 0, 0)),
            pl.BlockSpec((d, tf), lambda b, i, f: (0, f)),
            pl.BlockSpec((tf, d), lambda b, i, f: (f, 0)),
            _const_spec((1, d)),
            _const_spec((1, d)),
        ],
        out_specs=pl.BlockSpec((1, tm, d), lambda b, i, f: (b, i, 0)),
        scratch_shapes=[pltpu.VMEM((tm, d), BF16), pltpu.VMEM((tm, d), F32)],
        compiler_params=_params("parallel", "parallel", "arbitrary"),
        name="mlp_ln",
    )(x, sh, sc, g2, w1, w2, ln_g, ln_b)


def _bias_kernel(tab_ref, t_ref, tc_ref):
    h = pl.program_id(0)
    key = lax.broadcasted_iota(jnp.int32, (TQ, TQ), 0)
    qry = lax.broadcasted_iota(jnp.int32, (TQ, TQ), 1)
    win = lax.broadcasted_iota(jnp.int32, (LANES, TQ), 0)
    qry_c = lax.broadcasted_iota(jnp.int32, (LANES, TQ), 1)
    dists = (qry - key, TQ + qry - key, qry_c - CMP_STRIDE * (win - CMP_PAD) - (CMP_LEN - 1))

    def body(d, accs):
        val = tab_ref[h * MAX_DISTANCE + d] * LOG2E
        return tuple(jnp.where(dist == d, val, acc) for dist, acc in zip(dists, accs))

    accs = lax.fori_loop(0, MAX_DISTANCE, body, tuple(jnp.zeros(dist.shape, F32) for dist in dists))
    t_ref[0] = jnp.where(dists[0] < 0, NEG_INF, accs[0])
    t_ref[1] = accs[1]
    tc_ref[...] = jnp.where((dists[2] < 0) | (win >= CMP_WIN), NEG_INF, accs[2])


def _bias_call(rel_bias):
    nh = rel_bias.shape[1]
    shifted = rel_bias[_bucket_table()] - rel_bias[NUM_BUCKETS - 1]
    tab = shifted.T.reshape(-1)
    return pl.pallas_call(
        _bias_kernel,
        out_shape=[
            jax.ShapeDtypeStruct((2, TQ, nh * TQ), F32),
            jax.ShapeDtypeStruct((LANES, nh * TQ), F32),
        ],
        grid=(nh,),
        in_specs=[pl.BlockSpec(memory_space=pltpu.SMEM)],
        out_specs=[
            pl.BlockSpec((2, TQ, TQ), lambda h: (0, 0, h)),
            pl.BlockSpec((LANES, TQ), lambda h: (0, h)),
        ],
        compiler_params=_params("arbitrary"),
        name="t5_bias_tiles",
    )(tab)


def _flash_init(m_ref, l_ref, acc_ref):
    m_ref[...] = jnp.full(m_ref.shape, NEG_INF, F32)
    l_ref[...] = jnp.zeros(l_ref.shape, F32)
    acc_ref[...] = jnp.zeros(acc_ref.shape, F32)


def _flash_update(s, v_tile, m_ref, l_ref, acc_ref):
    m_prev = m_ref[...]
    m_new = jnp.maximum(m_prev, jnp.max(s, axis=0, keepdims=True))
    alpha = jnp.exp2(m_prev - m_new)
    p = jnp.exp2(s - m_new)
    l_ref[...] = alpha * l_ref[...] + jnp.sum(p, axis=0, keepdims=True)
    acc_ref[...] = alpha * acc_ref[...] + _dot_tn(v_tile, p.astype(BF16))
    m_ref[...] = m_new


def _near_bias(t_ref, head, near, u):
    d = u - (near - 1)
    if d < 0:
        return NEG_INF
    if d >= 2:
        return None
    return t_ref[d, :, head * TQ:(head + 1) * TQ]


def _flash_causal(qt, nq, qk_fn, mask_fn, bias_fn, v_fn, bufs, m_ref, l_ref, acc_ref):
    even_set, odd_set = bufs
    sets = (even_set, odd_set)
    width = even_set[0].shape[1]
    blocks = [slice(b * TQ, (b + 1) * TQ) for b in range(width // TQ)]
    last = qt * nq + nq - 1
    n_far = jnp.maximum(qt * nq - 1, 0)
    _flash_init(m_ref, l_ref, acc_ref)
    for _, p_buf, a_buf, _ in bufs:
        p_buf[...] = jnp.zeros(p_buf.shape, p_buf.dtype)
        a_buf[...] = jnp.ones(a_buf.shape, F32)

    def logits(kt, dst, near):
        s = qk_fn(kt)
        mask = mask_fn(kt)
        tops = []
        for b, blk in enumerate(blocks):
            sb = s[:, blk]
            if mask is not None:
                sb = sb + mask
            bias = None if near is None else bias_fn(near, b)
            if bias is not None:
                sb = sb + bias
            dst[0][:, blk] = sb
            tops.append(jnp.max(sb, axis=0, keepdims=True))
        dst[3][...] = jnp.concatenate(tops, axis=1)

    def add_bias(cur, near):
        tops = []
        for b, blk in enumerate(blocks):
            bias = bias_fn(near, b)
            if bias is None:
                tops.append(cur[3][:, blk])
                continue
            sb = cur[0][:, blk] + bias
            cur[0][:, blk] = sb
            tops.append(jnp.max(sb, axis=0, keepdims=True))
        cur[3][...] = jnp.concatenate(tops, axis=1)

    def softmax(cur):
        m_prev = m_ref[...]
        m_new = jnp.maximum(m_prev, cur[3][...])
        alpha = jnp.exp2(m_prev - m_new)
        sums = []
        for blk in blocks:
            p = jnp.exp2(cur[0][:, blk] - m_new[:, blk])
            sums.append(jnp.sum(p, axis=0, keepdims=True))
            cur[1][:, blk] = p.astype(BF16)
        l_ref[...] = alpha * l_ref[...] + jnp.concatenate(sums, axis=1)
        m_ref[...] = m_new
        cur[2][...] = alpha

    def values(kt, src):
        acc_ref[...] = src[2][...] * acc_ref[...] + _dot_tn(v_fn(jnp.maximum(kt, 0)), src[1][...])

    def trip(kt, cur, nxt):
        logits(kt + 1, nxt, None)
        softmax(cur)
        values(kt - 1, nxt)

    first_is_even = (last & 1) == 0
    pl.when(qt == 0)(functools.partial(logits, 0, sets[(nq - 1) & 1], 1))
    pl.when((qt >= 1) & first_is_even)(functools.partial(logits, 0, even_set, None))
    pl.when((qt >= 1) & jnp.logical_not(first_is_even))(functools.partial(logits, 0, odd_set, None))

    second = sets[(nq + 1) & 1]
    first = sets[nq & 1]
    start = n_far & 1
    pl.when(start == 1)(functools.partial(trip, 0, second, first))

    def pair(j, carry):
        kt = start + 2 * j
        trip(kt, first, second)
        trip(kt + 1, second, first)
        return carry

    lax.fori_loop(0, (n_far - start) // 2, pair, 0)

    def near_tile(near):
        cur = sets[(nq - near) & 1]
        kt = last - (nq - near)
        if near == 0:
            add_bias(cur, near)
        if near < nq:
            logits(kt + 1, sets[(nq - near - 1) & 1], near + 1)
        softmax(cur)
        values(kt - 1, sets[(nq - near + 1) & 1])

    pl.when(qt >= 1)(functools.partial(near_tile, 0))
    for near in range(1, nq + 1):
        near_tile(near)
    values(last, even_set)


def _flash_scratch(width):
    stats = [pltpu.VMEM((1, width), F32), pltpu.VMEM((1, width), F32), pltpu.VMEM((LANES, width), F32)]
    one_set = [pltpu.VMEM((TQ, width), F32), pltpu.VMEM((TQ, width), BF16), pltpu.VMEM((1, width), F32),
               pltpu.VMEM((1, width), F32)]
    return stats + 2 * one_set


def _flash_args(flash):
    return (tuple(flash[3:7]), tuple(flash[7:11])), flash[0], flash[1], flash[2]


def _half_rows(half, width):
    row = lax.broadcasted_iota(jnp.int32, (LANES, width), 0)
    return (row >= half * HEAD_DIM) & (row < (half + 1) * HEAD_DIM)


def _key_tile(ref, kt):
    return ref[0, pl.ds(pl.multiple_of(kt * TQ, TQ), TQ), :]


def _diff_kernel(out_scale, q_ref, k_ref, v_ref, t_ref, lam_ref, sub_ref, o_ref,
                 qt_scr, *flash):
    qt = pl.program_id(2)
    qw = DIFF_NQ * TQ
    q_t = (q_ref[0].astype(F32) * (DIFF_DIM ** -0.5 * LOG2E)).T
    for m in range(2):
        qt_scr[:, m * qw:(m + 1) * qw] = jnp.where(_half_rows(m, qw), q_t, 0.0).astype(BF16)

    _flash_causal(
        qt, DIFF_NQ,
        lambda kt: _dot(_key_tile(k_ref, kt), qt_scr[...]),
        lambda kt: None,
        lambda near, b: _near_bias(t_ref, b // DIFF_NQ, near, b % DIFF_NQ),
        lambda kt: _key_tile(v_ref, kt),
        *_flash_args(flash))

    o = flash[2][...] / flash[1][...]
    lam = lam_ref[...]
    lam_full = (jnp.exp(jnp.sum(lam[0:1] * lam[1:2], axis=1, keepdims=True))
                - jnp.exp(jnp.sum(lam[2:3] * lam[3:4], axis=1, keepdims=True)) + (1.0 - out_scale))
    od = (o[:, :qw] - lam_full * o[:, qw:]).T
    od = od * lax.rsqrt(jnp.mean(jnp.square(od), axis=-1, keepdims=True) + EPS) * sub_ref[...]
    o_ref[0] = (od * out_scale).astype(o_ref.dtype)


def _diff_call(qkv, t_tiles, lam, subln, lambda_init):
    bsz, s, _ = qkv.shape
    nh = DIFF_HEADS
    qw = DIFF_NQ * TQ
    assert s % qw == 0
    return pl.pallas_call(
        functools.partial(_diff_kernel, 1.0 - lambda_init),
        out_shape=jax.ShapeDtypeStruct((bsz, s, nh * LANES), BF16),
        grid=(bsz, nh, s // qw),
        in_specs=[
            pl.BlockSpec((1, qw, LANES), lambda b, h, i: (b, i, h)),
            pl.BlockSpec((1, s, LANES), lambda b, h, i: (b, 0, nh + h)),
            pl.BlockSpec((1, s, LANES), lambda b, h, i: (b, 0, 2 * nh + h)),
            pl.BlockSpec((2, TQ, 2 * TQ), lambda b, h, i: (0, 0, h)),
            _const_spec((4, DIFF_DIM)),
            _const_spec((1, LANES)),
        ],
        out_specs=pl.BlockSpec((1, qw, LANES), lambda b, h, i: (b, i, h)),
        scratch_shapes=[pltpu.VMEM((LANES, 2 * qw), BF16)] + _flash_scratch(2 * qw),
        compiler_params=_params("parallel", "parallel", "arbitrary"),
        name="diff_attention",
    )(qkv, qkv, qkv, t_tiles, lam, subln)


def _compress_kernel(nc, xk_ref, xv_ref, pe_ref, w1_ref, w2_ref, kc_ref, vc_ref):
    half = CMP_LEN // 2
    for kv, (x_ref, o_ref) in enumerate(((xk_ref, kc_ref), (xv_ref, vc_ref))):
        first = jnp.zeros((nc, NSA_GROUPS * CMP_HIDDEN), F32)
        second = jnp.zeros_like(first)
        for l in range(half):
            x = x_ref[0, pl.ds(l, nc, stride=CMP_STRIDE), :]
            first += _dot((x + pe_ref[kv, l:l + 1]).astype(BF16), w1_ref[kv, l])
            second += _dot((x + pe_ref[kv, half + l:half + l + 1]).astype(BF16), w1_ref[kv, half + l])
        pre = first + pltpu.roll(second, nc - 1, 0)
        hid = jax.nn.gelu(pre, approximate=True)
        o_ref[0] = jnp.zeros(o_ref.shape[1:], o_ref.dtype)
        o_ref[0, CMP_PAD:CMP_PAD + nc, :] = _dot(hid.astype(BF16), w2_ref[kv]).astype(o_ref.dtype)


def _cmp_rows(s):
    nc = s // CMP_STRIDE
    return -(-(nc + LANES) // LANES) * LANES


def _compress_call(kvc, pe2, w1bd, w2bd):
    bsz, s, _ = kvc.shape
    nc = s // CMP_STRIDE
    rows = _cmp_rows(s)
    return pl.pallas_call(
        functools.partial(_compress_kernel, nc),
        out_shape=[jax.ShapeDtypeStruct((bsz, rows, LANES), BF16)] * 2,
        grid=(bsz,),
        in_specs=[
            pl.BlockSpec((1, s, LANES), lambda b: (b, 0, 0)),
            pl.BlockSpec((1, s, LANES), lambda b: (b, 0, 1)),
            _const_spec(pe2.shape),
            _const_spec(w1bd.shape),
            _const_spec(w2bd.shape),
        ],
        out_specs=[pl.BlockSpec((1, rows, LANES), lambda b: (b, 0, 0))] * 2,
        compiler_params=_params("parallel"),
        name="nsa_compress",
    )(kvc, kvc, pe2, w1bd, w2bd)


def _nsa_kernel(q_ref, gate_ref, kc_ref, vc_ref, ks_ref, vs_ref, kw_ref, vw_ref, imp_ref, eneg_ref, t_ref, tc_ref,
                o_ref, qpad_scr, lhs_scr, res_scr, *flash):
    m_scr, l_scr, acc_scr = flash[:3]
    qt = pl.program_id(1)
    q0 = qt * TQ
    n0 = pl.multiple_of(qt * (TQ // CMP_STRIDE), TQ // CMP_STRIDE)
    rows = kc_ref.shape[1]
    hq = NSA_HPG * TQ

    for c in range(NSA_HPG):
        q_t = (q_ref[0, :, c * LANES:(c + 1) * LANES].astype(F32) * (HEAD_DIM ** -0.5 * LOG2E)).T
        for g in range(NSA_GROUPS):
            qpad_scr[g, :, c * TQ:(c + 1) * TQ] = jnp.where(_half_rows(g, TQ), q_t, 0.0).astype(BF16)

    blk = lax.broadcasted_iota(jnp.int32, (LANES, TQ), 0)
    blkf = blk.astype(F32)
    cur = (q0 + lax.broadcasted_iota(jnp.int32, (LANES, TQ), 1)) // SEL_BLK
    allowed = blk <= cur
    forced = (blk == 0) | (blk == cur) | (blk == cur - 1)
    rowp = lax.broadcasted_iota(jnp.int32, (rows, TQ), 0)
    far_bias = jnp.where((rowp >= CMP_PAD) & (rowp < n0), 0.0, NEG_INF)
    near_bias = jnp.where((n0 + lax.broadcasted_iota(jnp.int32, (LANES, TQ), 0)) >= CMP_PAD, 0.0, NEG_INF)
    key_i = lax.broadcasted_iota(jnp.int32, (TQ, TQ), 0)
    qry_i = lax.broadcasted_iota(jnp.int32, (TQ, TQ), 1)
    win_edge = jnp.where(key_i > qry_i, 0.0, NEG_INF)

    for g in range(NSA_GROUPS):
        qp = qpad_scr[g]
        heads = slice(g * hq, (g + 1) * hq)
        kc_near = kc_ref[0, pl.ds(n0, LANES), :]
        vc_near = vc_ref[0, pl.ds(n0, LANES), :]
        s_far = _dot(kc_ref[0], qp)
        s_near = _dot(kc_near, qp) + tc_ref[:, heads]
        psum_far = jnp.zeros((rows, TQ), F32)
        psum_near = jnp.zeros((LANES, TQ), F32)
        p_far, p_near = [], []
        for c in range(NSA_HPG):
            sf = s_far[:, c * TQ:(c + 1) * TQ] + far_bias
            sn = s_near[:, c * TQ:(c + 1) * TQ] + near_bias
            mx = jnp.maximum(jnp.maximum(jnp.max(sf, axis=0, keepdims=True), jnp.max(sn, axis=0, keepdims=True)),
                             0.1 * NEG_INF)
            pf = jnp.exp2(sf - mx)
            pn = jnp.exp2(sn - mx)
            den = jnp.sum(pf, axis=0, keepdims=True) + jnp.sum(pn, axis=0, keepdims=True)
            inv = 1.0 / jnp.maximum(den, 1e-30)
            pf = pf * inv
            pn = pn * inv
            psum_far += pf
            psum_near += pn
            p_far.append(pf.astype(BF16))
            p_near.append(pn.astype(BF16))
        res_scr[0, g] = (_dot_tn(vc_ref[0], jnp.concatenate(p_far, axis=1))
                         + _dot_tn(vc_near, jnp.concatenate(p_near, axis=1)))
        score = jnp.zeros((LANES, TQ), F32)
        for psum, imp in ((psum_far, imp_ref[...]), (psum_near, imp_ref[pl.ds(n0, LANES), :])):
            hi = psum.astype(BF16)
            lo = (psum - hi.astype(F32)).astype(BF16)
            score += _dot_tn(imp, hi) + _dot_tn(imp, lo)
        score = jnp.where(forced, BIG, jnp.where(allowed, score, -BIG))
        unpicked = jnp.ones((LANES, TQ), F32)
        for _ in range(N_SEL):
            best = jnp.max(score, axis=0, keepdims=True)
            first = jnp.min(jnp.where(score == best, blkf, float(LANES)), axis=0, keepdims=True)
            pick = blkf == first
            unpicked = jnp.where(pick, 0.0, unpicked)
            score = jnp.where(pick, -jnp.inf, score)
        notsel = jnp.maximum(jnp.where(allowed, 0.0, 1.0), unpicked).astype(BF16)
        for c in range(NSA_HPG):
            lhs_scr[:LANES, c * TQ:(c + 1) * TQ] = notsel
        lhs_scr[LANES:, :] = qp

        def sel_logits(kt):
            k0 = pl.multiple_of(kt * TQ, TQ)
            return _dot(jnp.concatenate([eneg_ref[pl.ds(k0, TQ), :], _key_tile(ks_ref, kt)], axis=1), lhs_scr[...])

        _flash_causal(
            qt, 1, sel_logits,
            lambda kt: None,
            lambda near, b: _near_bias(t_ref, g * NSA_HPG + b, near, 0),
            lambda kt: _key_tile(vs_ref, kt),
            *_flash_args(flash))
        res_scr[1, g] = acc_scr[...] / l_scr[...]

        @pl.when(qt >= 2)
        def _():
            k0 = pl.multiple_of((qt - 2) * TQ, TQ)
            s = _dot(kw_ref[0, pl.ds(k0, 3 * TQ), :], qp)
            probs, sums = [], []
            for c in range(NSA_HPG):
                blk = slice(c * TQ, (c + 1) * TQ)
                hblk = slice((g * NSA_HPG + c) * TQ, (g * NSA_HPG + c + 1) * TQ)
                sb = s[:, blk] + jnp.concatenate([win_edge, t_ref[1, :, hblk], t_ref[0, :, hblk]], axis=0)
                p = jnp.exp2(sb - jnp.max(sb, axis=0, keepdims=True))
                sums.append(jnp.sum(p, axis=0, keepdims=True))
                probs.append(p.astype(BF16))
            out = _dot_tn(vw_ref[0, pl.ds(k0, 3 * TQ), :], jnp.concatenate(probs, axis=1))
            res_scr[2, g] = out / jnp.concatenate(sums, axis=1)

        @pl.when(qt < 2)
        def _():
            _flash_init(m_scr, l_scr, acc_scr)

            def win_step(kt, near):
                s = _dot(_key_tile(kw_ref, kt), qp) + t_ref[near, :, heads]
                _flash_update(s, _key_tile(vw_ref, kt), m_scr, l_scr, acc_scr)

            pl.when(qt >= 1)(functools.partial(win_step, qt - 1, 1))
            win_step(qt, 0)
            res_scr[2, g] = acc_scr[...] / l_scr[...]

    for c in range(NSA_HPG):
        out = jnp.zeros((TQ, LANES), F32)
        for br in range(3):
            gate = gate_ref[0, :, br * NSA_HEADS * HEAD_DIM + c * LANES:br * NSA_HEADS * HEAD_DIM + (c + 1) * LANES]
            both = jnp.concatenate([res_scr[br, 0, :HEAD_DIM, c * TQ:(c + 1) * TQ],
                                    res_scr[br, 1, HEAD_DIM:, c * TQ:(c + 1) * TQ]], axis=0)
            out += gate * both.T
        o_ref[0, :, c * LANES:(c + 1) * LANES] = out.astype(o_ref.dtype)


def _nsa_call(q, gates, kc, vc, kvs, kvw, imp, eneg, t_tiles, tc_tiles):
    bsz, s, _ = q.shape
    rows = kc.shape[1]
    hq = NSA_HPG * TQ
    whole = lambda col: pl.BlockSpec((1, s, LANES), lambda b, i: (b, 0, col))
    return pl.pallas_call(
        _nsa_kernel,
        out_shape=jax.ShapeDtypeStruct((bsz, s, NSA_HEADS * HEAD_DIM), BF16),
        grid=(bsz, s // TQ),
        in_specs=[
            pl.BlockSpec((1, TQ, NSA_HEADS * HEAD_DIM), lambda b, i: (b, i, 0)),
            pl.BlockSpec((1, TQ, 3 * NSA_HEADS * HEAD_DIM), lambda b, i: (b, i, 0)),
            pl.BlockSpec((1, rows, LANES), lambda b, i: (b, 0, 0)),
            pl.BlockSpec((1, rows, LANES), lambda b, i: (b, 0, 0)),
            whole(0), whole(1), whole(0), whole(1),
            _const_spec(imp.shape),
            _const_spec(eneg.shape),
            pl.BlockSpec((2, TQ, NSA_HEADS * TQ), lambda b, i: (0, 0, 0)),
            pl.BlockSpec((LANES, NSA_HEADS * TQ), lambda b, i: (0, 0)),
        ],
        out_specs=pl.BlockSpec((1, TQ, NSA_HEADS * HEAD_DIM), lambda b, i: (b, i, 0)),
        scratch_shapes=[
            pltpu.VMEM((NSA_GROUPS, LANES, hq), BF16),
            pltpu.VMEM((2 * LANES, hq), BF16),
            pltpu.VMEM((3, NSA_GROUPS, LANES, hq), F32),
        ] + _flash_scratch(hq),
        compiler_params=_params("parallel", "arbitrary"),
        name="nsa_attention",
    )(q, gates, kc, vc, kvs, kvs, kvw, kvw, imp, eneg, t_tiles, tc_tiles)


def _sort_key(x):
    b = lax.bitcast_convert_type(x, jnp.int32)
    return b ^ ((b >> 31) & jnp.int32(0x7FFFFFFF))


def _dsa_kernel(topk, dq_ref, ckv_ref, iq_ref, ik_ref, iw_ref, wuk_ref, wuv_ref, t_ref, o_ref,
                key_scr, lhs_scr, thr_scr, *flash):
    qt = pl.program_id(1)
    nh = DSA_HEADS
    key_i = lax.broadcasted_iota(jnp.int32, (TQ, TQ), 0)
    qry_i = lax.broadcasted_iota(jnp.int32, (TQ, TQ), 1)
    int_min = jnp.int32(-2 ** 31)

    for c in range(IDX_HEADS // 2):
        col_t = iq_ref[0, :, c * LANES:(c + 1) * LANES].astype(F32).T.astype(BF16)
        for hl in range(2):
            j = 2 * c + hl
            lhs_scr[:IDX_DIM, j * TQ:(j + 1) * TQ] = col_t[hl * IDX_DIM:(hl + 1) * IDX_DIM]
    iw_t = iw_ref[0].T

    def idx_tile(kt, causal):
        s = _dot(_key_tile(ik_ref, kt)[:, :IDX_DIM], lhs_scr[:IDX_DIM, :])
        isc = jnp.zeros((TQ, TQ), F32)
        for j in range(IDX_HEADS):
            isc += iw_t[j:j + 1, :] * jnp.maximum(s[:, j * TQ:(j + 1) * TQ], 0.0)
        if causal:
            isc = jnp.where(key_i <= qry_i, isc, NEG_INF)
        k0 = pl.multiple_of(kt * TQ, TQ)
        zero_key = -2 - (k0 + key_i)
        key_scr[pl.ds(k0, TQ), :] = jnp.where(jnp.abs(isc) < MIN_NORMAL, zero_key, _sort_key(isc))

    def idx_far(kt, carry):
        idx_tile(kt, False)
        return carry

    lax.fori_loop(0, qt, idx_far, 0)
    idx_tile(qt, True)

    @pl.when((qt & 1) == 0)
    def _():
        key_scr[pl.ds(pl.multiple_of((qt + 1) * TQ, TQ), TQ), :] = jnp.full((TQ, TQ), int_min, jnp.int32)

    def count(pred_fn):
        def body(kp, cnt):
            k0 = pl.multiple_of(kp * (2 * TQ), 2 * TQ)
            tile = key_scr[pl.ds(k0, 2 * TQ), :]
            hit = jnp.where(pred_fn(tile, k0), 1.0, 0.0)
            return cnt + jnp.sum(hit.reshape(2 * TQ // COUNT_ROWS, COUNT_ROWS, TQ), axis=0)

        cnt = lax.fori_loop(0, (qt + 2) // 2, body, jnp.zeros((COUNT_ROWS, TQ), F32))
        return jnp.sum(cnt, axis=0, keepdims=True)

    kf = float(topk)
    n_pos = count(lambda tile, k0: tile >= 0)
    thr0 = jnp.where(n_pos >= kf, jnp.int32(0), int_min)
    n_all = (qt + 1).astype(F32) * TQ

    def bit_body(i, carry):
        thr, n_ge = carry
        cand = thr | (jnp.int32(1) << (30 - i))
        n_cand = count(lambda tile, k0: tile >= cand)
        ok = n_cand >= kf
        return jnp.where(ok, cand, thr), jnp.where(ok, n_cand, n_ge)

    thr, n_ge = lax.fori_loop(0, 31, bit_body, (thr0, jnp.where(n_pos >= kf, n_pos, n_all)))
    thr_scr[...] = thr

    excess = jnp.max(jnp.where(n_ge > kf, 1.0, 0.0))

    @pl.when(excess > 0.0)
    def _():
        need = kf - count(lambda tile, k0: tile > thr)
        pos_i = lax.broadcasted_iota(jnp.int32, (2 * TQ, TQ), 0)

        def pos_body(i, bound):
            cand = bound | (jnp.int32(1) << (30 - i))
            below = count(lambda tile, k0: (tile == thr) & (k0 + pos_i < cand))
            return jnp.where(below <= need - 1.0, cand, bound)

        bound = lax.fori_loop(0, 31, pos_body, jnp.zeros((1, TQ), jnp.int32))

        def demote(kt, carry):
            k0 = pl.multiple_of(kt * TQ, TQ)
            tile = key_scr[pl.ds(k0, TQ), :]
            key_scr[pl.ds(k0, TQ), :] = jnp.where((tile == thr) & (k0 + key_i > bound), tile - 1, tile)
            return carry

        lax.fori_loop(0, qt + 1, demote, 0)

    for c in range(nh // 2):
        col_t = dq_ref[0, :, c * LANES:(c + 1) * LANES].astype(F32).T
        for hl in range(2):
            h = 2 * c + hl
            qm = jnp.where(_half_rows(hl, TQ), col_t, 0.0).astype(BF16)
            lhs_scr[:, h * TQ:(h + 1) * TQ] = (_dot(wuk_ref[c], qm) * (HEAD_DIM ** -0.5 * LOG2E)).astype(BF16)

    _flash_causal(
        qt, 1,
        lambda kt: _dot(_key_tile(ckv_ref, kt), lhs_scr[...]),
        lambda kt: jnp.where(key_scr[pl.ds(pl.multiple_of(kt * TQ, TQ), TQ), :] >= thr_scr[...], 0.0, NEG_INF),
        lambda near, b: _near_bias(t_ref, b, near, 0),
        lambda kt: _key_tile(ckv_ref, kt),
        *_flash_args(flash))

    o_lat = (flash[2][...] / flash[1][...]).astype(BF16)
    for c in range(nh // 2):
        out = (_dot(wuv_ref[2 * c], o_lat[:, (2 * c) * TQ:(2 * c + 1) * TQ])
               + _dot(wuv_ref[2 * c + 1], o_lat[:, (2 * c + 1) * TQ:(2 * c + 2) * TQ]))
        o_ref[0, :, c * LANES:(c + 1) * LANES] = out.T.astype(o_ref.dtype)


def _dsa_call(dq, ckv, iq, ik2, iw, wuk, wuv, t_tiles, topk):
    bsz, s, _ = dq.shape
    nh = DSA_HEADS
    hq = nh * TQ
    return pl.pallas_call(
        functools.partial(_dsa_kernel, topk),
        out_shape=jax.ShapeDtypeStruct((bsz, s, nh * HEAD_DIM), BF16),
        grid=(bsz, s // TQ),
        in_specs=[
            pl.BlockSpec((1, TQ, nh * HEAD_DIM), lambda b, i: (b, i, 0)),
            pl.BlockSpec((1, s, LANES), lambda b, i: (b, 0, 0)),
            pl.BlockSpec((1, TQ, IDX_HEADS * IDX_DIM), lambda b, i: (b, i, 0)),
            pl.BlockSpec((1, s, LANES), lambda b, i: (b, 0, 0)),
            pl.BlockSpec((1, TQ, LANES), lambda b, i: (b, i, 0)),
            _const_spec(wuk.shape),
            _const_spec(wuv.shape),
            pl.BlockSpec((2, TQ, hq), lambda b, i: (0, 0, 1)),
        ],
        out_specs=pl.BlockSpec((1, TQ, nh * HEAD_DIM), lambda b, i: (b, i, 0)),
        scratch_shapes=[
            pltpu.VMEM((s + TQ, TQ), jnp.int32),
            pltpu.VMEM((LANES, hq), BF16),
            pltpu.VMEM((1, TQ), jnp.int32),
        ] + _flash_scratch(hq),
        compiler_params=_params("parallel", "arbitrary"),
        name="dsa_attention",
    )(dq, ckv, iq, ik2, iw, wuk, wuv, t_tiles)


_NSA_Q = NSA_HEADS * HEAD_DIM
_NSA_KV = 2 * NSA_GROUPS * HEAD_DIM
_EVEN_SIZES = (_NSA_Q, _NSA_KV, _NSA_KV, _NSA_KV, NSA_HEADS * 3, DSA_HEADS * HEAD_DIM, DSA_KV_RANK,
               IDX_HEADS * IDX_DIM, IDX_DIM, IDX_HEADS)
_EVEN_OFF = np.concatenate([[0], np.cumsum(_EVEN_SIZES)])
_ZERO_COL = int(_EVEN_OFF[-1])


def _nsa_head_perm():
    pos = np.zeros(_NSA_Q, np.int64)
    for c in range(NSA_HPG):
        for g in range(NSA_GROUPS):
            for d in range(HEAD_DIM):
                pos[c * LANES + g * HEAD_DIM + d] = (g * NSA_HPG + c) * HEAD_DIM + d
    return pos


def _even_columns():
    o = _EVEN_OFF
    perm = _nsa_head_perm()
    gate = np.concatenate([o[4] + (perm // HEAD_DIM) * 3 + br for br in range(3)])
    idx_w = np.concatenate([np.arange(o[9], o[10]), np.full(LANES - IDX_HEADS, _ZERO_COL)])
    cols = [
        o[0] + perm,
        np.arange(o[1], o[2]), np.arange(o[2], o[3]), np.arange(o[3], o[4]),
        gate,
        np.arange(o[5], o[6]), np.arange(o[6], o[7]), np.arange(o[7], o[8]),
        np.concatenate([np.arange(o[8], o[9])] * 2),
        idx_w,
    ]
    segs = (
        (_NSA_Q, BF16, None), (_NSA_KV, F32, None), (_NSA_KV, BF16, None), (_NSA_KV, BF16, None),
        (3 * _NSA_Q, F32, "sigmoid"),
        (DSA_HEADS * HEAD_DIM, BF16, None), (DSA_KV_RANK, BF16, "rms"), (IDX_HEADS * IDX_DIM, BF16, None),
        (LANES, BF16, None), (LANES, F32, float(IDX_HEADS ** -0.5 * IDX_DIM ** -0.5)),
    )
    return np.concatenate(cols).astype(np.int32), segs


def _block_diag2(w):
    z = jnp.zeros_like(w)
    return jnp.concatenate([jnp.concatenate([w, z], axis=-1), jnp.concatenate([z, w], axis=-1)], axis=-2)


def _selection_constants(s):
    nc = s // CMP_STRIDE
    nb = s // SEL_BLK
    rows = _cmp_rows(s)
    cs = np.arange(nc) * CMP_STRIDE
    ss = np.arange(nb) * SEL_BLK
    overlap = np.clip(np.minimum(cs[:, None] + CMP_LEN, ss[None, :] + SEL_BLK) - np.maximum(cs[:, None], ss[None, :]), 0, None)
    imp = np.zeros((rows, LANES), np.float32)
    imp[CMP_PAD:CMP_PAD + nc, :nb] = overlap / CMP_STRIDE
    imp[CMP_PAD + nc - 1] = 0.0
    eneg = np.zeros((s, LANES), np.float32)
    eneg[np.arange(s), np.arange(s) // SEL_BLK] = NEG_INF
    return jnp.asarray(imp, BF16), jnp.asarray(eneg, BF16)


def kernel(x, c, rel_bias, ada_w, ada_b, ln_g, ln_b, ev_w_in, ev_w_out, nsa_pe_k, nsa_pe_v, nsa_w1_k, nsa_w2_k, nsa_w1_v, nsa_w2_v, dsa_kv_norm, dsa_w_uk, dsa_w_uv, od_w_in, od_w_out, diff_lam, diff_subln, mlp_w1, mlp_w2):
    bsz, s, d = x.shape
    depth = ada_w.shape[0]
    assert s % TQ == 0 and s // SEL_BLK <= LANES
    alpha = (2 * depth) ** 0.25
    topk = min(DSA_TOPK, s // 4)

    ada = _ada_call(c, ada_w, ada_b)
    t_tiles, tc_tiles = _bias_call(rel_bias)
    imp, eneg = _selection_constants(s)
    even_cols, even_segs = _even_columns()
    perm = _nsa_head_perm()
    odd_segs = ((od_w_in.shape[-1], BF16, None),)
    unit_gain = jnp.ones((1, LANES), F32)

    for l in range(depth):
        sh1, sc1, g1, sh2, sc2, g2 = [a.reshape(bsz, 1, d) for a in jnp.split(ada[l], 6, axis=-1)]
        i = l // 2
        if l % 2 == 0:
            w_aug = jnp.concatenate([ev_w_in[i], jnp.zeros((d, 1), F32)], axis=1)
            w_in = jnp.take(w_aug, even_cols, axis=1).astype(BF16)
            nsa_q, kvc, kvs, kvw, gates, dsa_q, ckv, idx_q, ik2, idx_w = _proj_call(
                x, sh1, sc1, w_in, dsa_kv_norm[i].reshape(1, LANES), even_segs)
            pe2 = jnp.stack([jnp.tile(nsa_pe_k[i], (1, 2)), jnp.tile(nsa_pe_v[i], (1, 2))])
            w1bd = jnp.stack([_block_diag2(w.reshape(CMP_LEN, HEAD_DIM, CMP_HIDDEN)) for w in (nsa_w1_k[i], nsa_w1_v[i])]).astype(BF16)
            w2bd = jnp.stack([_block_diag2(nsa_w2_k[i]), _block_diag2(nsa_w2_v[i])]).astype(BF16)
            kc, vc = _compress_call(kvc, pe2, w1bd, w2bd)
            o_a = _nsa_call(nsa_q, gates, kc, vc, kvs, kvw, imp, eneg, t_tiles, tc_tiles)
            wuk = dsa_w_uk[i].reshape(DSA_KV_RANK, DSA_HEADS // 2, LANES).transpose(1, 0, 2).astype(BF16)
            wuv_h = dsa_w_uv[i].transpose(1, 0, 2)
            zeros = jnp.zeros_like(wuv_h)
            even_h = (np.arange(DSA_HEADS) % 2 == 0)[:, None, None]
            wuv = jnp.where(even_h, jnp.concatenate([wuv_h, zeros], -1), jnp.concatenate([zeros, wuv_h], -1))
            wuv = wuv.transpose(0, 2, 1).astype(BF16)
            o_b = _dsa_call(dsa_q, ckv, idx_q, ik2, idx_w, wuk, wuv, t_tiles, topk)
            attn = jnp.concatenate([o_a, o_b], axis=-1)
            w_out = jnp.concatenate([ev_w_out[i][perm], ev_w_out[i][_NSA_Q:]], axis=0).astype(BF16)
        else:
            lambda_init = 0.8 - 0.6 * math.exp(-0.3 * l)
            (qkv,) = _proj_call(x, sh1, sc1, od_w_in[i].astype(BF16), unit_gain, odd_segs)
            attn = _diff_call(qkv, t_tiles, diff_lam[i], diff_subln[i].reshape(1, LANES), lambda_init)
            w_out = od_w_out[i].astype(BF16)
        x = _out_call(x, attn, w_out, g1, ln_g[l, 0].reshape(1, d), ln_b[l, 0].reshape(1, d), alpha)
        x = _mlp_call(x, sh2, sc2, g2, mlp_w1[l].astype(BF16), mlp_w2[l].astype(BF16),
                      ln_g[l, 1].reshape(1, d), ln_b[l, 1].reshape(1, d), alpha)
    return x
```

```python
import functools
import math

import jax
import jax.numpy as jnp
import numpy as np
from jax import lax
from jax.experimental import pallas as pl
from jax.experimental.pallas import tpu as pltpu

HEAD_DIM = 64
NSA_HEADS = 8
NSA_GROUPS = 2
NSA_HPG = NSA_HEADS // NSA_GROUPS
CMP_LEN = 32
CMP_STRIDE = 16
CMP_HIDDEN = 256
SEL_BLK = 64
N_SEL = 16
WINDOW = 512
DSA_HEADS = 8
DSA_KV_RANK = 128
IDX_HEADS = 8
IDX_DIM = 64
DSA_TOPK = 256
DIFF_HEADS = 8
DIFF_DIM = 64
NUM_BUCKETS = 32
MAX_DISTANCE = 128
NEG_INF = -1e30
BIG = 1e9
EPS = 1e-5

LANES = 128
TQ = 256
DIFF_NQ = 1
COUNT_ROWS = 64
MIN_NORMAL = float(np.finfo(np.float32).tiny)
HALF_RANGE = 1 << 15
CMP_PAD = 16
CMP_WIN = 32
VMEM_LIMIT = 56 * 1024 * 1024
LOG2E = math.log2(math.e)

F32 = jnp.float32
BF16 = jnp.bfloat16


def _bucket_table():
    n = np.arange(MAX_DISTANCE)
    exact = NUM_BUCKETS // 2
    nf = np.maximum(n, 1).astype(np.float64)
    large = exact + (np.log(nf / exact) / math.log(MAX_DISTANCE / exact) * (NUM_BUCKETS - exact)).astype(np.int64)
    return np.where(n < exact, n, np.minimum(large, NUM_BUCKETS - 1)).astype(np.int32)


def _dot(a, b):
    return jnp.dot(a, b, preferred_element_type=F32)


def _dot_tn(a, b):
    return lax.dot_general(a, b, (((0,), (0,)), ((), ())), preferred_element_type=F32)


def _params(*sem):
    return pltpu.CompilerParams(dimension_semantics=sem, vmem_limit_bytes=VMEM_LIMIT)


def _const_spec(shape):
    nd = len(shape)
    return pl.BlockSpec(shape, lambda *_: (0,) * nd)


def _ada_kernel(c_ref, w_ref, b_ref, o_ref):
    c = c_ref[...]
    ca = c * jax.nn.sigmoid(c)
    o_ref[0] = _dot(ca, w_ref[0]) + b_ref[0]


def _ada_call(c, ada_w, ada_b):
    depth, d, n = ada_w.shape
    bsz = c.shape[0]
    rows = 8
    cp = jnp.pad(c, ((0, rows - bsz), (0, 0)))
    tn = 1536
    out = pl.pallas_call(
        _ada_kernel,
        out_shape=jax.ShapeDtypeStruct((depth, rows, n), F32),
        grid=(depth, n // tn),
        in_specs=[
            pl.BlockSpec((rows, d), lambda l, j: (0, 0)),
            pl.BlockSpec((1, d, tn), lambda l, j: (l, 0, j)),
            pl.BlockSpec((1, 1, tn), lambda l, j: (l, 0, j)),
        ],
        out_specs=pl.BlockSpec((1, rows, tn), lambda l, j: (l, 0, j)),
        compiler_params=_params("arbitrary", "arbitrary"),
        name="ada_table",
    )(cp, ada_w, ada_b.reshape(depth, 1, n))
    return out[:, :bsz]


def _proj_kernel(segs, x_ref, sh_ref, sc_ref, w_ref, gain_ref, *o_refs):
    h = (x_ref[0] * (1.0 + sc_ref[0]) + sh_ref[0]).astype(BF16)
    off = 0
    for (width, _, epi), o_ref in zip(segs, o_refs):
        y = _dot(h, w_ref[:, off:off + width])
        if epi == "sigmoid":
            y = jax.nn.sigmoid(y)
        elif epi == "rms":
            y = y * lax.rsqrt(jnp.mean(jnp.square(y), axis=-1, keepdims=True) + EPS) * gain_ref[...]
        elif isinstance(epi, float):
            y = y * epi
        o_ref[0] = y.astype(o_ref.dtype)
        off += width


def _proj_call(x, sh, sc, w, gain, segs, tm=256):
    bsz, s, d = x.shape
    ntot = w.shape[1]
    return pl.pallas_call(
        functools.partial(_proj_kernel, segs),
        out_shape=[jax.ShapeDtypeStruct((bsz, s, width), dt) for width, dt, _ in segs],
        grid=(bsz, s // tm),
        in_specs=[
            pl.BlockSpec((1, tm, d), lambda b, i: (b, i, 0)),
            pl.BlockSpec((1, 1, d), lambda b, i: (b, 0, 0)),
            pl.BlockSpec((1, 1, d), lambda b, i: (b, 0, 0)),
            _const_spec((d, ntot)),
            _const_spec((1, LANES)),
        ],
        out_specs=[pl.BlockSpec((1, tm, width), lambda b, i: (b, i, 0)) for width, _, _ in segs],
        compiler_params=_params("parallel", "parallel"),
        name="modulate_in_proj",
    )(x, sh, sc, w, gain)


def _layer_norm(z, g, b):
    mu = jnp.mean(z, axis=-1, keepdims=True)
    zc = z - mu
    var = jnp.mean(jnp.square(zc), axis=-1, keepdims=True)
    return zc * lax.rsqrt(var + EPS) * g + b


def _out_kernel(alpha, x_ref, a_ref, w_ref, g1_ref, lg_ref, lb_ref, o_ref):
    y = _dot(a_ref[0], w_ref[...])
    z = alpha * x_ref[0] + (1.0 + g1_ref[0]) * y
    o_ref[0] = _layer_norm(z, lg_ref[...], lb_ref[...])


def _out_call(x, attn, w_out, g1, ln_g, ln_b, alpha, tm=512):
    bsz, s, d = x.shape
    k = attn.shape[-1]
    return pl.pallas_call(
        functools.partial(_out_kernel, alpha),
        out_shape=jax.ShapeDtypeStruct((bsz, s, d), F32),
        grid=(bsz, s // tm),
        in_specs=[
            pl.BlockSpec((1, tm, d), lambda b, i: (b, i, 0)),
            pl.BlockSpec((1, tm, k), lambda b, i: (b, i, 0)),
            _const_spec((k, d)),
            pl.BlockSpec((1, 1, d), lambda b, i: (b, 0, 0)),
            _const_spec((1, d)),
            _const_spec((1, d)),
        ],
        out_specs=pl.BlockSpec((1, tm, d), lambda b, i: (b, i, 0)),
        compiler_params=_params("parallel", "parallel"),
        name="out_proj_ln",
    )(x, attn, w_out, g1, ln_g, ln_b)


def _mlp_kernel(alpha, x_ref, sh_ref, sc_ref, g2_ref, w1_ref, w2_ref, lg_ref, lb_ref, o_ref, h_scr, acc_scr):
    f = pl.program_id(2)

    @pl.when(f == 0)
    def _():
        h_scr[...] = (x_ref[0] * (1.0 + sc_ref[0]) + sh_ref[0]).astype(BF16)
        acc_scr[...] = jnp.zeros_like(acc_scr)

    a = jnp.maximum(_dot(h_scr[...], w1_ref[...]), 0.0)
    acc_scr[...] += _dot((a * a).astype(BF16), w2_ref[...])

    @pl.when(f == pl.num_programs(2) - 1)
    def _():
        z = alpha * x_ref[0] + (1.0 + g2_ref[0]) * acc_scr[...]
        o_ref[0] = _layer_norm(z, lg_ref[...], lb_ref[...])


def _mlp_call(x, sh, sc, g2, w1, w2, ln_g, ln_b, alpha, tm=1024, tf=512):
    bsz, s, d = x.shape
    ff = w1.shape[1]
    tm = min(tm, s)
    assert s % tm == 0
    return pl.pallas_call(
        functools.partial(_mlp_kernel, alpha),
        out_shape=jax.ShapeDtypeStruct((bsz, s, d), F32),
        grid=(bsz, s // tm, ff // tf),
        in_specs=[
            pl.BlockSpec((1, tm, d), lambda b, i, f: (b, i, 0)),
            pl.BlockSpec((1, 1, d), lambda b, i, f: (b, 0, 0)),
            pl.BlockSpec((1, 1, d), lambda b, i, f: (b, 0, 0)),
            pl.BlockSpec((1, 1, d), lambda b, i, f: (b, 0, 0)),
            pl.BlockSpec((d, tf), lambda b, i, f: (0, f)),
            pl.BlockSpec((tf, d), lambda b, i, f: (f, 0)),
            _const_spec((1, d)),
            _const_spec((1, d)),
        ],
        out_specs=pl.BlockSpec((1, tm, d), lambda b, i, f: (b, i, 0)),
        scratch_shapes=[pltpu.VMEM((tm, d), BF16), pltpu.VMEM((tm, d), F32)],
        compiler_params=_params("parallel", "parallel", "arbitrary"),
        name="mlp_ln",
    )(x, sh, sc, g2, w1, w2, ln_g, ln_b)


def _bias_kernel(tab_ref, t_ref, tc_ref):
    h = pl.program_id(0)
    key = lax.broadcasted_iota(jnp.int32, (TQ, TQ), 0)
    qry = lax.broadcasted_iota(jnp.int32, (TQ, TQ), 1)
    win = lax.broadcasted_iota(jnp.int32, (LANES, TQ), 0)
    qry_c = lax.broadcasted_iota(jnp.int32, (LANES, TQ), 1)
    dists = (qry - key, TQ + qry - key, qry_c - CMP_STRIDE * (win - CMP_PAD) - (CMP_LEN - 1))

    def body(d, accs):
        val = tab_ref[h * MAX_DISTANCE + d] * LOG2E
        return tuple(jnp.where(dist == d, val, acc) for dist, acc in zip(dists, accs))

    accs = lax.fori_loop(0, MAX_DISTANCE, body, tuple(jnp.zeros(dist.shape, F32) for dist in dists))
    t_ref[0] = jnp.where(dists[0] < 0, NEG_INF, accs[0])
    t_ref[1] = accs[1]
    tc_ref[...] = jnp.where((dists[2] < 0) | (win >= CMP_WIN), NEG_INF, accs[2])


def _bias_call(rel_bias):
    nh = rel_bias.shape[1]
    shifted = rel_bias[_bucket_table()] - rel_bias[NUM_BUCKETS - 1]
    tab = shifted.T.reshape(-1)
    return pl.pallas_call(
        _bias_kernel,
        out_shape=[
            jax.ShapeDtypeStruct((2, TQ, nh * TQ), F32),
            jax.ShapeDtypeStruct((LANES, nh * TQ), F32),
        ],
        grid=(nh,),
        in_specs=[pl.BlockSpec(memory_space=pltpu.SMEM)],
        out_specs=[
            pl.BlockSpec((2, TQ, TQ), lambda h: (0, 0, h)),
            pl.BlockSpec((LANES, TQ), lambda h: (0, h)),
        ],
        compiler_params=_params("arbitrary"),
        name="t5_bias_tiles",
    )(tab)


def _flash_init(m_ref, l_ref, acc_ref):
    m_ref[...] = jnp.full(m_ref.shape, NEG_INF, F32)
    l_ref[...] = jnp.zeros(l_ref.shape, F32)
    acc_ref[...] = jnp.zeros(acc_ref.shape, F32)


def _flash_update(s, v_tile, m_ref, l_ref, acc_ref):
    m_prev = m_ref[...]
    m_new = jnp.maximum(m_prev, jnp.max(s, axis=0, keepdims=True))
    alpha = jnp.exp2(m_prev - m_new)
    p = jnp.exp2(s - m_new)
    l_ref[...] = alpha * l_ref[...] + jnp.sum(p, axis=0, keepdims=True)
    acc_ref[...] = alpha * acc_ref[...] + _dot_tn(v_tile, p.astype(BF16))
    m_ref[...] = m_new


def _near_bias(t_ref, head, near, u):
    d = u - (near - 1)
    if d < 0:
        return NEG_INF
    if d >= 2:
        return None
    return t_ref[d, :, head * TQ:(head + 1) * TQ]


def _flash_causal(qt, nq, qk_fn, mask_fn, bias_fn, v_fn, bufs, m_ref, l_ref, acc_ref):
    even_set, odd_set = bufs
    sets = (even_set, odd_set)
    width = even_set[0].shape[1]
    blocks = [slice(b * TQ, (b + 1) * TQ) for b in range(width // TQ)]
    last = qt * nq + nq - 1
    n_far = jnp.maximum(qt * nq - 1, 0)
    _flash_init(m_ref, l_ref, acc_ref)
    for _, p_buf, a_buf, _ in bufs:
        p_buf[...] = jnp.zeros(p_buf.shape, p_buf.dtype)
        a_buf[...] = jnp.ones(a_buf.shape, F32)

    def logits(kt, dst, near):
        s = qk_fn(kt)
        mask = mask_fn(kt)
        tops = []
        for b, blk in enumerate(blocks):
            sb = s[:, blk]
            if mask is not None:
                sb = sb + mask
            bias = None if near is None else bias_fn(near, b)
            if bias is not None:
                sb = sb + bias
            dst[0][:, blk] = sb
            tops.append(jnp.max(sb, axis=0, keepdims=True))
        dst[3][...] = jnp.concatenate(tops, axis=1)

    def add_bias(cur, near):
        tops = []
        for b, blk in enumerate(blocks):
            bias = bias_fn(near, b)
            if bias is None:
                tops.append(cur[3][:, blk])
                continue
            sb = cur[0][:, blk] + bias
            cur[0][:, blk] = sb
            tops.append(jnp.max(sb, axis=0, keepdims=True))
        cur[3][...] = jnp.concatenate(tops, axis=1)

    def softmax(cur):
        m_prev = m_ref[...]
        m_new = jnp.maximum(m_prev, cur[3][...])
        alpha = jnp.exp2(m_prev - m_new)
        sums = []
        for blk in blocks:
            p = jnp.exp2(cur[0][:, blk] - m_new[:, blk])
            sums.append(jnp.sum(p, axis=0, keepdims=True))
            cur[1][:, blk] = p.astype(BF16)
        l_ref[...] = alpha * l_ref[...] + jnp.concatenate(sums, axis=1)
        m_ref[...] = m_new
        cur[2][...] = alpha

    def values(kt, src):
        acc_ref[...] = src[2][...] * acc_ref[...] + _dot_tn(v_fn(jnp.maximum(kt, 0)), src[1][...])

    def trip(kt, cur, nxt):
        logits(kt + 1, nxt, None)
        softmax(cur)
        values(kt - 1, nxt)

    first_is_even = (last & 1) == 0
    pl.when(qt == 0)(functools.partial(logits, 0, sets[(nq - 1) & 1], 1))
    pl.when((qt >= 1) & first_is_even)(functools.partial(logits, 0, even_set, None))
    pl.when((qt >= 1) & jnp.logical_not(first_is_even))(functools.partial(logits, 0, odd_set, None))

    second = sets[(nq + 1) & 1]
    first = sets[nq & 1]
    start = n_far & 1
    pl.when(start == 1)(functools.partial(trip, 0, second, first))

    def pair(j, carry):
        kt = start + 2 * j
        trip(kt, first, second)
        trip(kt + 1, second, first)
        return carry

    lax.fori_loop(0, (n_far - start) // 2, pair, 0)

    def near_tile(near):
        cur = sets[(nq - near) & 1]
        kt = last - (nq - near)
        if near == 0:
            add_bias(cur, near)
        if near < nq:
            logits(kt + 1, sets[(nq - near - 1) & 1], near + 1)
        softmax(cur)
        values(kt - 1, sets[(nq - near + 1) & 1])

    pl.when(qt >= 1)(functools.partial(near_tile, 0))
    for near in range(1, nq + 1):
        near_tile(near)
    values(last, even_set)


def _flash_scratch(width):
    stats = [pltpu.VMEM((1, width), F32), pltpu.VMEM((1, width), F32), pltpu.VMEM((LANES, width), F32)]
    one_set = [pltpu.VMEM((TQ, width), F32), pltpu.VMEM((TQ, width), BF16), pltpu.VMEM((1, width), F32),
               pltpu.VMEM((1, width), F32)]
    return stats + 2 * one_set


def _flash_args(flash):
    return (tuple(flash[3:7]), tuple(flash[7:11])), flash[0], flash[1], flash[2]


def _half_rows(half, width):
    row = lax.broadcasted_iota(jnp.int32, (LANES, width), 0)
    return (row >= half * HEAD_DIM) & (row < (half + 1) * HEAD_DIM)


def _key_tile(ref, kt):
    return ref[0, pl.ds(pl.multiple_of(kt * TQ, TQ), TQ), :]


def _diff_kernel(out_scale, q_ref, k_ref, v_ref, t_ref, lam_ref, sub_ref, o_ref,
                 qt_scr, *flash):
    qt = pl.program_id(2)
    qw = DIFF_NQ * TQ
    q_t = (q_ref[0].astype(F32) * (DIFF_DIM ** -0.5 * LOG2E)).T
    for m in range(2):
        qt_scr[:, m * qw:(m + 1) * qw] = jnp.where(_half_rows(m, qw), q_t, 0.0).astype(BF16)

    _flash_causal(
        qt, DIFF_NQ,
        lambda kt: _dot(_key_tile(k_ref, kt), qt_scr[...]),
        lambda kt: None,
        lambda near, b: _near_bias(t_ref, b // DIFF_NQ, near, b % DIFF_NQ),
        lambda kt: _key_tile(v_ref, kt),
        *_flash_args(flash))

    o = flash[2][...] / flash[1][...]
    lam = lam_ref[...]
    lam_full = (jnp.exp(jnp.sum(lam[0:1] * lam[1:2], axis=1, keepdims=True))
                - jnp.exp(jnp.sum(lam[2:3] * lam[3:4], axis=1, keepdims=True)) + (1.0 - out_scale))
    od = (o[:, :qw] - lam_full * o[:, qw:]).T
    od = od * lax.rsqrt(jnp.mean(jnp.square(od), axis=-1, keepdims=True) + EPS) * sub_ref[...]
    o_ref[0] = (od * out_scale).astype(o_ref.dtype)


def _diff_call(qkv, t_tiles, lam, subln, lambda_init):
    bsz, s, _ = qkv.shape
    nh = DIFF_HEADS
    qw = DIFF_NQ * TQ
    assert s % qw == 0
    return pl.pallas_call(
        functools.partial(_diff_kernel, 1.0 - lambda_init),
        out_shape=jax.ShapeDtypeStruct((bsz, s, nh * LANES), BF16),
        grid=(bsz, nh, s // qw),
        in_specs=[
            pl.BlockSpec((1, qw, LANES), lambda b, h, i: (b, i, h)),
            pl.BlockSpec((1, s, LANES), lambda b, h, i: (b, 0, nh + h)),
            pl.BlockSpec((1, s, LANES), lambda b, h, i: (b, 0, 2 * nh + h)),
            pl.BlockSpec((2, TQ, 2 * TQ), lambda b, h, i: (0, 0, h)),
            _const_spec((4, DIFF_DIM)),
            _const_spec((1, LANES)),
        ],
        out_specs=pl.BlockSpec((1, qw, LANES), lambda b, h, i: (b, i, h)),
        scratch_shapes=[pltpu.VMEM((LANES, 2 * qw), BF16)] + _flash_scratch(2 * qw),
        compiler_params=_params("parallel", "parallel", "arbitrary"),
        name="diff_attention",
    )(qkv, qkv, qkv, t_tiles, lam, subln)


def _compress_kernel(nc, xk_ref, xv_ref, pe_ref, w1_ref, w2_ref, kc_ref, vc_ref):
    half = CMP_LEN // 2
    for kv, (x_ref, o_ref) in enumerate(((xk_ref, kc_ref), (xv_ref, vc_ref))):
        first = jnp.zeros((nc, NSA_GROUPS * CMP_HIDDEN), F32)
        second = jnp.zeros_like(first)
        for l in range(half):
            x = x_ref[0, pl.ds(l, nc, stride=CMP_STRIDE), :]
            first += _dot((x + pe_ref[kv, l:l + 1]).astype(BF16), w1_ref[kv, l])
            second += _dot((x + pe_ref[kv, half + l:half + l + 1]).astype(BF16), w1_ref[kv, half + l])
        pre = first + pltpu.roll(second, nc - 1, 0)
        hid = jax.nn.gelu(pre, approximate=True)
        o_ref[0] = jnp.zeros(o_ref.shape[1:], o_ref.dtype)
        o_ref[0, CMP_PAD:CMP_PAD + nc, :] = _dot(hid.astype(BF16), w2_ref[kv]).astype(o_ref.dtype)


def _cmp_rows(s):
    nc = s // CMP_STRIDE
    return -(-(nc + LANES) // LANES) * LANES


def _compress_call(kvc, pe2, w1bd, w2bd):
    bsz, s, _ = kvc.shape
    nc = s // CMP_STRIDE
    rows = _cmp_rows(s)
    return pl.pallas_call(
        functools.partial(_compress_kernel, nc),
        out_shape=[jax.ShapeDtypeStruct((bsz, rows, LANES), BF16)] * 2,
        grid=(bsz,),
        in_specs=[
            pl.BlockSpec((1, s, LANES), lambda b: (b, 0, 0)),
            pl.BlockSpec((1, s, LANES), lambda b: (b, 0, 1)),
            _const_spec(pe2.shape),
            _const_spec(w1bd.shape),
            _const_spec(w2bd.shape),
        ],
        out_specs=[pl.BlockSpec((1, rows, LANES), lambda b: (b, 0, 0))] * 2,
        compiler_params=_params("parallel"),
        name="nsa_compress",
    )(kvc, kvc, pe2, w1bd, w2bd)


def _nsa_kernel(q_ref, gate_ref, kc_ref, vc_ref, ks_ref, vs_ref, kw_ref, vw_ref, imp_ref, eneg_ref, t_ref, tc_ref,
                o_ref, qpad_scr, lhs_scr, res_scr, *flash):
    m_scr, l_scr, acc_scr = flash[:3]
    qt = pl.program_id(1)
    q0 = qt * TQ
    n0 = pl.multiple_of(qt * (TQ // CMP_STRIDE), TQ // CMP_STRIDE)
    rows = kc_ref.shape[1]
    hq = NSA_HPG * TQ

    for c in range(NSA_HPG):
        q_t = (q_ref[0, :, c * LANES:(c + 1) * LANES].astype(F32) * (HEAD_DIM ** -0.5 * LOG2E)).T
        for g in range(NSA_GROUPS):
            qpad_scr[g, :, c * TQ:(c + 1) * TQ] = jnp.where(_half_rows(g, TQ), q_t, 0.0).astype(BF16)

    blk = lax.broadcasted_iota(jnp.int32, (LANES, TQ), 0)
    blkf = blk.astype(F32)
    cur = (q0 + lax.broadcasted_iota(jnp.int32, (LANES, TQ), 1)) // SEL_BLK
    allowed = blk <= cur
    forced = (blk == 0) | (blk == cur) | (blk == cur - 1)
    rowp = lax.broadcasted_iota(jnp.int32, (rows, TQ), 0)
    far_bias = jnp.where((rowp >= CMP_PAD) & (rowp < n0), 0.0, NEG_INF)
    near_bias = jnp.where((n0 + lax.broadcasted_iota(jnp.int32, (LANES, TQ), 0)) >= CMP_PAD, 0.0, NEG_INF)
    key_i = lax.broadcasted_iota(jnp.int32, (TQ, TQ), 0)
    qry_i = lax.broadcasted_iota(jnp.int32, (TQ, TQ), 1)
    win_edge = jnp.where(key_i > qry_i, 0.0, NEG_INF)

    for g in range(NSA_GROUPS):
        qp = qpad_scr[g]
        heads = slice(g * hq, (g + 1) * hq)
        kc_near = kc_ref[0, pl.ds(n0, LANES), :]
        vc_near = vc_ref[0, pl.ds(n0, LANES), :]
        s_far = _dot(kc_ref[0], qp)
        s_near = _dot(kc_near, qp) + tc_ref[:, heads]
        psum_far = jnp.zeros((rows, TQ), F32)
        psum_near = jnp.zeros((LANES, TQ), F32)
        p_far, p_near = [], []
        for c in range(NSA_HPG):
            sf = s_far[:, c * TQ:(c + 1) * TQ] + far_bias
            sn = s_near[:, c * TQ:(c + 1) * TQ] + near_bias
            mx = jnp.maximum(jnp.maximum(jnp.max(sf, axis=0, keepdims=True), jnp.max(sn, axis=0, keepdims=True)),
                             0.1 * NEG_INF)
            pf = jnp.exp2(sf - mx)
            pn = jnp.exp2(sn - mx)
            den = jnp.sum(pf, axis=0, keepdims=True) + jnp.sum(pn, axis=0, keepdims=True)
            inv = 1.0 / jnp.maximum(den, 1e-30)
            pf = pf * inv
            pn = pn * inv
            psum_far += pf
            psum_near += pn
            p_far.append(pf.astype(BF16))
            p_near.append(pn.astype(BF16))
        res_scr[0, g] = (_dot_tn(vc_ref[0], jnp.concatenate(p_far, axis=1))
                         + _dot_tn(vc_near, jnp.concatenate(p_near, axis=1)))
        score = jnp.zeros((LANES, TQ), F32)
        for psum, imp in ((psum_far, imp_ref[...]), (psum_near, imp_ref[pl.ds(n0, LANES), :])):
            hi = psum.astype(BF16)
            lo = (psum - hi.astype(F32)).astype(BF16)
            score += _dot_tn(imp, hi) + _dot_tn(imp, lo)
        score = jnp.where(forced, BIG, jnp.where(allowed, score, -BIG))
        unpicked = jnp.ones((LANES, TQ), F32)
        for _ in range(N_SEL):
            best = jnp.max(score, axis=0, keepdims=True)
            first = jnp.min(jnp.where(score == best, blkf, float(LANES)), axis=0, keepdims=True)
            pick = blkf == first
            unpicked = jnp.where(pick, 0.0, unpicked)
            score = jnp.where(pick, -jnp.inf, score)
        notsel = jnp.maximum(jnp.where(allowed, 0.0, 1.0), unpicked).astype(BF16)
        for c in range(NSA_HPG):
            lhs_scr[:LANES, c * TQ:(c + 1) * TQ] = notsel
        lhs_scr[LANES:, :] = qp

        def sel_logits(kt):
            k0 = pl.multiple_of(kt * TQ, TQ)
            return _dot(jnp.concatenate([eneg_ref[pl.ds(k0, TQ), :], _key_tile(ks_ref, kt)], axis=1), lhs_scr[...])

        _flash_causal(
            qt, 1, sel_logits,
            lambda kt: None,
            lambda near, b: _near_bias(t_ref, g * NSA_HPG + b, near, 0),
            lambda kt: _key_tile(vs_ref, kt),
            *_flash_args(flash))
        res_scr[1, g] = acc_scr[...] / l_scr[...]

        @pl.when(qt >= 2)
        def _():
            k0 = pl.multiple_of((qt - 2) * TQ, TQ)
            s = _dot(kw_ref[0, pl.ds(k0, 3 * TQ), :], qp)
            probs, sums = [], []
            for c in range(NSA_HPG):
                blk = slice(c * TQ, (c + 1) * TQ)
                hblk = slice((g * NSA_HPG + c) * TQ, (g * NSA_HPG + c + 1) * TQ)
                sb = s[:, blk] + jnp.concatenate([win_edge, t_ref[1, :, hblk], t_ref[0, :, hblk]], axis=0)
                p = jnp.exp2(sb - jnp.max(sb, axis=0, keepdims=True))
                sums.append(jnp.sum(p, axis=0, keepdims=True))
                probs.append(p.astype(BF16))
            out = _dot_tn(vw_ref[0, pl.ds(k0, 3 * TQ), :], jnp.concatenate(probs, axis=1))
            res_scr[2, g] = out / jnp.concatenate(sums, axis=1)

        @pl.when(qt < 2)
        def _():
            _flash_init(m_scr, l_scr, acc_scr)

            def win_step(kt, near):
                s = _dot(_key_tile(kw_ref, kt), qp) + t_ref[near, :, heads]
                _flash_update(s, _key_tile(vw_ref, kt), m_scr, l_scr, acc_scr)

            pl.when(qt >= 1)(functools.partial(win_step, qt - 1, 1))
            win_step(qt, 0)
            res_scr[2, g] = acc_scr[...] / l_scr[...]

    for c in range(NSA_HPG):
        out = jnp.zeros((TQ, LANES), F32)
        for br in range(3):
            gate = gate_ref[0, :, br * NSA_HEADS * HEAD_DIM + c * LANES:br * NSA_HEADS * HEAD_DIM + (c + 1) * LANES]
            both = jnp.concatenate([res_scr[br, 0, :HEAD_DIM, c * TQ:(c + 1) * TQ],
                                    res_scr[br, 1, HEAD_DIM:, c * TQ:(c + 1) * TQ]], axis=0)
            out += gate * both.T
        o_ref[0, :, c * LANES:(c + 1) * LANES] = out.astype(o_ref.dtype)


def _nsa_call(q, gates, kc, vc, kvs, kvw, imp, eneg, t_tiles, tc_tiles):
    bsz, s, _ = q.shape
    rows = kc.shape[1]
    hq = NSA_HPG * TQ
    whole = lambda col: pl.BlockSpec((1, s, LANES), lambda b, i: (b, 0, col))
    return pl.pallas_call(
        _nsa_kernel,
        out_shape=jax.ShapeDtypeStruct((bsz, s, NSA_HEADS * HEAD_DIM), BF16),
        grid=(bsz, s // TQ),
        in_specs=[
            pl.BlockSpec((1, TQ, NSA_HEADS * HEAD_DIM), lambda b, i: (b, i, 0)),
            pl.BlockSpec((1, TQ, 3 * NSA_HEADS * HEAD_DIM), lambda b, i: (b, i, 0)),
            pl.BlockSpec((1, rows, LANES), lambda b, i: (b, 0, 0)),
            pl.BlockSpec((1, rows, LANES), lambda b, i: (b, 0, 0)),
            whole(0), whole(1), whole(0), whole(1),
            _const_spec(imp.shape),
            _const_spec(eneg.shape),
            pl.BlockSpec((2, TQ, NSA_HEADS * TQ), lambda b, i: (0, 0, 0)),
            pl.BlockSpec((LANES, NSA_HEADS * TQ), lambda b, i: (0, 0)),
        ],
        out_specs=pl.BlockSpec((1, TQ, NSA_HEADS * HEAD_DIM), lambda b, i: (b, i, 0)),
        scratch_shapes=[
            pltpu.VMEM((NSA_GROUPS, LANES, hq), BF16),
            pltpu.VMEM((2 * LANES, hq), BF16),
            pltpu.VMEM((3, NSA_GROUPS, LANES, hq), F32),
        ] + _flash_scratch(hq),
        compiler_params=_params("parallel", "arbitrary"),
        name="nsa_attention",
    )(q, gates, kc, vc, kvs, kvs, kvw, kvw, imp, eneg, t_tiles, tc_tiles)


def _sort_key(x):
    b = lax.bitcast_convert_type(x, jnp.int32)
    return b ^ ((b >> 31) & jnp.int32(0x7FFFFFFF))


def _dsa_kernel(topk, dq_ref, ckv_ref, iq_ref, ik_ref, iw_ref, wuk_ref, wuv_ref, t_ref, o_ref,
                key_scr, hi_scr, low_scr, lhs_scr, thr_scr, *flash):
    qt = pl.program_id(1)
    nh = DSA_HEADS
    key_i = lax.broadcasted_iota(jnp.int32, (TQ, TQ), 0)
    qry_i = lax.broadcasted_iota(jnp.int32, (TQ, TQ), 1)
    int_min = jnp.int32(-2 ** 31)

    for c in range(IDX_HEADS // 2):
        col_t = iq_ref[0, :, c * LANES:(c + 1) * LANES].astype(F32).T.astype(BF16)
        for hl in range(2):
            j = 2 * c + hl
            lhs_scr[:IDX_DIM, j * TQ:(j + 1) * TQ] = col_t[hl * IDX_DIM:(hl + 1) * IDX_DIM]
    iw_t = iw_ref[0].T

    def idx_tile(kt, causal):
        s = _dot(_key_tile(ik_ref, kt)[:, :IDX_DIM], lhs_scr[:IDX_DIM, :])
        isc = jnp.zeros((TQ, TQ), F32)
        for j in range(IDX_HEADS):
            isc += iw_t[j:j + 1, :] * jnp.maximum(s[:, j * TQ:(j + 1) * TQ], 0.0)
        if causal:
            isc = jnp.where(key_i <= qry_i, isc, NEG_INF)
        k0 = pl.multiple_of(kt * TQ, TQ)
        zero_key = -2 - (k0 + key_i)
        key = jnp.where(jnp.abs(isc) < MIN_NORMAL, zero_key, _sort_key(isc))
        key_scr[pl.ds(k0, TQ), :] = key
        hi_scr[pl.ds(k0, TQ), :] = (key >> 16).astype(jnp.int16)

    def idx_far(kt, carry):
        idx_tile(kt, False)
        return carry

    lax.fori_loop(0, qt, idx_far, 0)
    idx_tile(qt, True)

    @pl.when((qt & 1) == 0)
    def _():
        pad = pl.ds(pl.multiple_of((qt + 1) * TQ, TQ), TQ)
        key_scr[pad, :] = jnp.full((TQ, TQ), int_min, jnp.int32)
        hi_scr[pad, :] = jnp.full((TQ, TQ), -HALF_RANGE, jnp.int16)

    def slab(kp):
        return pl.ds(pl.multiple_of(kp * (2 * TQ), 2 * TQ), 2 * TQ)

    def count(pred_fn):
        def body(kp, cnt):
            hit = jnp.where(pred_fn(key_scr[slab(kp), :], kp * (2 * TQ)), 1.0, 0.0)
            return cnt + jnp.sum(hit.reshape(2 * TQ // COUNT_ROWS, COUNT_ROWS, TQ), axis=0)

        cnt = lax.fori_loop(0, (qt + 2) // 2, body, jnp.zeros((COUNT_ROWS, TQ), F32))
        return jnp.sum(cnt, axis=0, keepdims=True)

    def count_half(src_scr, cand):
        cand16 = cand.astype(jnp.int16)

        def body(kp, cnt):
            hit = jnp.where(src_scr[slab(kp), :] >= cand16, jnp.bfloat16(1), jnp.bfloat16(0))
            for i in range(2 * TQ // COUNT_ROWS):
                cnt = cnt + hit[i * COUNT_ROWS:(i + 1) * COUNT_ROWS]
            return cnt

        cnt = lax.fori_loop(0, (qt + 2) // 2, body, jnp.zeros((COUNT_ROWS, TQ), BF16))
        return jnp.sum(cnt.astype(F32), axis=0, keepdims=True)

    kf = float(topk)
    n_all = (qt + 1).astype(F32) * TQ
    n_pos = count_half(hi_scr, jnp.zeros((1, TQ), jnp.int32))
    hi0 = jnp.where(n_pos >= kf, 0, -HALF_RANGE)

    def hi_body(i, carry):
        hi, n_ge = carry
        cand = hi | (jnp.int32(1) << (14 - i))
        n_cand = count_half(hi_scr, cand)
        ok = n_cand >= kf
        return jnp.where(ok, cand, hi), jnp.where(ok, n_cand, n_ge)

    thr_hi, n_ge = lax.fori_loop(0, 15, hi_body, (hi0, jnp.where(n_pos >= kf, n_pos, n_all)))

    def remap(kp, carry):
        key = key_scr[slab(kp), :]
        hi = key >> 16
        low = (key & (2 * HALF_RANGE - 1)) - HALF_RANGE
        low = jnp.where(hi > thr_hi, HALF_RANGE - 1, jnp.where(hi < thr_hi, -HALF_RANGE, low))
        low_scr[slab(kp), :] = low.astype(jnp.int16)
        return carry

    lax.fori_loop(0, (qt + 2) // 2, remap, 0)

    def low_body(i, carry):
        low, n_ge = carry
        cand = low | (jnp.int32(1) << (15 - i))
        n_cand = count_half(low_scr, cand - HALF_RANGE)
        ok = n_cand >= kf
        return jnp.where(ok, cand, low), jnp.where(ok, n_cand, n_ge)

    thr_low, n_ge = lax.fori_loop(0, 16, low_body, (jnp.zeros((1, TQ), jnp.int32), n_ge))
    thr = (thr_hi << 16) | thr_low
    thr_scr[...] = thr

    excess = jnp.max(jnp.where(n_ge > kf, 1.0, 0.0))

    @pl.when(excess > 0.0)
    def _():
        need = kf - count(lambda tile, k0: tile > thr)
        pos_i = lax.broadcasted_iota(jnp.int32, (2 * TQ, TQ), 0)

        def pos_body(i, bound):
            cand = bound | (jnp.int32(1) << (30 - i))
            below = count(lambda tile, k0: (tile == thr) & (k0 + pos_i < cand))
            return jnp.where(below <= need - 1.0, cand, bound)

        bound = lax.fori_loop(0, 31, pos_body, jnp.zeros((1, TQ), jnp.int32))

        def demote(kt, carry):
            k0 = pl.multiple_of(kt * TQ, TQ)
            tile = key_scr[pl.ds(k0, TQ), :]
            key_scr[pl.ds(k0, TQ), :] = jnp.where((tile == thr) & (k0 + key_i > bound), tile - 1, tile)
            return carry

        lax.fori_loop(0, qt + 1, demote, 0)

    for c in range(nh // 2):
        col_t = dq_ref[0, :, c * LANES:(c + 1) * LANES].astype(F32).T
        for hl in range(2):
            h = 2 * c + hl
            qm = jnp.where(_half_rows(hl, TQ), col_t, 0.0).astype(BF16)
            lhs_scr[:, h * TQ:(h + 1) * TQ] = (_dot(wuk_ref[c], qm) * (HEAD_DIM ** -0.5 * LOG2E)).astype(BF16)

    _flash_causal(
        qt, 1,
        lambda kt: _dot(_key_tile(ckv_ref, kt), lhs_scr[...]),
        lambda kt: jnp.where(key_scr[pl.ds(pl.multiple_of(kt * TQ, TQ), TQ), :] >= thr_scr[...], 0.0, NEG_INF),
        lambda near, b: _near_bias(t_ref, b, near, 0),
        lambda kt: _key_tile(ckv_ref, kt),
        *_flash_args(flash))

    o_lat = (flash[2][...] / flash[1][...]).astype(BF16)
    for c in range(nh // 2):
        out = (_dot(wuv_ref[2 * c], o_lat[:, (2 * c) * TQ:(2 * c + 1) * TQ])
               + _dot(wuv_ref[2 * c + 1], o_lat[:, (2 * c + 1) * TQ:(2 * c + 2) * TQ]))
        o_ref[0, :, c * LANES:(c + 1) * LANES] = out.T.astype(o_ref.dtype)


def _dsa_call(dq, ckv, iq, ik2, iw, wuk, wuv, t_tiles, topk):
    bsz, s, _ = dq.shape
    nh = DSA_HEADS
    hq = nh * TQ
    return pl.pallas_call(
        functools.partial(_dsa_kernel, topk),
        out_shape=jax.ShapeDtypeStruct((bsz, s, nh * HEAD_DIM), BF16),
        grid=(bsz, s // TQ),
        in_specs=[
            pl.BlockSpec((1, TQ, nh * HEAD_DIM), lambda b, i: (b, i, 0)),
            pl.BlockSpec((1, s, LANES), lambda b, i: (b, 0, 0)),
            pl.BlockSpec((1, TQ, IDX_HEADS * IDX_DIM), lambda b, i: (b, i, 0)),
            pl.BlockSpec((1, s, LANES), lambda b, i: (b, 0, 0)),
            pl.BlockSpec((1, TQ, LANES), lambda b, i: (b, i, 0)),
            _const_spec(wuk.shape),
            _const_spec(wuv.shape),
            pl.BlockSpec((2, TQ, hq), lambda b, i: (0, 0, 1)),
        ],
        out_specs=pl.BlockSpec((1, TQ, nh * HEAD_DIM), lambda b, i: (b, i, 0)),
        scratch_shapes=[
            pltpu.VMEM((s + TQ, TQ), jnp.int32),
            pltpu.VMEM((s + TQ, TQ), jnp.int16),
            pltpu.VMEM((s + TQ, TQ), jnp.int16),
            pltpu.VMEM((LANES, hq), BF16),
            pltpu.VMEM((1, TQ), jnp.int32),
        ] + _flash_scratch(hq),
        compiler_params=_params("parallel", "arbitrary"),
        name="dsa_attention",
    )(dq, ckv, iq, ik2, iw, wuk, wuv, t_tiles)


_NSA_Q = NSA_HEADS * HEAD_DIM
_NSA_KV = 2 * NSA_GROUPS * HEAD_DIM
_EVEN_SIZES = (_NSA_Q, _NSA_KV, _NSA_KV, _NSA_KV, NSA_HEADS * 3, DSA_HEADS * HEAD_DIM, DSA_KV_RANK,
               IDX_HEADS * IDX_DIM, IDX_DIM, IDX_HEADS)
_EVEN_OFF = np.concatenate([[0], np.cumsum(_EVEN_SIZES)])
_ZERO_COL = int(_EVEN_OFF[-1])


def _nsa_head_perm():
    pos = np.zeros(_NSA_Q, np.int64)
    for c in range(NSA_HPG):
        for g in range(NSA_GROUPS):
            for d in range(HEAD_DIM):
                pos[c * LANES + g * HEAD_DIM + d] = (g * NSA_HPG + c) * HEAD_DIM + d
    return pos


def _even_columns():
    o = _EVEN_OFF
    perm = _nsa_head_perm()
    gate = np.concatenate([o[4] + (perm // HEAD_DIM) * 3 + br for br in range(3)])
    idx_w = np.concatenate([np.arange(o[9], o[10]), np.full(LANES - IDX_HEADS, _ZERO_COL)])
    cols = [
        o[0] + perm,
        np.arange(o[1], o[2]), np.arange(o[2], o[3]), np.arange(o[3], o[4]),
        gate,
        np.arange(o[5], o[6]), np.arange(o[6], o[7]), np.arange(o[7], o[8]),
        np.concatenate([np.arange(o[8], o[9])] * 2),
        idx_w,
    ]
    segs = (
        (_NSA_Q, BF16, None), (_NSA_KV, F32, None), (_NSA_KV, BF16, None), (_NSA_KV, BF16, None),
        (3 * _NSA_Q, F32, "sigmoid"),
        (DSA_HEADS * HEAD_DIM, BF16, None), (DSA_KV_RANK, BF16, "rms"), (IDX_HEADS * IDX_DIM, BF16, None),
        (LANES, BF16, None), (LANES, F32, float(IDX_HEADS ** -0.5 * IDX_DIM ** -0.5)),
    )
    return np.concatenate(cols).astype(np.int32), segs


def _block_diag2(w):
    z = jnp.zeros_like(w)
    return jnp.concatenate([jnp.concatenate([w, z], axis=-1), jnp.concatenate([z, w], axis=-1)], axis=-2)


def _selection_constants(s):
    nc = s // CMP_STRIDE
    nb = s // SEL_BLK
    rows = _cmp_rows(s)
    cs = np.arange(nc) * CMP_STRIDE
    ss = np.arange(nb) * SEL_BLK
    overlap = np.clip(np.minimum(cs[:, None] + CMP_LEN, ss[None, :] + SEL_BLK) - np.maximum(cs[:, None], ss[None, :]), 0, None)
    imp = np.zeros((rows, LANES), np.float32)
    imp[CMP_PAD:CMP_PAD + nc, :nb] = overlap / CMP_STRIDE
    imp[CMP_PAD + nc - 1] = 0.0
    eneg = np.zeros((s, LANES), np.float32)
    eneg[np.arange(s), np.arange(s) // SEL_BLK] = NEG_INF
    return jnp.asarray(imp, BF16), jnp.asarray(eneg, BF16)


def kernel(x, c, rel_bias, ada_w, ada_b, ln_g, ln_b, ev_w_in, ev_w_out, nsa_pe_k, nsa_pe_v, nsa_w1_k, nsa_w2_k, nsa_w1_v, nsa_w2_v, dsa_kv_norm, dsa_w_uk, dsa_w_uv, od_w_in, od_w_out, diff_lam, diff_subln, mlp_w1, mlp_w2):
    bsz, s, d = x.shape
    depth = ada_w.shape[0]
    assert s % TQ == 0 and s // SEL_BLK <= LANES
    alpha = (2 * depth) ** 0.25
    topk = min(DSA_TOPK, s // 4)

    ada = _ada_call(c, ada_w, ada_b)
    t_tiles, tc_tiles = _bias_call(rel_bias)
    imp, eneg = _selection_constants(s)
    even_cols, even_segs = _even_columns()
    perm = _nsa_head_perm()
    odd_segs = ((od_w_in.shape[-1], BF16, None),)
    unit_gain = jnp.ones((1, LANES), F32)

    for l in range(depth):
        sh1, sc1, g1, sh2, sc2, g2 = [a.reshape(bsz, 1, d) for a in jnp.split(ada[l], 6, axis=-1)]
        i = l // 2
        if l % 2 == 0:
            w_aug = jnp.concatenate([ev_w_in[i], jnp.zeros((d, 1), F32)], axis=1)
            w_in = jnp.take(w_aug, even_cols, axis=1).astype(BF16)
            nsa_q, kvc, kvs, kvw, gates, dsa_q, ckv, idx_q, ik2, idx_w = _proj_call(
                x, sh1, sc1, w_in, dsa_kv_norm[i].reshape(1, LANES), even_segs)
            pe2 = jnp.stack([jnp.tile(nsa_pe_k[i], (1, 2)), jnp.tile(nsa_pe_v[i], (1, 2))])
            w1bd = jnp.stack([_block_diag2(w.reshape(CMP_LEN, HEAD_DIM, CMP_HIDDEN)) for w in (nsa_w1_k[i], nsa_w1_v[i])]).astype(BF16)
            w2bd = jnp.stack([_block_diag2(nsa_w2_k[i]), _block_diag2(nsa_w2_v[i])]).astype(BF16)
            kc, vc = _compress_call(kvc, pe2, w1bd, w2bd)
            o_a = _nsa_call(nsa_q, gates, kc, vc, kvs, kvw, imp, eneg, t_tiles, tc_tiles)
            wuk = dsa_w_uk[i].reshape(DSA_KV_RANK, DSA_HEADS // 2, LANES).transpose(1, 0, 2).astype(BF16)
            wuv_h = dsa_w_uv[i].transpose(1, 0, 2)
            zeros = jnp.zeros_like(wuv_h)
            even_h = (np.arange(DSA_HEADS) % 2 == 0)[:, None, None]
            wuv = jnp.where(even_h, jnp.concatenate([wuv_h, zeros], -1), jnp.concatenate([zeros, wuv_h], -1))
            wuv = wuv.transpose(0, 2, 1).astype(BF16)
            o_b = _dsa_call(dsa_q, ckv, idx_q, ik2, idx_w, wuk, wuv, t_tiles, topk)
            attn = jnp.concatenate([o_a, o_b], axis=-1)
            w_out = jnp.concatenate([ev_w_out[i][perm], ev_w_out[i][_NSA_Q:]], axis=0).astype(BF16)
        else:
            lambda_init = 0.8 - 0.6 * math.exp(-0.3 * l)
            (qkv,) = _proj_call(x, sh1, sc1, od_w_in[i].astype(BF16), unit_gain, odd_segs)
            attn = _diff_call(qkv, t_tiles, diff_lam[i], diff_subln[i].reshape(1, LANES), lambda_init)
            w_out = od_w_out[i].astype(BF16)
        x = _out_call(x, attn, w_out, g1, ln_g[l, 0].reshape(1, d), ln_b[l, 0].reshape(1, d), alpha)
        x = _mlp_call(x, sh2, sc2, g2, mlp_w1[l].astype(BF16), mlp_w2[l].astype(BF16),
                      ln_g[l, 1].reshape(1, d), ln_b[l, 1].reshape(1, d), alpha)
    return x
```

```python
import functools
import math

import jax
import jax.numpy as jnp
import numpy as np
from jax import lax
from jax.experimental import pallas as pl
from jax.experimental.pallas import tpu as pltpu

HEAD_DIM = 64
NSA_HEADS = 8
NSA_GROUPS = 2
NSA_HPG = NSA_HEADS // NSA_GROUPS
CMP_LEN = 32
CMP_STRIDE = 16
CMP_HIDDEN = 256
SEL_BLK = 64
N_SEL = 16
WINDOW = 512
DSA_HEADS = 8
DSA_KV_RANK = 128
IDX_HEADS = 8
IDX_DIM = 64
DSA_TOPK = 256
DIFF_HEADS = 8
DIFF_DIM = 64
NUM_BUCKETS = 32
MAX_DISTANCE = 128
NEG_INF = -1e30
BIG = 1e9
EPS = 1e-5

LANES = 128
TQ = 256
COUNT_ROWS = 64
MIN_NORMAL = float(np.finfo(np.float32).tiny)
HALF_RANGE = 1 << 15
CMP_PAD = 16
CMP_WIN = 32
VMEM_LIMIT = 56 * 1024 * 1024
LOG2E = math.log2(math.e)

F32 = jnp.float32
BF16 = jnp.bfloat16


def _bucket_table():
    n = np.arange(MAX_DISTANCE)
    exact = NUM_BUCKETS // 2
    nf = np.maximum(n, 1).astype(np.float64)
    large = exact + (np.log(nf / exact) / math.log(MAX_DISTANCE / exact) * (NUM_BUCKETS - exact)).astype(np.int64)
    return np.where(n < exact, n, np.minimum(large, NUM_BUCKETS - 1)).astype(np.int32)


def _bucket_starts():
    table = _bucket_table()
    return [(int(b), int(np.argmax(table == b))) for b in sorted(set(table.tolist()))]


def _dot(a, b):
    return jnp.dot(a, b, preferred_element_type=F32)


def _dot_tn(a, b):
    return lax.dot_general(a, b, (((0,), (0,)), ((), ())), preferred_element_type=F32)


def _params(*sem):
    return pltpu.CompilerParams(dimension_semantics=sem, vmem_limit_bytes=VMEM_LIMIT)


def _const_spec(shape):
    nd = len(shape)
    return pl.BlockSpec(shape, lambda *_: (0,) * nd)


def _ada_kernel(c_ref, w_ref, b_ref, o_ref):
    c = c_ref[...]
    ca = c * jax.nn.sigmoid(c)
    o_ref[0] = _dot(ca, w_ref[0]) + b_ref[0]


def _ada_call(c, ada_w, ada_b):
    depth, d, n = ada_w.shape
    bsz = c.shape[0]
    rows = 8
    cp = jnp.pad(c, ((0, rows - bsz), (0, 0)))
    tn = 1536
    out = pl.pallas_call(
        _ada_kernel,
        out_shape=jax.ShapeDtypeStruct((depth, rows, n), F32),
        grid=(depth, n // tn),
        in_specs=[
            pl.BlockSpec((rows, d), lambda l, j: (0, 0)),
            pl.BlockSpec((1, d, tn), lambda l, j: (l, 0, j)),
            pl.BlockSpec((1, 1, tn), lambda l, j: (l, 0, j)),
        ],
        out_specs=pl.BlockSpec((1, rows, tn), lambda l, j: (l, 0, j)),
        compiler_params=_params("arbitrary", "arbitrary"),
        name="ada_table",
    )(cp, ada_w, ada_b.reshape(depth, 1, n))
    return out[:, :bsz]


def _proj_kernel(segs, x_ref, sh_ref, sc_ref, w_ref, gain_ref, *o_refs):
    h = (x_ref[0] * (1.0 + sc_ref[0]) + sh_ref[0]).astype(BF16)
    off = 0
    for (width, _, epi), o_ref in zip(segs, o_refs):
        y = _dot(h, w_ref[:, off:off + width])
        if epi == "sigmoid":
            y = jax.nn.sigmoid(y)
        elif epi == "rms":
            y = y * lax.rsqrt(jnp.mean(jnp.square(y), axis=-1, keepdims=True) + EPS) * gain_ref[...]
        elif isinstance(epi, float):
            y = y * epi
        o_ref[0] = y.astype(o_ref.dtype)
        off += width


def _proj_call(x, sh, sc, w, gain, segs, tm=256):
    bsz, s, d = x.shape
    ntot = w.shape[1]
    return pl.pallas_call(
        functools.partial(_proj_kernel, segs),
        out_shape=[jax.ShapeDtypeStruct((bsz, s, width), dt) for width, dt, _ in segs],
        grid=(bsz, s // tm),
        in_specs=[
            pl.BlockSpec((1, tm, d), lambda b, i: (b, i, 0)),
            pl.BlockSpec((1, 1, d), lambda b, i: (b, 0, 0)),
            pl.BlockSpec((1, 1, d), lambda b, i: (b, 0, 0)),
            _const_spec((d, ntot)),
            _const_spec((1, LANES)),
        ],
        out_specs=[pl.BlockSpec((1, tm, width), lambda b, i: (b, i, 0)) for width, _, _ in segs],
        compiler_params=_params("parallel", "parallel"),
        name="modulate_in_proj",
    )(x, sh, sc, w, gain)


def _layer_norm(z, g, b):
    mu = jnp.mean(z, axis=-1, keepdims=True)
    zc = z - mu
    var = jnp.mean(jnp.square(zc), axis=-1, keepdims=True)
    return zc * lax.rsqrt(var + EPS) * g + b


def _out_kernel(alpha, x_ref, a_ref, w_ref, g1_ref, lg_ref, lb_ref, o_ref):
    y = _dot(a_ref[0], w_ref[...])
    z = alpha * x_ref[0] + (1.0 + g1_ref[0]) * y
    o_ref[0] = _layer_norm(z, lg_ref[...], lb_ref[...])


def _out_call(x, attn, w_out, g1, ln_g, ln_b, alpha, tm=512):
    bsz, s, d = x.shape
    k = attn.shape[-1]
    return pl.pallas_call(
        functools.partial(_out_kernel, alpha),
        out_shape=jax.ShapeDtypeStruct((bsz, s, d), F32),
        grid=(bsz, s // tm),
        in_specs=[
            pl.BlockSpec((1, tm, d), lambda b, i: (b, i, 0)),
            pl.BlockSpec((1, tm, k), lambda b, i: (b, i, 0)),
            _const_spec((k, d)),
            pl.BlockSpec((1, 1, d), lambda b, i: (b, 0, 0)),
            _const_spec((1, d)),
            _const_spec((1, d)),
        ],
        out_specs=pl.BlockSpec((1, tm, d), lambda b, i: (b, i, 0)),
        compiler_params=_params("parallel", "parallel"),
        name="out_proj_ln",
    )(x, attn, w_out, g1, ln_g, ln_b)


def _mlp_kernel(alpha, x_ref, sh_ref, sc_ref, g2_ref, w1_ref, w2_ref, lg_ref, lb_ref, o_ref, h_scr, acc_scr):
    f = pl.program_id(2)

    @pl.when(f == 0)
    def _():
        h_scr[...] = (x_ref[0] * (1.0 + sc_ref[0]) + sh_ref[0]).astype(BF16)
        acc_scr[...] = jnp.zeros_like(acc_scr)

    a = jnp.maximum(_dot(h_scr[...], w1_ref[...]), 0.0)
    acc_scr[...] += _dot((a * a).astype(BF16), w2_ref[...])

    @pl.when(f == pl.num_programs(2) - 1)
    def _():
        z = alpha * x_ref[0] + (1.0 + g2_ref[0]) * acc_scr[...]
        o_ref[0] = _layer_norm(z, lg_ref[...], lb_ref[...])


def _mlp_call(x, sh, sc, g2, w1, w2, ln_g, ln_b, alpha, tm=1024, tf=512):
    bsz, s, d = x.shape
    ff = w1.shape[1]
    tm = min(tm, s)
    assert s % tm == 0
    return pl.pallas_call(
        functools.partial(_mlp_kernel, alpha),
        out_shape=jax.ShapeDtypeStruct((bsz, s, d), F32),
        grid=(bsz, s // tm, ff // tf),
        in_specs=[
            pl.BlockSpec((1, tm, d), lambda b, i, f: (b, i, 0)),
            pl.BlockSpec((1, 1, d), lambda b, i, f: (b, 0, 0)),
            pl.BlockSpec((1, 1, d), lambda b, i, f: (b, 0, 0)),
            pl.BlockSpec((1, 1, d), lambda b, i, f: (b, 0, 0)),
            pl.BlockSpec((d, tf), lambda b, i, f: (0, f)),
            pl.BlockSpec((tf, d), lambda b, i, f: (f, 0)),
            _const_spec((1, d)),
            _const_spec((1, d)),
        ],
        out_specs=pl.BlockSpec((1, tm, d), lambda b, i, f: (b, i, 0)),
        scratch_shapes=[pltpu.VMEM((tm, d), BF16), pltpu.VMEM((tm, d), F32)],
        compiler_params=_params("parallel", "parallel", "arbitrary"),
        name="mlp_ln",
    )(x, sh, sc, g2, w1, w2, ln_g, ln_b)


def _bias_kernel(tab_ref, t_ref, tc_ref):
    h = pl.program_id(0)
    key = lax.broadcasted_iota(jnp.int32, (TQ, TQ), 0)
    qry = lax.broadcasted_iota(jnp.int32, (TQ, TQ), 1)
    win = lax.broadcasted_iota(jnp.int32, (LANES, TQ), 0)
    qry_c = lax.broadcasted_iota(jnp.int32, (LANES, TQ), 1)
    dists = (qry - key, TQ + qry - key, qry_c - CMP_STRIDE * (win - CMP_PAD) - (CMP_LEN - 1))

    accs = [jnp.zeros(dist.shape, F32) for dist in dists]
    for bucket, first in _bucket_starts():
        val = tab_ref[h * NUM_BUCKETS + bucket] * LOG2E
        accs = [jnp.where(dist >= first, val, acc) for dist, acc in zip(dists, accs)]
    accs = [jnp.where(dist >= MAX_DISTANCE, 0.0, acc) for dist, acc in zip(dists, accs)]
    t_ref[0] = jnp.where(dists[0] < 0, NEG_INF, accs[0])
    t_ref[1] = accs[1]
    tc_ref[...] = jnp.where((dists[2] < 0) | (win >= CMP_WIN), NEG_INF, accs[2])


def _bias_call(rel_bias):
    nh = rel_bias.shape[1]
    tab = (rel_bias - rel_bias[NUM_BUCKETS - 1]).T.reshape(-1)
    return pl.pallas_call(
        _bias_kernel,
        out_shape=[
            jax.ShapeDtypeStruct((2, TQ, nh * TQ), F32),
            jax.ShapeDtypeStruct((LANES, nh * TQ), F32),
        ],
        grid=(nh,),
        in_specs=[pl.BlockSpec(memory_space=pltpu.SMEM)],
        out_specs=[
            pl.BlockSpec((2, TQ, TQ), lambda h: (0, 0, h)),
            pl.BlockSpec((LANES, TQ), lambda h: (0, h)),
        ],
        compiler_params=_params("arbitrary"),
        name="t5_bias_tiles",
    )(tab)


def _flash_init(m_ref, l_ref, acc_ref):
    m_ref[...] = jnp.full(m_ref.shape, NEG_INF, F32)
    l_ref[...] = jnp.zeros(l_ref.shape, F32)
    acc_ref[...] = jnp.zeros(acc_ref.shape, F32)


def _flash_update(s, v_tile, m_ref, l_ref, acc_ref):
    m_prev = m_ref[...]
    m_new = jnp.maximum(m_prev, jnp.max(s, axis=0, keepdims=True))
    alpha = jnp.exp2(m_prev - m_new)
    p = jnp.exp2(s - m_new)
    l_ref[...] = alpha * l_ref[...] + jnp.sum(p, axis=0, keepdims=True)
    acc_ref[...] = alpha * acc_ref[...] + _dot_tn(v_tile, p.astype(BF16))
    m_ref[...] = m_new


def _near_bias(t_ref, head, near, u):
    d = u - (near - 1)
    if d < 0:
        return NEG_INF
    if d >= 2:
        return None
    return t_ref[d, :, head * TQ:(head + 1) * TQ]


def _flash_causal(qt, nq, qk_fn, mask_fn, bias_fn, v_fn, bufs, m_ref, l_ref, acc_ref):
    even_set, odd_set = bufs
    sets = (even_set, odd_set)
    width = even_set[0].shape[1]
    blocks = [slice(b * TQ, (b + 1) * TQ) for b in range(width // TQ)]
    last = qt * nq + nq - 1
    n_far = jnp.maximum(qt * nq - 1, 0)
    _flash_init(m_ref, l_ref, acc_ref)
    for _, p_buf, a_buf, _ in bufs:
        p_buf[...] = jnp.zeros(p_buf.shape, p_buf.dtype)
        a_buf[...] = jnp.ones(a_buf.shape, F32)

    def logits(kt, dst, near):
        s = qk_fn(kt)
        mask = mask_fn(kt)
        tops = []
        for b, blk in enumerate(blocks):
            sb = s[:, blk]
            if mask is not None:
                sb = sb + mask
            bias = None if near is None else bias_fn(near, b)
            if bias is not None:
                sb = sb + bias
            dst[0][:, blk] = sb
            tops.append(jnp.max(sb, axis=0, keepdims=True))
        dst[3][...] = jnp.concatenate(tops, axis=1)

    def add_bias(cur, near):
        tops = []
        for b, blk in enumerate(blocks):
            bias = bias_fn(near, b)
            if bias is None:
                tops.append(cur[3][:, blk])
                continue
            sb = cur[0][:, blk] + bias
            cur[0][:, blk] = sb
            tops.append(jnp.max(sb, axis=0, keepdims=True))
        cur[3][...] = jnp.concatenate(tops, axis=1)

    def softmax(cur):
        m_prev = m_ref[...]
        m_new = jnp.maximum(m_prev, cur[3][...])
        alpha = jnp.exp2(m_prev - m_new)
        sums = []
        for blk in blocks:
            p = jnp.exp2(cur[0][:, blk] - m_new[:, blk])
            sums.append(jnp.sum(p, axis=0, keepdims=True))
            cur[1][:, blk] = p.astype(BF16)
        l_ref[...] = alpha * l_ref[...] + jnp.concatenate(sums, axis=1)
        m_ref[...] = m_new
        cur[2][...] = alpha

    def values(kt, src):
        acc_ref[...] = src[2][...] * acc_ref[...] + _dot_tn(v_fn(jnp.maximum(kt, 0)), src[1][...])

    def trip(kt, cur, nxt):
        logits(kt + 1, nxt, None)
        softmax(cur)
        values(kt - 1, nxt)

    first_is_even = (last & 1) == 0
    pl.when(qt == 0)(functools.partial(logits, 0, sets[(nq - 1) & 1], 1))
    pl.when((qt >= 1) & first_is_even)(functools.partial(logits, 0, even_set, None))
    pl.when((qt >= 1) & jnp.logical_not(first_is_even))(functools.partial(logits, 0, odd_set, None))

    second = sets[(nq + 1) & 1]
    first = sets[nq & 1]
    start = n_far & 1
    pl.when(start == 1)(functools.partial(trip, 0, second, first))

    def pair(j, carry):
        kt = start + 2 * j
        trip(kt, first, second)
        trip(kt + 1, second, first)
        return carry

    lax.fori_loop(0, (n_far - start) // 2, pair, 0)

    def near_tile(near):
        cur = sets[(nq - near) & 1]
        kt = last - (nq - near)
        if near == 0:
            add_bias(cur, near)
        if near < nq:
            logits(kt + 1, sets[(nq - near - 1) & 1], near + 1)
        softmax(cur)
        values(kt - 1, sets[(nq - near + 1) & 1])

    pl.when(qt >= 1)(functools.partial(near_tile, 0))
    for near in range(1, nq + 1):
        near_tile(near)
    values(last, even_set)


def _flash_scratch(width):
    stats = [pltpu.VMEM((1, width), F32), pltpu.VMEM((1, width), F32), pltpu.VMEM((LANES, width), F32)]
    one_set = [pltpu.VMEM((TQ, width), F32), pltpu.VMEM((TQ, width), BF16), pltpu.VMEM((1, width), F32),
               pltpu.VMEM((1, width), F32)]
    return stats + 2 * one_set


def _flash_args(flash):
    return (tuple(flash[3:7]), tuple(flash[7:11])), flash[0], flash[1], flash[2]


def _half_rows(half, width):
    row = lax.broadcasted_iota(jnp.int32, (LANES, width), 0)
    return (row >= half * HEAD_DIM) & (row < (half + 1) * HEAD_DIM)


def _key_tile(ref, kt):
    return ref[0, pl.ds(pl.multiple_of(kt * TQ, TQ), TQ), :]


def _diff_kernel(out_scale, q_ref, k_ref, v_ref, t_ref, lam_ref, sub_ref, o_ref, qt_scr, *flash):
    qt = pl.program_id(2)
    q_t = (q_ref[0].astype(F32) * (DIFF_DIM ** -0.5 * LOG2E)).T
    for m in range(2):
        qt_scr[:, m * TQ:(m + 1) * TQ] = jnp.where(_half_rows(m, TQ), q_t, 0.0).astype(BF16)

    _flash_causal(
        qt, 1,
        lambda kt: _dot(_key_tile(k_ref, kt), qt_scr[...]),
        lambda kt: None,
        lambda near, b: _near_bias(t_ref, b, near, 0),
        lambda kt: _key_tile(v_ref, kt),
        *_flash_args(flash))

    o = flash[2][...] / flash[1][...]
    lam = lam_ref[...]
    lam_full = (jnp.exp(jnp.sum(lam[0:1] * lam[1:2], axis=1, keepdims=True))
                - jnp.exp(jnp.sum(lam[2:3] * lam[3:4], axis=1, keepdims=True)) + (1.0 - out_scale))
    od = (o[:, :TQ] - lam_full * o[:, TQ:]).T
    od = od * lax.rsqrt(jnp.mean(jnp.square(od), axis=-1, keepdims=True) + EPS) * sub_ref[...]
    o_ref[0] = (od * out_scale).astype(o_ref.dtype)


def _diff_call(qkv, t_tiles, lam, subln, lambda_init):
    bsz, s, _ = qkv.shape
    nh = DIFF_HEADS
    return pl.pallas_call(
        functools.partial(_diff_kernel, 1.0 - lambda_init),
        out_shape=jax.ShapeDtypeStruct((bsz, s, nh * LANES), BF16),
        grid=(bsz, nh, s // TQ),
        in_specs=[
            pl.BlockSpec((1, TQ, LANES), lambda b, h, i: (b, i, h)),
            pl.BlockSpec((1, s, LANES), lambda b, h, i: (b, 0, nh + h)),
            pl.BlockSpec((1, s, LANES), lambda b, h, i: (b, 0, 2 * nh + h)),
            pl.BlockSpec((2, TQ, 2 * TQ), lambda b, h, i: (0, 0, h)),
            _const_spec((4, DIFF_DIM)),
            _const_spec((1, LANES)),
        ],
        out_specs=pl.BlockSpec((1, TQ, LANES), lambda b, h, i: (b, i, h)),
        scratch_shapes=[pltpu.VMEM((LANES, 2 * TQ), BF16)] + _flash_scratch(2 * TQ),
        compiler_params=_params("parallel", "parallel", "arbitrary"),
        name="diff_attention",
    )(qkv, qkv, qkv, t_tiles, lam, subln)


def _compress_kernel(nc, xk_ref, xv_ref, pe_ref, w1_ref, w2_ref, kc_ref, vc_ref):
    half = CMP_LEN // 2
    for kv, (x_ref, o_ref) in enumerate(((xk_ref, kc_ref), (xv_ref, vc_ref))):
        first = jnp.zeros((nc, NSA_GROUPS * CMP_HIDDEN), F32)
        second = jnp.zeros_like(first)
        for l in range(half):
            x = x_ref[0, pl.ds(l, nc, stride=CMP_STRIDE), :]
            first += _dot((x + pe_ref[kv, l:l + 1]).astype(BF16), w1_ref[kv, l])
            second += _dot((x + pe_ref[kv, half + l:half + l + 1]).astype(BF16), w1_ref[kv, half + l])
        pre = first + pltpu.roll(second, nc - 1, 0)
        hid = jax.nn.gelu(pre, approximate=True)
        o_ref[0] = jnp.zeros(o_ref.shape[1:], o_ref.dtype)
        o_ref[0, CMP_PAD:CMP_PAD + nc, :] = _dot(hid.astype(BF16), w2_ref[kv]).astype(o_ref.dtype)


def _cmp_rows(s):
    nc = s // CMP_STRIDE
    return -(-(nc + LANES) // LANES) * LANES


def _compress_call(kvc, pe2, w1bd, w2bd):
    bsz, s, _ = kvc.shape
    nc = s // CMP_STRIDE
    rows = _cmp_rows(s)
    return pl.pallas_call(
        functools.partial(_compress_kernel, nc),
        out_shape=[jax.ShapeDtypeStruct((bsz, rows, LANES), BF16)] * 2,
        grid=(bsz,),
        in_specs=[
            pl.BlockSpec((1, s, LANES), lambda b: (b, 0, 0)),
            pl.BlockSpec((1, s, LANES), lambda b: (b, 0, 1)),
            _const_spec(pe2.shape),
            _const_spec(w1bd.shape),
            _const_spec(w2bd.shape),
        ],
        out_specs=[pl.BlockSpec((1, rows, LANES), lambda b: (b, 0, 0))] * 2,
        compiler_params=_params("parallel"),
        name="nsa_compress",
    )(kvc, kvc, pe2, w1bd, w2bd)


def _nsa_kernel(q_ref, gate_ref, kc_ref, vc_ref, ks_ref, vs_ref, kw_ref, vw_ref, imp_ref, eneg_ref, t_ref, tc_ref,
                o_ref, qpad_scr, lhs_scr, res_scr, *flash):
    m_scr, l_scr, acc_scr = flash[:3]
    qt = pl.program_id(1)
    q0 = qt * TQ
    n0 = pl.multiple_of(qt * (TQ // CMP_STRIDE), TQ // CMP_STRIDE)
    rows = kc_ref.shape[1]
    hq = NSA_HPG * TQ

    for c in range(NSA_HPG):
        q_t = (q_ref[0, :, c * LANES:(c + 1) * LANES].astype(F32) * (HEAD_DIM ** -0.5 * LOG2E)).T
        for g in range(NSA_GROUPS):
            qpad_scr[g, :, c * TQ:(c + 1) * TQ] = jnp.where(_half_rows(g, TQ), q_t, 0.0).astype(BF16)

    blk = lax.broadcasted_iota(jnp.int32, (LANES, TQ), 0)
    blkf = blk.astype(F32)
    cur = (q0 + lax.broadcasted_iota(jnp.int32, (LANES, TQ), 1)) // SEL_BLK
    allowed = blk <= cur
    forced = (blk == 0) | (blk == cur) | (blk == cur - 1)
    rowp = lax.broadcasted_iota(jnp.int32, (rows, TQ), 0)
    far_bias = jnp.where((rowp >= CMP_PAD) & (rowp < n0), 0.0, NEG_INF)
    near_bias = jnp.where((n0 + lax.broadcasted_iota(jnp.int32, (LANES, TQ), 0)) >= CMP_PAD, 0.0, NEG_INF)
    key_i = lax.broadcasted_iota(jnp.int32, (TQ, TQ), 0)
    qry_i = lax.broadcasted_iota(jnp.int32, (TQ, TQ), 1)
    win_edge = jnp.where(key_i > qry_i, 0.0, NEG_INF)

    for g in range(NSA_GROUPS):
        qp = qpad_scr[g]
        heads = slice(g * hq, (g + 1) * hq)
        kc_near = kc_ref[0, pl.ds(n0, LANES), :]
        vc_near = vc_ref[0, pl.ds(n0, LANES), :]
        s_far = _dot(kc_ref[0], qp)
        s_near = _dot(kc_near, qp) + tc_ref[:, heads]
        psum_far = jnp.zeros((rows, TQ), F32)
        psum_near = jnp.zeros((LANES, TQ), F32)
        p_far, p_near = [], []
        for c in range(NSA_HPG):
            sf = s_far[:, c * TQ:(c + 1) * TQ] + far_bias
            sn = s_near[:, c * TQ:(c + 1) * TQ] + near_bias
            mx = jnp.maximum(jnp.maximum(jnp.max(sf, axis=0, keepdims=True), jnp.max(sn, axis=0, keepdims=True)),
                             0.1 * NEG_INF)
            pf = jnp.exp2(sf - mx)
            pn = jnp.exp2(sn - mx)
            den = jnp.sum(pf, axis=0, keepdims=True) + jnp.sum(pn, axis=0, keepdims=True)
            inv = 1.0 / jnp.maximum(den, 1e-30)
            pf = pf * inv
            pn = pn * inv
            psum_far += pf
            psum_near += pn
            p_far.append(pf.astype(BF16))
            p_near.append(pn.astype(BF16))
        res_scr[0, g] = (_dot_tn(vc_ref[0], jnp.concatenate(p_far, axis=1))
                         + _dot_tn(vc_near, jnp.concatenate(p_near, axis=1)))
        score = jnp.zeros((LANES, TQ), F32)
        for psum, imp in ((psum_far, imp_ref[...]), (psum_near, imp_ref[pl.ds(n0, LANES), :])):
            hi = psum.astype(BF16)
            lo = (psum - hi.astype(F32)).astype(BF16)
            score += _dot_tn(imp, hi) + _dot_tn(imp, lo)
        score = jnp.where(forced, BIG, jnp.where(allowed, score, -BIG))
        unpicked = jnp.ones((LANES, TQ), F32)
        for _ in range(N_SEL):
            best = jnp.max(score, axis=0, keepdims=True)
            first = jnp.min(jnp.where(score == best, blkf, float(LANES)), axis=0, keepdims=True)
            pick = blkf == first
            unpicked = jnp.where(pick, 0.0, unpicked)
            score = jnp.where(pick, -jnp.inf, score)
        notsel = jnp.maximum(jnp.where(allowed, 0.0, 1.0), unpicked).astype(BF16)
        for c in range(NSA_HPG):
            lhs_scr[:LANES, c * TQ:(c + 1) * TQ] = notsel
        lhs_scr[LANES:, :] = qp

        def sel_logits(kt):
            k0 = pl.multiple_of(kt * TQ, TQ)
            return _dot(jnp.concatenate([eneg_ref[pl.ds(k0, TQ), :], _key_tile(ks_ref, kt)], axis=1), lhs_scr[...])

        _flash_causal(
            qt, 1, sel_logits,
            lambda kt: None,
            lambda near, b: _near_bias(t_ref, g * NSA_HPG + b, near, 0),
            lambda kt: _key_tile(vs_ref, kt),
            *_flash_args(flash))
        res_scr[1, g] = acc_scr[...] / l_scr[...]

        @pl.when(qt >= 2)
        def _():
            k0 = pl.multiple_of((qt - 2) * TQ, TQ)
            s = _dot(kw_ref[0, pl.ds(k0, 3 * TQ), :], qp)
            probs, sums = [], []
            for c in range(NSA_HPG):
                blk = slice(c * TQ, (c + 1) * TQ)
                hblk = slice((g * NSA_HPG + c) * TQ, (g * NSA_HPG + c + 1) * TQ)
                sb = s[:, blk] + jnp.concatenate([win_edge, t_ref[1, :, hblk], t_ref[0, :, hblk]], axis=0)
                p = jnp.exp2(sb - jnp.max(sb, axis=0, keepdims=True))
                sums.append(jnp.sum(p, axis=0, keepdims=True))
                probs.append(p.astype(BF16))
            out = _dot_tn(vw_ref[0, pl.ds(k0, 3 * TQ), :], jnp.concatenate(probs, axis=1))
            res_scr[2, g] = out / jnp.concatenate(sums, axis=1)

        @pl.when(qt < 2)
        def _():
            _flash_init(m_scr, l_scr, acc_scr)

            def win_step(kt, near):
                s = _dot(_key_tile(kw_ref, kt), qp) + t_ref[near, :, heads]
                _flash_update(s, _key_tile(vw_ref, kt), m_scr, l_scr, acc_scr)

            pl.when(qt >= 1)(functools.partial(win_step, qt - 1, 1))
            win_step(qt, 0)
            res_scr[2, g] = acc_scr[...] / l_scr[...]

    for c in range(NSA_HPG):
        out = jnp.zeros((TQ, LANES), F32)
        for br in range(3):
            gate = gate_ref[0, :, br * NSA_HEADS * HEAD_DIM + c * LANES:br * NSA_HEADS * HEAD_DIM + (c + 1) * LANES]
            both = jnp.concatenate([res_scr[br, 0, :HEAD_DIM, c * TQ:(c + 1) * TQ],
                                    res_scr[br, 1, HEAD_DIM:, c * TQ:(c + 1) * TQ]], axis=0)
            out += gate * both.T
        o_ref[0, :, c * LANES:(c + 1) * LANES] = out.astype(o_ref.dtype)


def _nsa_call(q, gates, kc, vc, kvs, kvw, imp, eneg, t_tiles, tc_tiles):
    bsz, s, _ = q.shape
    rows = kc.shape[1]
    hq = NSA_HPG * TQ
    whole = lambda col: pl.BlockSpec((1, s, LANES), lambda b, i: (b, 0, col))
    return pl.pallas_call(
        _nsa_kernel,
        out_shape=jax.ShapeDtypeStruct((bsz, s, NSA_HEADS * HEAD_DIM), BF16),
        grid=(bsz, s // TQ),
        in_specs=[
            pl.BlockSpec((1, TQ, NSA_HEADS * HEAD_DIM), lambda b, i: (b, i, 0)),
            pl.BlockSpec((1, TQ, 3 * NSA_HEADS * HEAD_DIM), lambda b, i: (b, i, 0)),
            pl.BlockSpec((1, rows, LANES), lambda b, i: (b, 0, 0)),
            pl.BlockSpec((1, rows, LANES), lambda b, i: (b, 0, 0)),
            whole(0), whole(1), whole(0), whole(1),
            _const_spec(imp.shape),
            _const_spec(eneg.shape),
            pl.BlockSpec((2, TQ, NSA_HEADS * TQ), lambda b, i: (0, 0, 0)),
            pl.BlockSpec((LANES, NSA_HEADS * TQ), lambda b, i: (0, 0)),
        ],
        out_specs=pl.BlockSpec((1, TQ, NSA_HEADS * HEAD_DIM), lambda b, i: (b, i, 0)),
        scratch_shapes=[
            pltpu.VMEM((NSA_GROUPS, LANES, hq), BF16),
            pltpu.VMEM((2 * LANES, hq), BF16),
            pltpu.VMEM((3, NSA_GROUPS, LANES, hq), F32),
        ] + _flash_scratch(hq),
        compiler_params=_params("parallel", "arbitrary"),
        name="nsa_attention",
    )(q, gates, kc, vc, kvs, kvs, kvw, kvw, imp, eneg, t_tiles, tc_tiles)


def _sort_key(x):
    b = lax.bitcast_convert_type(x, jnp.int32)
    return b ^ ((b >> 31) & jnp.int32(0x7FFFFFFF))


def _dsa_kernel(topk, dq_ref, ckv_ref, iq_ref, ik_ref, iw_ref, wuk_ref, wuv_ref, t_ref, o_ref,
                key_scr, hi_scr, low_scr, lhs_scr, thr_scr, *flash):
    qt = pl.program_id(1)
    nh = DSA_HEADS
    key_i = lax.broadcasted_iota(jnp.int32, (TQ, TQ), 0)
    qry_i = lax.broadcasted_iota(jnp.int32, (TQ, TQ), 1)
    int_min = jnp.int32(-2 ** 31)

    for c in range(IDX_HEADS // 2):
        col_t = iq_ref[0, :, c * LANES:(c + 1) * LANES].astype(F32).T.astype(BF16)
        for hl in range(2):
            j = 2 * c + hl
            lhs_scr[:IDX_DIM, j * TQ:(j + 1) * TQ] = col_t[hl * IDX_DIM:(hl + 1) * IDX_DIM]
    iw_t = iw_ref[0].T

    def idx_tile(kt, causal):
        s = _dot(_key_tile(ik_ref, kt)[:, :IDX_DIM], lhs_scr[:IDX_DIM, :])
        isc = jnp.zeros((TQ, TQ), F32)
        for j in range(IDX_HEADS):
            isc += iw_t[j:j + 1, :] * jnp.maximum(s[:, j * TQ:(j + 1) * TQ], 0.0)
        if causal:
            isc = jnp.where(key_i <= qry_i, isc, NEG_INF)
        k0 = pl.multiple_of(kt * TQ, TQ)
        zero_key = -2 - (k0 + key_i)
        key = jnp.where(jnp.abs(isc) < MIN_NORMAL, zero_key, _sort_key(isc))
        key_scr[pl.ds(k0, TQ), :] = key
        hi_scr[pl.ds(k0, TQ), :] = (key >> 16).astype(jnp.int16)

    pl.when((qt & 1) == 1)(functools.partial(idx_tile, 0, False))

    def idx_pair(j, carry):
        idx_tile((qt & 1) + 2 * j, False)
        idx_tile((qt & 1) + 2 * j + 1, False)
        return carry

    lax.fori_loop(0, qt // 2, idx_pair, 0)
    idx_tile(qt, True)

    @pl.when((qt & 1) == 0)
    def _():
        pad = pl.ds(pl.multiple_of((qt + 1) * TQ, TQ), TQ)
        key_scr[pad, :] = jnp.full((TQ, TQ), int_min, jnp.int32)
        hi_scr[pad, :] = jnp.full((TQ, TQ), -HALF_RANGE, jnp.int16)

    def slab(kp):
        return pl.ds(pl.multiple_of(kp * (2 * TQ), 2 * TQ), 2 * TQ)

    def count(pred_fn):
        def body(kp, cnt):
            hit = jnp.where(pred_fn(key_scr[slab(kp), :], kp * (2 * TQ)), 1.0, 0.0)
            return cnt + jnp.sum(hit.reshape(2 * TQ // COUNT_ROWS, COUNT_ROWS, TQ), axis=0)

        cnt = lax.fori_loop(0, (qt + 2) // 2, body, jnp.zeros((COUNT_ROWS, TQ), F32))
        return jnp.sum(cnt, axis=0, keepdims=True)

    def count_half(src_scr, cand):
        cand16 = cand.astype(jnp.int16)

        def body(kp, cnt):
            hit = jnp.where(src_scr[slab(kp), :] >= cand16, jnp.bfloat16(1), jnp.bfloat16(0))
            for i in range(2 * TQ // COUNT_ROWS):
                cnt = cnt + hit[i * COUNT_ROWS:(i + 1) * COUNT_ROWS]
            return cnt

        cnt = lax.fori_loop(0, (qt + 2) // 2, body, jnp.zeros((COUNT_ROWS, TQ), BF16))
        return jnp.sum(cnt.astype(F32), axis=0, keepdims=True)

    kf = float(topk)
    n_all = (qt + 1).astype(F32) * TQ
    n_pos = count_half(hi_scr, jnp.zeros((1, TQ), jnp.int32))
    hi0 = jnp.where(n_pos >= kf, 0, -HALF_RANGE)

    def hi_body(i, carry):
        hi, n_ge = carry
        cand = hi | (jnp.int32(1) << (14 - i))
        n_cand = count_half(hi_scr, cand)
        ok = n_cand >= kf
        return jnp.where(ok, cand, hi), jnp.where(ok, n_cand, n_ge)

    thr_hi, n_ge = lax.fori_loop(0, 15, hi_body, (hi0, jnp.where(n_pos >= kf, n_pos, n_all)))

    def remap(kp, carry):
        key = key_scr[slab(kp), :]
        hi = key >> 16
        low = (key & (2 * HALF_RANGE - 1)) - HALF_RANGE
        low = jnp.where(hi > thr_hi, HALF_RANGE - 1, jnp.where(hi < thr_hi, -HALF_RANGE, low))
        low_scr[slab(kp), :] = low.astype(jnp.int16)
        return carry

    lax.fori_loop(0, (qt + 2) // 2, remap, 0)

    def low_body(i, carry):
        low, n_ge = carry
        cand = low | (jnp.int32(1) << (15 - i))
        n_cand = count_half(low_scr, cand - HALF_RANGE)
        ok = n_cand >= kf
        return jnp.where(ok, cand, low), jnp.where(ok, n_cand, n_ge)

    thr_low, n_ge = lax.fori_loop(0, 16, low_body, (jnp.zeros((1, TQ), jnp.int32), n_ge))
    thr = (thr_hi << 16) | thr_low
    thr_scr[...] = thr

    excess = jnp.max(jnp.where(n_ge > kf, 1.0, 0.0))

    @pl.when(excess > 0.0)
    def _():
        need = kf - count(lambda tile, k0: tile > thr)
        pos_i = lax.broadcasted_iota(jnp.int32, (2 * TQ, TQ), 0)

        def pos_body(i, bound):
            cand = bound | (jnp.int32(1) << (30 - i))
            below = count(lambda tile, k0: (tile == thr) & (k0 + pos_i < cand))
            return jnp.where(below <= need - 1.0, cand, bound)

        bound = lax.fori_loop(0, 31, pos_body, jnp.zeros((1, TQ), jnp.int32))

        def demote(kt, carry):
            k0 = pl.multiple_of(kt * TQ, TQ)
            tile = key_scr[pl.ds(k0, TQ), :]
            key_scr[pl.ds(k0, TQ), :] = jnp.where((tile == thr) & (k0 + key_i > bound), tile - 1, tile)
            return carry

        lax.fori_loop(0, qt + 1, demote, 0)

    for c in range(nh // 2):
        col_t = dq_ref[0, :, c * LANES:(c + 1) * LANES].astype(F32).T
        for hl in range(2):
            h = 2 * c + hl
            qm = jnp.where(_half_rows(hl, TQ), col_t, 0.0).astype(BF16)
            lhs_scr[:, h * TQ:(h + 1) * TQ] = (_dot(wuk_ref[c], qm) * (HEAD_DIM ** -0.5 * LOG2E)).astype(BF16)

    _flash_causal(
        qt, 1,
        lambda kt: _dot(_key_tile(ckv_ref, kt), lhs_scr[...]),
        lambda kt: jnp.where(key_scr[pl.ds(pl.multiple_of(kt * TQ, TQ), TQ), :] >= thr_scr[...], 0.0, NEG_INF),
        lambda near, b: _near_bias(t_ref, b, near, 0),
        lambda kt: _key_tile(ckv_ref, kt),
        *_flash_args(flash))

    o_lat = (flash[2][...] / flash[1][...]).astype(BF16)
    for c in range(nh // 2):
        out = (_dot(wuv_ref[2 * c], o_lat[:, (2 * c) * TQ:(2 * c + 1) * TQ])
               + _dot(wuv_ref[2 * c + 1], o_lat[:, (2 * c + 1) * TQ:(2 * c + 2) * TQ]))
        o_ref[0, :, c * LANES:(c + 1) * LANES] = out.T.astype(o_ref.dtype)


def _dsa_call(dq, ckv, iq, ik2, iw, wuk, wuv, t_tiles, topk):
    bsz, s, _ = dq.shape
    nh = DSA_HEADS
    hq = nh * TQ
    return pl.pallas_call(
        functools.partial(_dsa_kernel, topk),
        out_shape=jax.ShapeDtypeStruct((bsz, s, nh * HEAD_DIM), BF16),
        grid=(bsz, s // TQ),
        in_specs=[
            pl.BlockSpec((1, TQ, nh * HEAD_DIM), lambda b, i: (b, i, 0)),
            pl.BlockSpec((1, s, LANES), lambda b, i: (b, 0, 0)),
            pl.BlockSpec((1, TQ, IDX_HEADS * IDX_DIM), lambda b, i: (b, i, 0)),
            pl.BlockSpec((1, s, LANES), lambda b, i: (b, 0, 0)),
            pl.BlockSpec((1, TQ, LANES), lambda b, i: (b, i, 0)),
            _const_spec(wuk.shape),
            _const_spec(wuv.shape),
            pl.BlockSpec((2, TQ, hq), lambda b, i: (0, 0, 1)),
        ],
        out_specs=pl.BlockSpec((1, TQ, nh * HEAD_DIM), lambda b, i: (b, i, 0)),
        scratch_shapes=[
            pltpu.VMEM((s + TQ, TQ), jnp.int32),
            pltpu.VMEM((s + TQ, TQ), jnp.int16),
            pltpu.VMEM((s + TQ, TQ), jnp.int16),
            pltpu.VMEM((LANES, hq), BF16),
            pltpu.VMEM((1, TQ), jnp.int32),
        ] + _flash_scratch(hq),
        compiler_params=_params("parallel", "arbitrary"),
        name="dsa_attention",
    )(dq, ckv, iq, ik2, iw, wuk, wuv, t_tiles)


_NSA_Q = NSA_HEADS * HEAD_DIM
_NSA_KV = 2 * NSA_GROUPS * HEAD_DIM
_EVEN_SIZES = (_NSA_Q, _NSA_KV, _NSA_KV, _NSA_KV, NSA_HEADS * 3, DSA_HEADS * HEAD_DIM, DSA_KV_RANK,
               IDX_HEADS * IDX_DIM, IDX_DIM, IDX_HEADS)
_EVEN_OFF = np.concatenate([[0], np.cumsum(_EVEN_SIZES)])
_ZERO_COL = int(_EVEN_OFF[-1])


def _nsa_head_perm():
    pos = np.zeros(_NSA_Q, np.int64)
    for c in range(NSA_HPG):
        for g in range(NSA_GROUPS):
            for d in range(HEAD_DIM):
                pos[c * LANES + g * HEAD_DIM + d] = (g * NSA_HPG + c) * HEAD_DIM + d
    return pos


def _even_columns():
    o = _EVEN_OFF
    perm = _nsa_head_perm()
    gate = np.concatenate([o[4] + (perm // HEAD_DIM) * 3 + br for br in range(3)])
    idx_w = np.concatenate([np.arange(o[9], o[10]), np.full(LANES - IDX_HEADS, _ZERO_COL)])
    cols = [
        o[0] + perm,
        np.arange(o[1], o[2]), np.arange(o[2], o[3]), np.arange(o[3], o[4]),
        gate,
        np.arange(o[5], o[6]), np.arange(o[6], o[7]), np.arange(o[7], o[8]),
        np.concatenate([np.arange(o[8], o[9])] * 2),
        idx_w,
    ]
    segs = (
        (_NSA_Q, BF16, None), (_NSA_KV, F32, None), (_NSA_KV, BF16, None), (_NSA_KV, BF16, None),
        (3 * _NSA_Q, F32, "sigmoid"),
        (DSA_HEADS * HEAD_DIM, BF16, None), (DSA_KV_RANK, BF16, "rms"), (IDX_HEADS * IDX_DIM, BF16, None),
        (LANES, BF16, None), (LANES, F32, float(IDX_HEADS ** -0.5 * IDX_DIM ** -0.5)),
    )
    return np.concatenate(cols).astype(np.int32), segs


def _block_diag2(w):
    z = jnp.zeros_like(w)
    return jnp.concatenate([jnp.concatenate([w, z], axis=-1), jnp.concatenate([z, w], axis=-1)], axis=-2)


def _selection_constants(s):
    nc = s // CMP_STRIDE
    nb = s // SEL_BLK
    rows = _cmp_rows(s)
    cs = np.arange(nc) * CMP_STRIDE
    ss = np.arange(nb) * SEL_BLK
    overlap = np.clip(np.minimum(cs[:, None] + CMP_LEN, ss[None, :] + SEL_BLK) - np.maximum(cs[:, None], ss[None, :]), 0, None)
    imp = np.zeros((rows, LANES), np.float32)
    imp[CMP_PAD:CMP_PAD + nc, :nb] = overlap / CMP_STRIDE
    imp[CMP_PAD + nc - 1] = 0.0
    eneg = np.zeros((s, LANES), np.float32)
    eneg[np.arange(s), np.arange(s) // SEL_BLK] = NEG_INF
    return jnp.asarray(imp, BF16), jnp.asarray(eneg, BF16)


def kernel(x, c, rel_bias, ada_w, ada_b, ln_g, ln_b, ev_w_in, ev_w_out, nsa_pe_k, nsa_pe_v, nsa_w1_k, nsa_w2_k, nsa_w1_v, nsa_w2_v, dsa_kv_norm, dsa_w_uk, dsa_w_uv, od_w_in, od_w_out, diff_lam, diff_subln, mlp_w1, mlp_w2):
    bsz, s, d = x.shape
    depth = ada_w.shape[0]
    assert s % TQ == 0 and s // SEL_BLK <= LANES
    alpha = (2 * depth) ** 0.25
    topk = min(DSA_TOPK, s // 4)

    ada = _ada_call(c, ada_w, ada_b)
    t_tiles, tc_tiles = _bias_call(rel_bias)
    imp, eneg = _selection_constants(s)
    even_cols, even_segs = _even_columns()
    perm = _nsa_head_perm()
    odd_segs = ((od_w_in.shape[-1], BF16, None),)
    unit_gain = jnp.ones((1, LANES), F32)

    for l in range(depth):
        sh1, sc1, g1, sh2, sc2, g2 = [a.reshape(bsz, 1, d) for a in jnp.split(ada[l], 6, axis=-1)]
        i = l // 2
        if l % 2 == 0:
            w_aug = jnp.concatenate([ev_w_in[i], jnp.zeros((d, 1), F32)], axis=1)
            w_in = jnp.take(w_aug, even_cols, axis=1).astype(BF16)
            nsa_q, kvc, kvs, kvw, gates, dsa_q, ckv, idx_q, ik2, idx_w = _proj_call(
                x, sh1, sc1, w_in, dsa_kv_norm[i].reshape(1, LANES), even_segs)
            pe2 = jnp.stack([jnp.tile(nsa_pe_k[i], (1, 2)), jnp.tile(nsa_pe_v[i], (1, 2))])
            w1bd = jnp.stack([_block_diag2(w.reshape(CMP_LEN, HEAD_DIM, CMP_HIDDEN)) for w in (nsa_w1_k[i], nsa_w1_v[i])]).astype(BF16)
            w2bd = jnp.stack([_block_diag2(nsa_w2_k[i]), _block_diag2(nsa_w2_v[i])]).astype(BF16)
            kc, vc = _compress_call(kvc, pe2, w1bd, w2bd)
            o_a = _nsa_call(nsa_q, gates, kc, vc, kvs, kvw, imp, eneg, t_tiles, tc_tiles)
            wuk = dsa_w_uk[i].reshape(DSA_KV_RANK, DSA_HEADS // 2, LANES).transpose(1, 0, 2).astype(BF16)
            wuv_h = dsa_w_uv[i].transpose(1, 0, 2)
            zeros = jnp.zeros_like(wuv_h)
            even_h = (np.arange(DSA_HEADS) % 2 == 0)[:, None, None]
            wuv = jnp.where(even_h, jnp.concatenate([wuv_h, zeros], -1), jnp.concatenate([zeros, wuv_h], -1))
            wuv = wuv.transpose(0, 2, 1).astype(BF16)
            o_b = _dsa_call(dsa_q, ckv, idx_q, ik2, idx_w, wuk, wuv, t_tiles, topk)
            attn = jnp.concatenate([o_a, o_b], axis=-1)
            w_out = jnp.concatenate([ev_w_out[i][perm], ev_w_out[i][_NSA_Q:]], axis=0).astype(BF16)
        else:
            lambda_init = 0.8 - 0.6 * math.exp(-0.3 * l)
            (qkv,) = _proj_call(x, sh1, sc1, od_w_in[i].astype(BF16), unit_gain, odd_segs)
            attn = _diff_call(qkv, t_tiles, diff_lam[i], diff_subln[i].reshape(1, LANES), lambda_init)
            w_out = od_w_out[i].astype(BF16)
        x = _out_call(x, attn, w_out, g1, ln_g[l, 0].reshape(1, d), ln_b[l, 0].reshape(1, d), alpha)
        x = _mlp_call(x, sh2, sc2, g2, mlp_w1[l].astype(BF16), mlp_w2[l].astype(BF16),
                      ln_g[l, 1].reshape(1, d), ln_b[l, 1].reshape(1, d), alpha)
    return x
```

```python
import functools
import math

import jax
import jax.numpy as jnp
import numpy as np
from jax import lax
from jax.experimental import pallas as pl
from jax.experimental.pallas import tpu as pltpu

HEAD_DIM = 64
NSA_HEADS = 8
NSA_GROUPS = 2
NSA_HPG = NSA_HEADS // NSA_GROUPS
CMP_LEN = 32
CMP_STRIDE = 16
CMP_HIDDEN = 256
SEL_BLK = 64
N_SEL = 16
WINDOW = 512
DSA_HEADS = 8
DSA_KV_RANK = 128
IDX_HEADS = 8
IDX_DIM = 64
DSA_TOPK = 256
DIFF_HEADS = 8
DIFF_DIM = 64
NUM_BUCKETS = 32
MAX_DISTANCE = 128
NEG_INF = -1e30
BIG = 1e9
EPS = 1e-5

LANES = 128
TQ = 256
COUNT_ROWS = 64
MIN_NORMAL = float(np.finfo(np.float32).tiny)
HALF_RANGE = 1 << 15
CMP_PAD = 16
CMP_WIN = 32
VMEM_LIMIT = 56 * 1024 * 1024
LOG2E = math.log2(math.e)

F32 = jnp.float32
BF16 = jnp.bfloat16


def _bucket_table():
    n = np.arange(MAX_DISTANCE)
    exact = NUM_BUCKETS // 2
    nf = np.maximum(n, 1).astype(np.float64)
    large = exact + (np.log(nf / exact) / math.log(MAX_DISTANCE / exact) * (NUM_BUCKETS - exact)).astype(np.int64)
    return np.where(n < exact, n, np.minimum(large, NUM_BUCKETS - 1)).astype(np.int32)


def _bucket_starts():
    table = _bucket_table()
    return [(int(b), int(np.argmax(table == b))) for b in sorted(set(table.tolist()))]


def _dot(a, b):
    return jnp.dot(a, b, preferred_element_type=F32)


def _dot_tn(a, b):
    return lax.dot_general(a, b, (((0,), (0,)), ((), ())), preferred_element_type=F32)


def _params(*sem):
    return pltpu.CompilerParams(dimension_semantics=sem, vmem_limit_bytes=VMEM_LIMIT)


def _const_spec(shape):
    nd = len(shape)
    return pl.BlockSpec(shape, lambda *_: (0,) * nd)


def _ada_kernel(c_ref, w_ref, b_ref, o_ref):
    c = c_ref[...]
    ca = c * jax.nn.sigmoid(c)
    o_ref[0] = _dot(ca, w_ref[0]) + b_ref[0]


def _ada_call(c, ada_w, ada_b):
    depth, d, n = ada_w.shape
    bsz = c.shape[0]
    rows = 8
    cp = jnp.pad(c, ((0, rows - bsz), (0, 0)))
    tn = 1536
    out = pl.pallas_call(
        _ada_kernel,
        out_shape=jax.ShapeDtypeStruct((depth, rows, n), F32),
        grid=(depth, n // tn),
        in_specs=[
            pl.BlockSpec((rows, d), lambda l, j: (0, 0)),
            pl.BlockSpec((1, d, tn), lambda l, j: (l, 0, j)),
            pl.BlockSpec((1, 1, tn), lambda l, j: (l, 0, j)),
        ],
        out_specs=pl.BlockSpec((1, rows, tn), lambda l, j: (l, 0, j)),
        compiler_params=_params("arbitrary", "arbitrary"),
        name="ada_table",
    )(cp, ada_w, ada_b.reshape(depth, 1, n))
    return out[:, :bsz]


def _proj_kernel(segs, x_ref, sh_ref, sc_ref, w_ref, gain_ref, *o_refs):
    h = (x_ref[0] * (1.0 + sc_ref[0]) + sh_ref[0]).astype(BF16)
    off = 0
    for (width, _, epi), o_ref in zip(segs, o_refs):
        y = _dot(h, w_ref[:, off:off + width])
        if epi == "sigmoid":
            y = jax.nn.sigmoid(y)
        elif epi == "rms":
            y = y * lax.rsqrt(jnp.mean(jnp.square(y), axis=-1, keepdims=True) + EPS) * gain_ref[...]
        elif isinstance(epi, float):
            y = y * epi
        o_ref[0] = y.astype(o_ref.dtype)
        off += width


def _proj_call(x, sh, sc, w, gain, segs, tm=512):
    bsz, s, d = x.shape
    ntot = w.shape[1]
    return pl.pallas_call(
        functools.partial(_proj_kernel, segs),
        out_shape=[jax.ShapeDtypeStruct((bsz, s, width), dt) for width, dt, _ in segs],
        grid=(bsz, s // tm),
        in_specs=[
            pl.BlockSpec((1, tm, d), lambda b, i: (b, i, 0)),
            pl.BlockSpec((1, 1, d), lambda b, i: (b, 0, 0)),
            pl.BlockSpec((1, 1, d), lambda b, i: (b, 0, 0)),
            _const_spec((d, ntot)),
            _const_spec((1, LANES)),
        ],
        out_specs=[pl.BlockSpec((1, tm, width), lambda b, i: (b, i, 0)) for width, _, _ in segs],
        compiler_params=_params("parallel", "parallel"),
        name="modulate_in_proj",
    )(x, sh, sc, w, gain)


def _layer_norm(z, g, b):
    mu = jnp.mean(z, axis=-1, keepdims=True)
    zc = z - mu
    var = jnp.mean(jnp.square(zc), axis=-1, keepdims=True)
    return zc * lax.rsqrt(var + EPS) * g + b


def _out_kernel(alpha, x_ref, a_ref, w_ref, g1_ref, lg_ref, lb_ref, o_ref):
    y = _dot(a_ref[0], w_ref[...])
    z = alpha * x_ref[0] + (1.0 + g1_ref[0]) * y
    o_ref[0] = _layer_norm(z, lg_ref[...], lb_ref[...])


def _out_call(x, attn, w_out, g1, ln_g, ln_b, alpha, tm=512):
    bsz, s, d = x.shape
    k = attn.shape[-1]
    return pl.pallas_call(
        functools.partial(_out_kernel, alpha),
        out_shape=jax.ShapeDtypeStruct((bsz, s, d), F32),
        grid=(bsz, s // tm),
        in_specs=[
            pl.BlockSpec((1, tm, d), lambda b, i: (b, i, 0)),
            pl.BlockSpec((1, tm, k), lambda b, i: (b, i, 0)),
            _const_spec((k, d)),
            pl.BlockSpec((1, 1, d), lambda b, i: (b, 0, 0)),
            _const_spec((1, d)),
            _const_spec((1, d)),
        ],
        out_specs=pl.BlockSpec((1, tm, d), lambda b, i: (b, i, 0)),
        compiler_params=_params("parallel", "parallel"),
        name="out_proj_ln",
    )(x, attn, w_out, g1, ln_g, ln_b)


def _mlp_kernel(alpha, x_ref, sh_ref, sc_ref, g2_ref, w1_ref, w2_ref, lg_ref, lb_ref, o_ref, h_scr, acc_scr):
    f = pl.program_id(2)

    @pl.when(f == 0)
    def _():
        h_scr[...] = (x_ref[0] * (1.0 + sc_ref[0]) + sh_ref[0]).astype(BF16)
        acc_scr[...] = jnp.zeros_like(acc_scr)

    a = jnp.maximum(_dot(h_scr[...], w1_ref[...]), 0.0)
    acc_scr[...] += _dot((a * a).astype(BF16), w2_ref[...])

    @pl.when(f == pl.num_programs(2) - 1)
    def _():
        z = alpha * x_ref[0] + (1.0 + g2_ref[0]) * acc_scr[...]
        o_ref[0] = _layer_norm(z, lg_ref[...], lb_ref[...])


def _mlp_call(x, sh, sc, g2, w1, w2, ln_g, ln_b, alpha, tm=1024, tf=2048):
    bsz, s, d = x.shape
    ff = w1.shape[1]
    tm = min(tm, s)
    assert s % tm == 0
    return pl.pallas_call(
        functools.partial(_mlp_kernel, alpha),
        out_shape=jax.ShapeDtypeStruct((bsz, s, d), F32),
        grid=(bsz, s // tm, ff // tf),
        in_specs=[
            pl.BlockSpec((1, tm, d), lambda b, i, f: (b, i, 0)),
            pl.BlockSpec((1, 1, d), lambda b, i, f: (b, 0, 0)),
            pl.BlockSpec((1, 1, d), lambda b, i, f: (b, 0, 0)),
            pl.BlockSpec((1, 1, d), lambda b, i, f: (b, 0, 0)),
            pl.BlockSpec((d, tf), lambda b, i, f: (0, f)),
            pl.BlockSpec((tf, d), lambda b, i, f: (f, 0)),
            _const_spec((1, d)),
            _const_spec((1, d)),
        ],
        out_specs=pl.BlockSpec((1, tm, d), lambda b, i, f: (b, i, 0)),
        scratch_shapes=[pltpu.VMEM((tm, d), BF16), pltpu.VMEM((tm, d), F32)],
        compiler_params=_params("parallel", "parallel", "arbitrary"),
        name="mlp_ln",
    )(x, sh, sc, g2, w1, w2, ln_g, ln_b)


def _bias_kernel(tab_ref, t_ref, tc_ref):
    h = pl.program_id(0)
    key = lax.broadcasted_iota(jnp.int32, (TQ, TQ), 0)
    qry = lax.broadcasted_iota(jnp.int32, (TQ, TQ), 1)
    win = lax.broadcasted_iota(jnp.int32, (LANES, TQ), 0)
    qry_c = lax.broadcasted_iota(jnp.int32, (LANES, TQ), 1)
    dists = (qry - key, TQ + qry - key, qry_c - CMP_STRIDE * (win - CMP_PAD) - (CMP_LEN - 1))

    accs = [jnp.zeros(dist.shape, F32) for dist in dists]
    for bucket, first in _bucket_starts():
        val = tab_ref[h * NUM_BUCKETS + bucket] * LOG2E
        accs = [jnp.where(dist >= first, val, acc) for dist, acc in zip(dists, accs)]
    accs = [jnp.where(dist >= MAX_DISTANCE, 0.0, acc) for dist, acc in zip(dists, accs)]
    t_ref[0] = jnp.where(dists[0] < 0, NEG_INF, accs[0])
    t_ref[1] = accs[1]
    tc_ref[...] = jnp.where((dists[2] < 0) | (win >= CMP_WIN), NEG_INF, accs[2])


def _bias_call(rel_bias):
    nh = rel_bias.shape[1]
    tab = (rel_bias - rel_bias[NUM_BUCKETS - 1]).T.reshape(-1)
    return pl.pallas_call(
        _bias_kernel,
        out_shape=[
            jax.ShapeDtypeStruct((2, TQ, nh * TQ), F32),
            jax.ShapeDtypeStruct((LANES, nh * TQ), F32),
        ],
        grid=(nh,),
        in_specs=[pl.BlockSpec(memory_space=pltpu.SMEM)],
        out_specs=[
            pl.BlockSpec((2, TQ, TQ), lambda h: (0, 0, h)),
            pl.BlockSpec((LANES, TQ), lambda h: (0, h)),
        ],
        compiler_params=_params("arbitrary"),
        name="t5_bias_tiles",
    )(tab)


def _flash_init(m_ref, l_ref, acc_ref):
    m_ref[...] = jnp.full(m_ref.shape, NEG_INF, F32)
    l_ref[...] = jnp.zeros(l_ref.shape, F32)
    acc_ref[...] = jnp.zeros(acc_ref.shape, F32)


def _flash_update(s, v_tile, m_ref, l_ref, acc_ref):
    m_prev = m_ref[...]
    m_new = jnp.maximum(m_prev, jnp.max(s, axis=0, keepdims=True))
    alpha = jnp.exp2(m_prev - m_new)
    p = jnp.exp2(s - m_new)
    l_ref[...] = alpha * l_ref[...] + jnp.sum(p, axis=0, keepdims=True)
    acc_ref[...] = alpha * acc_ref[...] + _dot_tn(v_tile, p.astype(BF16))
    m_ref[...] = m_new


def _near_bias(t_ref, head, near, u):
    d = u - (near - 1)
    if d < 0:
        return NEG_INF
    if d >= 2:
        return None
    return t_ref[d, :, head * TQ:(head + 1) * TQ]


def _flash_causal(qt, nq, qk_fn, mask_fn, bias_fn, v_fn, bufs, m_ref, l_ref, acc_ref):
    even_set, odd_set = bufs
    sets = (even_set, odd_set)
    width = even_set[0].shape[1]
    blocks = [slice(b * TQ, (b + 1) * TQ) for b in range(width // TQ)]
    last = qt * nq + nq - 1
    n_far = jnp.maximum(qt * nq - 1, 0)
    _flash_init(m_ref, l_ref, acc_ref)
    for _, p_buf, a_buf, _ in bufs:
        p_buf[...] = jnp.zeros(p_buf.shape, p_buf.dtype)
        a_buf[...] = jnp.ones(a_buf.shape, F32)

    def logits(kt, dst, near):
        s = qk_fn(kt)
        mask = mask_fn(kt)
        tops = []
        for b, blk in enumerate(blocks):
            sb = s[:, blk]
            if mask is not None:
                sb = sb + mask
            bias = None if near is None else bias_fn(near, b)
            if bias is not None:
                sb = sb + bias
            dst[0][:, blk] = sb
            tops.append(jnp.max(sb, axis=0, keepdims=True))
        dst[3][...] = jnp.concatenate(tops, axis=1)

    def add_bias(cur, near):
        tops = []
        for b, blk in enumerate(blocks):
            bias = bias_fn(near, b)
            if bias is None:
                tops.append(cur[3][:, blk])
                continue
            sb = cur[0][:, blk] + bias
            cur[0][:, blk] = sb
            tops.append(jnp.max(sb, axis=0, keepdims=True))
        cur[3][...] = jnp.concatenate(tops, axis=1)

    def softmax(cur):
        m_prev = m_ref[...]
        m_new = jnp.maximum(m_prev, cur[3][...])
        alpha = jnp.exp2(m_prev - m_new)
        sums = []
        for blk in blocks:
            p = jnp.exp2(cur[0][:, blk] - m_new[:, blk])
            sums.append(jnp.sum(p, axis=0, keepdims=True))
            cur[1][:, blk] = p.astype(BF16)
        l_ref[...] = alpha * l_ref[...] + jnp.concatenate(sums, axis=1)
        m_ref[...] = m_new
        cur[2][...] = alpha

    def values(kt, src):
        acc_ref[...] = src[2][...] * acc_ref[...] + _dot_tn(v_fn(jnp.maximum(kt, 0)), src[1][...])

    def trip(kt, cur, nxt):
        logits(kt + 1, nxt, None)
        softmax(cur)
        values(kt - 1, nxt)

    first_is_even = (last & 1) == 0
    pl.when(qt == 0)(functools.partial(logits, 0, sets[(nq - 1) & 1], 1))
    pl.when((qt >= 1) & first_is_even)(functools.partial(logits, 0, even_set, None))
    pl.when((qt >= 1) & jnp.logical_not(first_is_even))(functools.partial(logits, 0, odd_set, None))

    second = sets[(nq + 1) & 1]
    first = sets[nq & 1]
    start = n_far & 1
    pl.when(start == 1)(functools.partial(trip, 0, second, first))

    def pair(j, carry):
        kt = start + 2 * j
        trip(kt, first, second)
        trip(kt + 1, second, first)
        return carry

    lax.fori_loop(0, (n_far - start) // 2, pair, 0)

    def near_tile(near):
        cur = sets[(nq - near) & 1]
        kt = last - (nq - near)
        if near == 0:
            add_bias(cur, near)
        if near < nq:
            logits(kt + 1, sets[(nq - near - 1) & 1], near + 1)
        softmax(cur)
        values(kt - 1, sets[(nq - near + 1) & 1])

    pl.when(qt >= 1)(functools.partial(near_tile, 0))
    for near in range(1, nq + 1):
        near_tile(near)
    values(last, even_set)


def _flash_scratch(width):
    stats = [pltpu.VMEM((1, width), F32), pltpu.VMEM((1, width), F32), pltpu.VMEM((LANES, width), F32)]
    one_set = [pltpu.VMEM((TQ, width), F32), pltpu.VMEM((TQ, width), BF16), pltpu.VMEM((1, width), F32),
               pltpu.VMEM((1, width), F32)]
    return stats + 2 * one_set


def _flash_args(flash):
    return (tuple(flash[3:7]), tuple(flash[7:11])), flash[0], flash[1], flash[2]


def _half_rows(half, width):
    row = lax.broadcasted_iota(jnp.int32, (LANES, width), 0)
    return (row >= half * HEAD_DIM) & (row < (half + 1) * HEAD_DIM)


def _key_tile(ref, kt):
    return ref[0, pl.ds(pl.multiple_of(kt * TQ, TQ), TQ), :]


def _diff_kernel(out_scale, q_ref, k_ref, v_ref, t_ref, lam_ref, sub_ref, o_ref, qt_scr, *flash):
    qt = pl.program_id(2)
    q_t = (q_ref[0].astype(F32) * (DIFF_DIM ** -0.5 * LOG2E)).T
    for m in range(2):
        qt_scr[:, m * TQ:(m + 1) * TQ] = jnp.where(_half_rows(m, TQ), q_t, 0.0).astype(BF16)

    _flash_causal(
        qt, 1,
        lambda kt: _dot(_key_tile(k_ref, kt), qt_scr[...]),
        lambda kt: None,
        lambda near, b: _near_bias(t_ref, b, near, 0),
        lambda kt: _key_tile(v_ref, kt),
        *_flash_args(flash))

    o = flash[2][...] / flash[1][...]
    lam = lam_ref[...]
    lam_full = (jnp.exp(jnp.sum(lam[0:1] * lam[1:2], axis=1, keepdims=True))
                - jnp.exp(jnp.sum(lam[2:3] * lam[3:4], axis=1, keepdims=True)) + (1.0 - out_scale))
    od = (o[:, :TQ] - lam_full * o[:, TQ:]).T
    od = od * lax.rsqrt(jnp.mean(jnp.square(od), axis=-1, keepdims=True) + EPS) * sub_ref[...]
    o_ref[0] = (od * out_scale).astype(o_ref.dtype)


def _diff_call(qkv, t_tiles, lam, subln, lambda_init):
    bsz, s, _ = qkv.shape
    nh = DIFF_HEADS
    return pl.pallas_call(
        functools.partial(_diff_kernel, 1.0 - lambda_init),
        out_shape=jax.ShapeDtypeStruct((bsz, s, nh * LANES), BF16),
        grid=(bsz, nh, s // TQ),
        in_specs=[
            pl.BlockSpec((1, TQ, LANES), lambda b, h, i: (b, i, h)),
            pl.BlockSpec((1, s, LANES), lambda b, h, i: (b, 0, nh + h)),
            pl.BlockSpec((1, s, LANES), lambda b, h, i: (b, 0, 2 * nh + h)),
            pl.BlockSpec((2, TQ, 2 * TQ), lambda b, h, i: (0, 0, h)),
            _const_spec((4, DIFF_DIM)),
            _const_spec((1, LANES)),
        ],
        out_specs=pl.BlockSpec((1, TQ, LANES), lambda b, h, i: (b, i, h)),
        scratch_shapes=[pltpu.VMEM((LANES, 2 * TQ), BF16)] + _flash_scratch(2 * TQ),
        compiler_params=_params("parallel", "parallel", "arbitrary"),
        name="diff_attention",
    )(qkv, qkv, qkv, t_tiles, lam, subln)


def _compress_kernel(nc, xk_ref, xv_ref, pe_ref, w1_ref, w2_ref, kc_ref, vc_ref):
    half = CMP_LEN // 2
    for kv, (x_ref, o_ref) in enumerate(((xk_ref, kc_ref), (xv_ref, vc_ref))):
        first = jnp.zeros((nc, NSA_GROUPS * CMP_HIDDEN), F32)
        second = jnp.zeros_like(first)
        for l in range(half):
            x = x_ref[0, pl.ds(l, nc, stride=CMP_STRIDE), :]
            first += _dot((x + pe_ref[kv, l:l + 1]).astype(BF16), w1_ref[kv, l])
            second += _dot((x + pe_ref[kv, half + l:half + l + 1]).astype(BF16), w1_ref[kv, half + l])
        pre = first + pltpu.roll(second, nc - 1, 0)
        hid = jax.nn.gelu(pre, approximate=True)
        o_ref[0] = jnp.zeros(o_ref.shape[1:], o_ref.dtype)
        o_ref[0, CMP_PAD:CMP_PAD + nc, :] = _dot(hid.astype(BF16), w2_ref[kv]).astype(o_ref.dtype)


def _cmp_rows(s):
    nc = s // CMP_STRIDE
    return -(-(nc + LANES) // LANES) * LANES


def _compress_call(kvc, pe2, w1bd, w2bd):
    bsz, s, _ = kvc.shape
    nc = s // CMP_STRIDE
    rows = _cmp_rows(s)
    return pl.pallas_call(
        functools.partial(_compress_kernel, nc),
        out_shape=[jax.ShapeDtypeStruct((bsz, rows, LANES), BF16)] * 2,
        grid=(bsz,),
        in_specs=[
            pl.BlockSpec((1, s, LANES), lambda b: (b, 0, 0)),
            pl.BlockSpec((1, s, LANES), lambda b: (b, 0, 1)),
            _const_spec(pe2.shape),
            _const_spec(w1bd.shape),
            _const_spec(w2bd.shape),
        ],
        out_specs=[pl.BlockSpec((1, rows, LANES), lambda b: (b, 0, 0))] * 2,
        compiler_params=_params("parallel"),
        name="nsa_compress",
    )(kvc, kvc, pe2, w1bd, w2bd)


def _nsa_kernel(q_ref, gate_ref, kc_ref, vc_ref, ks_ref, vs_ref, kw_ref, vw_ref, imp_ref, eneg_ref, t_ref, tc_ref,
                o_ref, qpad_scr, lhs_scr, res_scr, *flash):
    m_scr, l_scr, acc_scr = flash[:3]
    qt = pl.program_id(1)
    q0 = qt * TQ
    n0 = pl.multiple_of(qt * (TQ // CMP_STRIDE), TQ // CMP_STRIDE)
    rows = kc_ref.shape[1]
    hq = NSA_HPG * TQ

    for c in range(NSA_HPG):
        q_t = (q_ref[0, :, c * LANES:(c + 1) * LANES].astype(F32) * (HEAD_DIM ** -0.5 * LOG2E)).T
        for g in range(NSA_GROUPS):
            qpad_scr[g, :, c * TQ:(c + 1) * TQ] = jnp.where(_half_rows(g, TQ), q_t, 0.0).astype(BF16)

    blk = lax.broadcasted_iota(jnp.int32, (LANES, TQ), 0)
    blkf = blk.astype(F32)
    cur = (q0 + lax.broadcasted_iota(jnp.int32, (LANES, TQ), 1)) // SEL_BLK
    allowed = blk <= cur
    forced = (blk == 0) | (blk == cur) | (blk == cur - 1)
    rowp = lax.broadcasted_iota(jnp.int32, (rows, TQ), 0)
    far_bias = jnp.where((rowp >= CMP_PAD) & (rowp < n0), 0.0, NEG_INF)
    near_bias = jnp.where((n0 + lax.broadcasted_iota(jnp.int32, (LANES, TQ), 0)) >= CMP_PAD, 0.0, NEG_INF)
    key_i = lax.broadcasted_iota(jnp.int32, (TQ, TQ), 0)
    qry_i = lax.broadcasted_iota(jnp.int32, (TQ, TQ), 1)
    win_edge = jnp.where(key_i > qry_i, 0.0, NEG_INF)

    for g in range(NSA_GROUPS):
        qp = qpad_scr[g]
        heads = slice(g * hq, (g + 1) * hq)
        kc_near = kc_ref[0, pl.ds(n0, LANES), :]
        vc_near = vc_ref[0, pl.ds(n0, LANES), :]
        s_far = _dot(kc_ref[0], qp)
        s_near = _dot(kc_near, qp) + tc_ref[:, heads]
        psum_far = jnp.zeros((rows, TQ), F32)
        psum_near = jnp.zeros((LANES, TQ), F32)
        p_far, p_near = [], []
        for c in range(NSA_HPG):
            sf = s_far[:, c * TQ:(c + 1) * TQ] + far_bias
            sn = s_near[:, c * TQ:(c + 1) * TQ] + near_bias
            mx = jnp.maximum(jnp.maximum(jnp.max(sf, axis=0, keepdims=True), jnp.max(sn, axis=0, keepdims=True)),
                             0.1 * NEG_INF)
            pf = jnp.exp2(sf - mx)
            pn = jnp.exp2(sn - mx)
            den = jnp.sum(pf, axis=0, keepdims=True) + jnp.sum(pn, axis=0, keepdims=True)
            inv = 1.0 / jnp.maximum(den, 1e-30)
            pf = pf * inv
            pn = pn * inv
            psum_far += pf
            psum_near += pn
            p_far.append(pf.astype(BF16))
            p_near.append(pn.astype(BF16))
        res_scr[0, g] = (_dot_tn(vc_ref[0], jnp.concatenate(p_far, axis=1))
                         + _dot_tn(vc_near, jnp.concatenate(p_near, axis=1)))
        score = jnp.zeros((LANES, TQ), F32)
        for psum, imp in ((psum_far, imp_ref[...]), (psum_near, imp_ref[pl.ds(n0, LANES), :])):
            hi = psum.astype(BF16)
            lo = (psum - hi.astype(F32)).astype(BF16)
            score += _dot_tn(imp, hi) + _dot_tn(imp, lo)
        score = jnp.where(forced, BIG, jnp.where(allowed, score, -BIG))
        unpicked = jnp.ones((LANES, TQ), F32)
        for _ in range(N_SEL):
            best = jnp.max(score, axis=0, keepdims=True)
            first = jnp.min(jnp.where(score == best, blkf, float(LANES)), axis=0, keepdims=True)
            pick = blkf == first
            unpicked = jnp.where(pick, 0.0, unpicked)
            score = jnp.where(pick, -jnp.inf, score)
        notsel = jnp.maximum(jnp.where(allowed, 0.0, 1.0), unpicked).astype(BF16)
        for c in range(NSA_HPG):
            lhs_scr[:LANES, c * TQ:(c + 1) * TQ] = notsel
        lhs_scr[LANES:, :] = qp

        def sel_logits(kt):
            k0 = pl.multiple_of(kt * TQ, TQ)
            return _dot(jnp.concatenate([eneg_ref[pl.ds(k0, TQ), :], _key_tile(ks_ref, kt)], axis=1), lhs_scr[...])

        _flash_causal(
            qt, 1, sel_logits,
            lambda kt: None,
            lambda near, b: _near_bias(t_ref, g * NSA_HPG + b, near, 0),
            lambda kt: _key_tile(vs_ref, kt),
            *_flash_args(flash))
        res_scr[1, g] = acc_scr[...] / l_scr[...]

        @pl.when(qt >= 2)
        def _():
            k0 = pl.multiple_of((qt - 2) * TQ, TQ)
            s = _dot(kw_ref[0, pl.ds(k0, 3 * TQ), :], qp)
            probs, sums = [], []
            for c in range(NSA_HPG):
                blk = slice(c * TQ, (c + 1) * TQ)
                hblk = slice((g * NSA_HPG + c) * TQ, (g * NSA_HPG + c + 1) * TQ)
                sb = s[:, blk] + jnp.concatenate([win_edge, t_ref[1, :, hblk], t_ref[0, :, hblk]], axis=0)
                p = jnp.exp2(sb - jnp.max(sb, axis=0, keepdims=True))
                sums.append(jnp.sum(p, axis=0, keepdims=True))
                probs.append(p.astype(BF16))
            out = _dot_tn(vw_ref[0, pl.ds(k0, 3 * TQ), :], jnp.concatenate(probs, axis=1))
            res_scr[2, g] = out / jnp.concatenate(sums, axis=1)

        @pl.when(qt < 2)
        def _():
            _flash_init(m_scr, l_scr, acc_scr)

            def win_step(kt, near):
                s = _dot(_key_tile(kw_ref, kt), qp) + t_ref[near, :, heads]
                _flash_update(s, _key_tile(vw_ref, kt), m_scr, l_scr, acc_scr)

            pl.when(qt >= 1)(functools.partial(win_step, qt - 1, 1))
            win_step(qt, 0)
            res_scr[2, g] = acc_scr[...] / l_scr[...]

    for c in range(NSA_HPG):
        out = jnp.zeros((TQ, LANES), F32)
        for br in range(3):
            gate = gate_ref[0, :, br * NSA_HEADS * HEAD_DIM + c * LANES:br * NSA_HEADS * HEAD_DIM + (c + 1) * LANES]
            both = jnp.concatenate([res_scr[br, 0, :HEAD_DIM, c * TQ:(c + 1) * TQ],
                                    res_scr[br, 1, HEAD_DIM:, c * TQ:(c + 1) * TQ]], axis=0)
            out += gate * both.T
        o_ref[0, :, c * LANES:(c + 1) * LANES] = out.astype(o_ref.dtype)


def _nsa_call(q, gates, kc, vc, kvs, kvw, imp, eneg, t_tiles, tc_tiles):
    bsz, s, _ = q.shape
    rows = kc.shape[1]
    hq = NSA_HPG * TQ
    whole = lambda col: pl.BlockSpec((1, s, LANES), lambda b, i: (b, 0, col))
    return pl.pallas_call(
        _nsa_kernel,
        out_shape=jax.ShapeDtypeStruct((bsz, s, NSA_HEADS * HEAD_DIM), BF16),
        grid=(bsz, s // TQ),
        in_specs=[
            pl.BlockSpec((1, TQ, NSA_HEADS * HEAD_DIM), lambda b, i: (b, i, 0)),
            pl.BlockSpec((1, TQ, 3 * NSA_HEADS * HEAD_DIM), lambda b, i: (b, i, 0)),
            pl.BlockSpec((1, rows, LANES), lambda b, i: (b, 0, 0)),
            pl.BlockSpec((1, rows, LANES), lambda b, i: (b, 0, 0)),
            whole(0), whole(1), whole(0), whole(1),
            _const_spec(imp.shape),
            _const_spec(eneg.shape),
            pl.BlockSpec((2, TQ, NSA_HEADS * TQ), lambda b, i: (0, 0, 0)),
            pl.BlockSpec((LANES, NSA_HEADS * TQ), lambda b, i: (0, 0)),
        ],
        out_specs=pl.BlockSpec((1, TQ, NSA_HEADS * HEAD_DIM), lambda b, i: (b, i, 0)),
        scratch_shapes=[
            pltpu.VMEM((NSA_GROUPS, LANES, hq), BF16),
            pltpu.VMEM((2 * LANES, hq), BF16),
            pltpu.VMEM((3, NSA_GROUPS, LANES, hq), F32),
        ] + _flash_scratch(hq),
        compiler_params=_params("parallel", "arbitrary"),
        name="nsa_attention",
    )(q, gates, kc, vc, kvs, kvs, kvw, kvw, imp, eneg, t_tiles, tc_tiles)


def _sort_key(x):
    b = lax.bitcast_convert_type(x, jnp.int32)
    return b ^ ((b >> 31) & jnp.int32(0x7FFFFFFF))


def _dsa_kernel(topk, dq_ref, ckv_ref, iq_ref, ik_ref, iw_ref, wuk_ref, wuv_ref, t_ref, o_ref,
                key_scr, hi_scr, low_scr, lhs_scr, thr_scr, *flash):
    qt = pl.program_id(1)
    nh = DSA_HEADS
    key_i = lax.broadcasted_iota(jnp.int32, (TQ, TQ), 0)
    qry_i = lax.broadcasted_iota(jnp.int32, (TQ, TQ), 1)
    int_min = jnp.int32(-2 ** 31)

    for c in range(IDX_HEADS // 2):
        col_t = iq_ref[0, :, c * LANES:(c + 1) * LANES].astype(F32).T.astype(BF16)
        for hl in range(2):
            j = 2 * c + hl
            lhs_scr[:IDX_DIM, j * TQ:(j + 1) * TQ] = col_t[hl * IDX_DIM:(hl + 1) * IDX_DIM]
    iw_t = iw_ref[0].T

    def idx_tile(kt, causal):
        s = _dot(_key_tile(ik_ref, kt)[:, :IDX_DIM], lhs_scr[:IDX_DIM, :])
        isc = jnp.zeros((TQ, TQ), F32)
        for j in range(IDX_HEADS):
            isc += iw_t[j:j + 1, :] * jnp.maximum(s[:, j * TQ:(j + 1) * TQ], 0.0)
        if causal:
            isc = jnp.where(key_i <= qry_i, isc, NEG_INF)
        k0 = pl.multiple_of(kt * TQ, TQ)
        zero_key = -2 - (k0 + key_i)
        key = jnp.where(jnp.abs(isc) < MIN_NORMAL, zero_key, _sort_key(isc))
        key_scr[pl.ds(k0, TQ), :] = key
        hi_scr[pl.ds(k0, TQ), :] = (key >> 16).astype(jnp.int16)

    pl.when((qt & 1) == 1)(functools.partial(idx_tile, 0, False))

    def idx_pair(j, carry):
        idx_tile((qt & 1) + 2 * j, False)
        idx_tile((qt & 1) + 2 * j + 1, False)
        return carry

    lax.fori_loop(0, qt // 2, idx_pair, 0)
    idx_tile(qt, True)

    @pl.when((qt & 1) == 0)
    def _():
        pad = pl.ds(pl.multiple_of((qt + 1) * TQ, TQ), TQ)
        key_scr[pad, :] = jnp.full((TQ, TQ), int_min, jnp.int32)
        hi_scr[pad, :] = jnp.full((TQ, TQ), -HALF_RANGE, jnp.int16)

    def slab(kp):
        return pl.ds(pl.multiple_of(kp * (2 * TQ), 2 * TQ), 2 * TQ)

    def count(pred_fn):
        def body(kp, cnt):
            hit = jnp.where(pred_fn(key_scr[slab(kp), :], kp * (2 * TQ)), 1.0, 0.0)
            return cnt + jnp.sum(hit.reshape(2 * TQ // COUNT_ROWS, COUNT_ROWS, TQ), axis=0)

        cnt = lax.fori_loop(0, (qt + 2) // 2, body, jnp.zeros((COUNT_ROWS, TQ), F32))
        return jnp.sum(cnt, axis=0, keepdims=True)

    def count_half(src_scr, cand):
        cand16 = cand.astype(jnp.int16)

        def body(kp, cnt):
            hit = jnp.where(src_scr[slab(kp), :] >= cand16, jnp.bfloat16(1), jnp.bfloat16(0))
            for i in range(2 * TQ // COUNT_ROWS):
                cnt = cnt + hit[i * COUNT_ROWS:(i + 1) * COUNT_ROWS]
            return cnt

        cnt = lax.fori_loop(0, (qt + 2) // 2, body, jnp.zeros((COUNT_ROWS, TQ), BF16))
        return jnp.sum(cnt.astype(F32), axis=0, keepdims=True)

    kf = float(topk)
    n_all = (qt + 1).astype(F32) * TQ
    n_pos = count_half(hi_scr, jnp.zeros((1, TQ), jnp.int32))
    hi0 = jnp.where(n_pos >= kf, 0, -HALF_RANGE)

    def hi_body(i, carry):
        hi, n_ge = carry
        cand = hi | (jnp.int32(1) << (14 - i))
        n_cand = count_half(hi_scr, cand)
        ok = n_cand >= kf
        return jnp.where(ok, cand, hi), jnp.where(ok, n_cand, n_ge)

    thr_hi, n_ge = lax.fori_loop(0, 15, hi_body, (hi0, jnp.where(n_pos >= kf, n_pos, n_all)))

    def remap(kp, carry):
        key = key_scr[slab(kp), :]
        hi = key >> 16
        low = (key & (2 * HALF_RANGE - 1)) - HALF_RANGE
        low = jnp.where(hi > thr_hi, HALF_RANGE - 1, jnp.where(hi < thr_hi, -HALF_RANGE, low))
        low_scr[slab(kp), :] = low.astype(jnp.int16)
        return carry

    lax.fori_loop(0, (qt + 2) // 2, remap, 0)

    def low_body(i, carry):
        low, n_ge = carry
        cand = low | (jnp.int32(1) << (15 - i))
        n_cand = count_half(low_scr, cand - HALF_RANGE)
        ok = n_cand >= kf
        return jnp.where(ok, cand, low), jnp.where(ok, n_cand, n_ge)

    thr_low, n_ge = lax.fori_loop(0, 16, low_body, (jnp.zeros((1, TQ), jnp.int32), n_ge))
    thr = (thr_hi << 16) | thr_low
    thr_scr[...] = thr

    excess = jnp.max(jnp.where(n_ge > kf, 1.0, 0.0))

    @pl.when(excess > 0.0)
    def _():
        need = kf - count(lambda tile, k0: tile > thr)
        pos_i = lax.broadcasted_iota(jnp.int32, (2 * TQ, TQ), 0)

        def pos_body(i, bound):
            cand = bound | (jnp.int32(1) << (30 - i))
            below = count(lambda tile, k0: (tile == thr) & (k0 + pos_i < cand))
            return jnp.where(below <= need - 1.0, cand, bound)

        bound = lax.fori_loop(0, 31, pos_body, jnp.zeros((1, TQ), jnp.int32))

        def demote(kt, carry):
            k0 = pl.multiple_of(kt * TQ, TQ)
            tile = key_scr[pl.ds(k0, TQ), :]
            key_scr[pl.ds(k0, TQ), :] = jnp.where((tile == thr) & (k0 + key_i > bound), tile - 1, tile)
            return carry

        lax.fori_loop(0, qt + 1, demote, 0)

    for c in range(nh // 2):
        col_t = dq_ref[0, :, c * LANES:(c + 1) * LANES].astype(F32).T
        for hl in range(2):
            h = 2 * c + hl
            qm = jnp.where(_half_rows(hl, TQ), col_t, 0.0).astype(BF16)
            lhs_scr[:, h * TQ:(h + 1) * TQ] = (_dot(wuk_ref[c], qm) * (HEAD_DIM ** -0.5 * LOG2E)).astype(BF16)

    _flash_causal(
        qt, 1,
        lambda kt: _dot(_key_tile(ckv_ref, kt), lhs_scr[...]),
        lambda kt: jnp.where(key_scr[pl.ds(pl.multiple_of(kt * TQ, TQ), TQ), :] >= thr_scr[...], 0.0, NEG_INF),
        lambda near, b: _near_bias(t_ref, b, near, 0),
        lambda kt: _key_tile(ckv_ref, kt),
        *_flash_args(flash))

    o_lat = (flash[2][...] / flash[1][...]).astype(BF16)
    for c in range(nh // 2):
        out = (_dot(wuv_ref[2 * c], o_lat[:, (2 * c) * TQ:(2 * c + 1) * TQ])
               + _dot(wuv_ref[2 * c + 1], o_lat[:, (2 * c + 1) * TQ:(2 * c + 2) * TQ]))
        o_ref[0, :, c * LANES:(c + 1) * LANES] = out.T.astype(o_ref.dtype)


def _dsa_call(dq, ckv, iq, ik2, iw, wuk, wuv, t_tiles, topk):
    bsz, s, _ = dq.shape
    nh = DSA_HEADS
    hq = nh * TQ
    return pl.pallas_call(
        functools.partial(_dsa_kernel, topk),
        out_shape=jax.ShapeDtypeStruct((bsz, s, nh * HEAD_DIM), BF16),
        grid=(bsz, s // TQ),
        in_specs=[
            pl.BlockSpec((1, TQ, nh * HEAD_DIM), lambda b, i: (b, i, 0)),
            pl.BlockSpec((1, s, LANES), lambda b, i: (b, 0, 0)),
            pl.BlockSpec((1, TQ, IDX_HEADS * IDX_DIM), lambda b, i: (b, i, 0)),
            pl.BlockSpec((1, s, LANES), lambda b, i: (b, 0, 0)),
            pl.BlockSpec((1, TQ, LANES), lambda b, i: (b, i, 0)),
            _const_spec(wuk.shape),
            _const_spec(wuv.shape),
            pl.BlockSpec((2, TQ, hq), lambda b, i: (0, 0, 1)),
        ],
        out_specs=pl.BlockSpec((1, TQ, nh * HEAD_DIM), lambda b, i: (b, i, 0)),
        scratch_shapes=[
            pltpu.VMEM((s + TQ, TQ), jnp.int32),
            pltpu.VMEM((s + TQ, TQ), jnp.int16),
            pltpu.VMEM((s + TQ, TQ), jnp.int16),
            pltpu.VMEM((LANES, hq), BF16),
            pltpu.VMEM((1, TQ), jnp.int32),
        ] + _flash_scratch(hq),
        compiler_params=_params("parallel", "arbitrary"),
        name="dsa_attention",
    )(dq, ckv, iq, ik2, iw, wuk, wuv, t_tiles)


_NSA_Q = NSA_HEADS * HEAD_DIM
_NSA_KV = 2 * NSA_GROUPS * HEAD_DIM
_EVEN_SIZES = (_NSA_Q, _NSA_KV, _NSA_KV, _NSA_KV, NSA_HEADS * 3, DSA_HEADS * HEAD_DIM, DSA_KV_RANK,
               IDX_HEADS * IDX_DIM, IDX_DIM, IDX_HEADS)
_EVEN_OFF = np.concatenate([[0], np.cumsum(_EVEN_SIZES)])
_ZERO_COL = int(_EVEN_OFF[-1])


def _nsa_head_perm():
    pos = np.zeros(_NSA_Q, np.int64)
    for c in range(NSA_HPG):
        for g in range(NSA_GROUPS):
            for d in range(HEAD_DIM):
                pos[c * LANES + g * HEAD_DIM + d] = (g * NSA_HPG + c) * HEAD_DIM + d
    return pos


def _even_columns():
    o = _EVEN_OFF
    perm = _nsa_head_perm()
    gate = np.concatenate([o[4] + (perm // HEAD_DIM) * 3 + br for br in range(3)])
    idx_w = np.concatenate([np.arange(o[9], o[10]), np.full(LANES - IDX_HEADS, _ZERO_COL)])
    cols = [
        o[0] + perm,
        np.arange(o[1], o[2]), np.arange(o[2], o[3]), np.arange(o[3], o[4]),
        gate,
        np.arange(o[5], o[6]), np.arange(o[6], o[7]), np.arange(o[7], o[8]),
        np.concatenate([np.arange(o[8], o[9])] * 2),
        idx_w,
    ]
    segs = (
        (_NSA_Q, BF16, None), (_NSA_KV, F32, None), (_NSA_KV, BF16, None), (_NSA_KV, BF16, None),
        (3 * _NSA_Q, F32, "sigmoid"),
        (DSA_HEADS * HEAD_DIM, BF16, None), (DSA_KV_RANK, BF16, "rms"), (IDX_HEADS * IDX_DIM, BF16, None),
        (LANES, BF16, None), (LANES, F32, float(IDX_HEADS ** -0.5 * IDX_DIM ** -0.5)),
    )
    return np.concatenate(cols).astype(np.int32), segs


def _block_diag2(w):
    z = jnp.zeros_like(w)
    return jnp.concatenate([jnp.concatenate([w, z], axis=-1), jnp.concatenate([z, w], axis=-1)], axis=-2)


def _selection_constants(s):
    nc = s // CMP_STRIDE
    nb = s // SEL_BLK
    rows = _cmp_rows(s)
    cs = np.arange(nc) * CMP_STRIDE
    ss = np.arange(nb) * SEL_BLK
    overlap = np.clip(np.minimum(cs[:, None] + CMP_LEN, ss[None, :] + SEL_BLK) - np.maximum(cs[:, None], ss[None, :]), 0, None)
    imp = np.zeros((rows, LANES), np.float32)
    imp[CMP_PAD:CMP_PAD + nc, :nb] = overlap / CMP_STRIDE
    imp[CMP_PAD + nc - 1] = 0.0
    eneg = np.zeros((s, LANES), np.float32)
    eneg[np.arange(s), np.arange(s) // SEL_BLK] = NEG_INF
    return jnp.asarray(imp, BF16), jnp.asarray(eneg, BF16)


def kernel(x, c, rel_bias, ada_w, ada_b, ln_g, ln_b, ev_w_in, ev_w_out, nsa_pe_k, nsa_pe_v, nsa_w1_k, nsa_w2_k, nsa_w1_v, nsa_w2_v, dsa_kv_norm, dsa_w_uk, dsa_w_uv, od_w_in, od_w_out, diff_lam, diff_subln, mlp_w1, mlp_w2):
    bsz, s, d = x.shape
    depth = ada_w.shape[0]
    assert s % TQ == 0 and s // SEL_BLK <= LANES
    alpha = (2 * depth) ** 0.25
    topk = min(DSA_TOPK, s // 4)

    ada = _ada_call(c, ada_w, ada_b)
    t_tiles, tc_tiles = _bias_call(rel_bias)
    imp, eneg = _selection_constants(s)
    even_cols, even_segs = _even_columns()
    perm = _nsa_head_perm()
    odd_segs = ((od_w_in.shape[-1], BF16, None),)
    unit_gain = jnp.ones((1, LANES), F32)

    for l in range(depth):
        sh1, sc1, g1, sh2, sc2, g2 = [a.reshape(bsz, 1, d) for a in jnp.split(ada[l], 6, axis=-1)]
        i = l // 2
        if l % 2 == 0:
            w_aug = jnp.concatenate([ev_w_in[i], jnp.zeros((d, 1), F32)], axis=1)
            w_in = jnp.take(w_aug, even_cols, axis=1).astype(BF16)
            nsa_q, kvc, kvs, kvw, gates, dsa_q, ckv, idx_q, ik2, idx_w = _proj_call(
                x, sh1, sc1, w_in, dsa_kv_norm[i].reshape(1, LANES), even_segs)
            pe2 = jnp.stack([jnp.tile(nsa_pe_k[i], (1, 2)), jnp.tile(nsa_pe_v[i], (1, 2))])
            w1bd = jnp.stack([_block_diag2(w.reshape(CMP_LEN, HEAD_DIM, CMP_HIDDEN)) for w in (nsa_w1_k[i], nsa_w1_v[i])]).astype(BF16)
            w2bd = jnp.stack([_block_diag2(nsa_w2_k[i]), _block_diag2(nsa_w2_v[i])]).astype(BF16)
            kc, vc = _compress_call(kvc, pe2, w1bd, w2bd)
            o_a = _nsa_call(nsa_q, gates, kc, vc, kvs, kvw, imp, eneg, t_tiles, tc_tiles)
            wuk = dsa_w_uk[i].reshape(DSA_KV_RANK, DSA_HEADS // 2, LANES).transpose(1, 0, 2).astype(BF16)
            wuv_h = dsa_w_uv[i].transpose(1, 0, 2)
            zeros = jnp.zeros_like(wuv_h)
            even_h = (np.arange(DSA_HEADS) % 2 == 0)[:, None, None]
            wuv = jnp.where(even_h, jnp.concatenate([wuv_h, zeros], -1), jnp.concatenate([zeros, wuv_h], -1))
            wuv = wuv.transpose(0, 2, 1).astype(BF16)
            o_b = _dsa_call(dsa_q, ckv, idx_q, ik2, idx_w, wuk, wuv, t_tiles, topk)
            attn = jnp.concatenate([o_a, o_b], axis=-1)
            w_out = jnp.concatenate([ev_w_out[i][perm], ev_w_out[i][_NSA_Q:]], axis=0).astype(BF16)
        else:
            lambda_init = 0.8 - 0.6 * math.exp(-0.3 * l)
            (qkv,) = _proj_call(x, sh1, sc1, od_w_in[i].astype(BF16), unit_gain, odd_segs)
            attn = _diff_call(qkv, t_tiles, diff_lam[i], diff_subln[i].reshape(1, LANES), lambda_init)
            w_out = od_w_out[i].astype(BF16)
        x = _out_call(x, attn, w_out, g1, ln_g[l, 0].reshape(1, d), ln_b[l, 0].reshape(1, d), alpha)
        x = _mlp_call(x, sh2, sc2, g2, mlp_w1[l].astype(BF16), mlp_w2[l].astype(BF16),
                      ln_g[l, 1].reshape(1, d), ln_b[l, 1].reshape(1, d), alpha)
    return x
```
